```python
import jax, jax.numpy as jnp
from jax import lax
import numpy as np

D_MODEL = 2048
BATCH = 32
SEQ = 256
DEPTH = 4
DEC_BATCH = 2
DEC_SEQ = 2048
PAST_LEN = 256

GRID_W = 64
CHUNK = 128
ATTN_BLOCK = 128
N_BRANCH = 3
BRANCH_W = D_MODEL
D_A = BRANCH_W
GMLP_GW = 128
GMLP_GROUPS = D_A // GMLP_GW
V_DIM = 128
NOPE_DIM = 128
ROPE_DIM = 64
QK_DIM = NOPE_DIM + ROPE_DIM
MLA_HEADS = BRANCH_W // V_DIM
Q_LORA = 512
KV_LORA = 512
ROPE_THETA = 10000.0
D_C = BRANCH_W
RWKV_HEAD = 64
RWKV_HEADS = D_C // RWKV_HEAD
DECAY_LORA = 96
AAA_LORA = 96
GATE_LORA = 256
GN_EPS = 64e-5
N_EXPERTS = 32
TOP_K = 4
EXPERT_FF = D_MODEL
SWIGLU_LIMIT = 7.0
SWIGLU_ALPHA = 1.702
MOE_BLOCK = 128
LN_EPS = 1e-5
RMS_EPS = 1e-6
ALPHA = (2 * DEPTH) ** 0.25
BETA = (8 * DEPTH) ** -0.25
GMLP_COLS = 2 * D_A
MLA_COLS = Q_LORA + KV_LORA + ROPE_DIM
RWKV_COLS = 3 * D_C + 2 * DECAY_LORA + 2 * AAA_LORA + GATE_LORA
GATE_COLS = N_BRANCH * D_MODEL
N_IN = GMLP_COLS + MLA_COLS + RWKV_COLS + GATE_COLS
IN_SPLITS = [GMLP_COLS, GMLP_COLS + MLA_COLS, GMLP_COLS + MLA_COLS + RWKV_COLS]
RWKV_SPLITS = [D_C, 2 * D_C, 3 * D_C, 3 * D_C + DECAY_LORA, 3 * D_C + 2 * DECAY_LORA,
               3 * D_C + 2 * DECAY_LORA + AAA_LORA, 3 * D_C + 2 * DECAY_LORA + 2 * AAA_LORA]

kernel_name = 'hybrid_diffusion_gmlp_mla_rwkv7_moe_step'

F32 = jnp.float32


def layer_norm(x, g=None, b=None):
    xf = x.astype(F32)
    mu = jnp.mean(xf, -1, keepdims=True)
    var = jnp.mean(jnp.square(xf - mu), -1, keepdims=True)
    y = (xf - mu) * lax.rsqrt(var + LN_EPS)
    if g is not None:
        y = y * g.astype(F32) + b.astype(F32)
    return y.astype(x.dtype)


def rms_norm(x, g):
    xf = x.astype(F32)
    y = xf * lax.rsqrt(jnp.mean(xf * xf, -1, keepdims=True) + RMS_EPS)
    return (y * g.astype(F32)).astype(x.dtype)


def axial_rope_tables(n):
    rows = n // GRID_W
    row_id = jnp.repeat(jnp.arange(rows, dtype=F32), GRID_W)
    col_id = jnp.tile(jnp.arange(GRID_W, dtype=F32), rows)
    half = ROPE_DIM // 2
    inv_freq = ROPE_THETA ** (-jnp.arange(0, half, 2, dtype=F32) / half)
    ang_r = row_id[:, None] * inv_freq
    ang_c = col_id[:, None] * inv_freq
    return (jnp.cos(ang_r), jnp.sin(ang_r), jnp.cos(ang_c), jnp.sin(ang_c))


def _rotate(x, cos, sin):
    h = x.shape[-1] // 2
    x1, x2 = x[..., :h], x[..., h:]
    return jnp.concatenate([x1 * cos - x2 * sin, x1 * sin + x2 * cos], axis=-1)


def apply_axial_rope(x, rope):
    cr, sr, cc, sc = (z.reshape(z.shape[0], *([1] * (x.ndim - 3)), z.shape[-1]) for z in rope)
    xf = x.astype(F32)
    half = ROPE_DIM // 2
    out = jnp.concatenate([_rotate(xf[..., :half], cr, sr), _rotate(xf[..., half:], cc, sc)], axis=-1)
    return out.astype(x.dtype)


def gmlp_mixer(p, prm, l):
    b, t, _ = p.shape
    u, v = jnp.split(jax.nn.gelu(p), 2, axis=-1)
    v = layer_norm(v, prm['gmlp_ln_g'][l], prm['gmlp_ln_b'][l])
    v = v.reshape(b, t // CHUNK, CHUNK, GMLP_GROUPS, GMLP_GW)
    mixed = jnp.einsum('gpq,bnqgc->bnpgc', prm['gmlp_ws'][l], v) + prm['gmlp_bs'][l].T[:, :, None]
    return u * mixed.reshape(b, t, D_A)


def mla_attend(q_nope, q_rope, k_nope, k_rope, v):
    b, t = q_nope.shape[:2]
    nblk = t // ATTN_BLOCK
    blocks = lambda z: jnp.swapaxes(z.reshape(b, nblk, ATTN_BLOCK, *z.shape[2:]), 0, 1)
    scale = QK_DIM ** -0.5

    def attend_block(qs):
        qn, qr = qs
        s = jnp.einsum('bqhd,bkhd->bhqk', qn, k_nope) + jnp.einsum('bqhr,bkr->bhqk', qr, k_rope)
        prob = jax.nn.softmax(s.astype(F32) * scale, axis=-1).astype(v.dtype)
        return jnp.einsum('bhqk,bkhd->bqhd', prob, v)

    o = lax.map(attend_block, (blocks(q_nope), blocks(q_rope)))
    return jnp.swapaxes(o, 0, 1).reshape(b, t, MLA_HEADS * V_DIM)


def mla_mixer(p, prm, l, ctx_ckv, ctx_kr, rope):
    b, t, _ = p.shape
    cq, ckv, kr = jnp.split(p, [Q_LORA, Q_LORA + KV_LORA], axis=-1)
    q = (rms_norm(cq, prm['q_norm_g'][l]) @ prm['w_uq'][l]).reshape(b, t, MLA_HEADS, QK_DIM)
    q_nope, q_rope = q[..., :NOPE_DIM], q[..., NOPE_DIM:]
    ckv = rms_norm(ckv, prm['kv_norm_g'][l])
    if rope is None:
        keys_c, keys_r = ckv, kr
    else:
        q_rope = apply_axial_rope(q_rope, rope)
        keys_c = jnp.concatenate([ctx_ckv.astype(ckv.dtype), ckv], axis=1)
        keys_r = jnp.concatenate([ctx_kr.astype(kr.dtype), apply_axial_rope(kr, rope)], axis=1)
    kv = (keys_c @ prm['w_ukv'][l]).reshape(b, keys_c.shape[1], MLA_HEADS, NOPE_DIM + V_DIM)
    o = mla_attend(q_nope, q_rope, kv[..., :NOPE_DIM], keys_r, kv[..., NOPE_DIM:])
    return o, ckv, kr


def wkv7_scan(r, w, k, v, a, bb, s0, reverse):
    xs = tuple(jnp.moveaxis(z.astype(F32), 1, 0) for z in (r, w, k, v, a, bb))

    def step(S, inp):
        r_t, w_t, k_t, v_t, a_t, b_t = inp
        sa = jnp.einsum('bhij,bhj->bhi', S, a_t)
        S = S * w_t[:, :, None, :] + sa[..., None] * b_t[:, :, None, :] + v_t[..., None] * k_t[:, :, None, :]
        return S, jnp.einsum('bhij,bhj->bhi', S, r_t)

    S, y = lax.scan(step, s0.astype(F32), xs, reverse=reverse)
    return jnp.moveaxis(y, 0, 1), S


def rwkv_mixer(p, prm, l, s_fwd, s_bwd):
    b, t, _ = p.shape
    zero = jnp.zeros_like(p[:, :1])
    prev = jnp.concatenate([zero, p[:, :-1]], axis=1)
    nxt = jnp.concatenate([p[:, 1:], zero], axis=1)
    p = p + (0.5 * (prev + nxt) - p) * prm['rwkv_mu'][l]
    r, k, v, wl_f, wl_b, al_f, al_b, gl = jnp.split(p, RWKV_SPLITS, axis=-1)
    heads = lambda z: z.reshape(b, t, RWKV_HEADS, RWKV_HEAD)
    kkf = heads((k * prm['rwkv_kk'][l]).astype(F32))
    kk = kkf / jnp.maximum(jnp.linalg.norm(kkf, axis=-1, keepdims=True), 1e-12)
    ys, finals = [], []
    for d, (wl, al, s0) in enumerate(((wl_f, al_f, s_fwd), (wl_b, al_b, s_bwd))):
        logw = -jax.nn.softplus(-(prm['rwkv_w0'][l, d] + jnp.tanh(wl) @ prm['rwkv_w2'][l, d]).astype(F32)) - 0.5
        decay = jnp.exp(-jnp.exp(logw))
        a = jax.nn.sigmoid((prm['rwkv_a0'][l, d] + al @ prm['rwkv_a2'][l, d]).astype(F32))
        k_d = k.astype(F32) * (1.0 + (a - 1.0) * prm['rwkv_ka'][l, d].astype(F32))
        y_d, s_d = wkv7_scan(heads(r), heads(decay), heads(k_d), heads(v), -kk, kk * heads(a), s0, reverse=(d == 1))
        ys.append(y_d)
        finals.append(s_d)
    y = ys[0] + ys[1]
    mu = jnp.mean(y, -1, keepdims=True)
    var = jnp.mean(jnp.square(y - mu), -1, keepdims=True)
    yn = ((y - mu) * lax.rsqrt(var + GN_EPS)).reshape(b, t, D_C) * prm['rwkv_lnx_g'][l].astype(F32) + prm['rwkv_lnx_b'][l].astype(F32)
    rh, kh, vh = heads(r.astype(F32)), heads(k.astype(F32)), heads(v.astype(F32))
    bonus = (jnp.sum(rh * kh * prm['rwkv_rk'][l].astype(F32), -1, keepdims=True) * vh).reshape(b, t, D_C)
    g = jax.nn.sigmoid(gl) @ prm['rwkv_g2'][l]
    return ((yn + bonus) * g).astype(p.dtype), finals[0], finals[1]


def token_mixers(h, prm, l, ctx, rope):
    b, t, _ = h.shape
    p_gm, p_mla, p_rw, p_gate = jnp.split(h @ prm['w_in'][l], IN_SPLITS, axis=-1)
    o_a = gmlp_mixer(p_gm, prm, l)
    if ctx is None:
        s0 = jnp.zeros((b, RWKV_HEADS, RWKV_HEAD, RWKV_HEAD), F32)
        o_b, ckv, kr = mla_mixer(p_mla, prm, l, None, None, None)
        o_c, s_f, s_b = rwkv_mixer(p_rw, prm, l, s0, s0)
        ctx_out = (ckv, kr, s_f, s_b)
    else:
        c_ckv, c_kr, c_sf, c_sb = ctx
        o_b, _, _ = mla_mixer(p_mla, prm, l, c_ckv, c_kr, rope)
        o_c, _, _ = rwkv_mixer(p_rw, prm, l, c_sf, c_sb)
        ctx_out = None
    o = jnp.stack([o_a, o_b, o_c], axis=2)
    y = jnp.einsum('btnc,ncd->btnd', o, prm['w_branch'][l])
    gates = jax.nn.sigmoid(p_gate).reshape(b, t, N_BRANCH, D_MODEL)
    return jnp.sum(gates * y, axis=2) @ prm['w_o'][l], ctx_out


def moe_ffn(h, prm, l):
    n, d = h.shape
    logits = (h @ prm['router_w'][l] + prm['router_b'][l]).astype(F32)
    top_val, top_idx = lax.top_k(logits, TOP_K)
    top_w = jax.nn.softmax(top_val, axis=-1).astype(h.dtype)
    flat_e = top_idx.reshape(-1).astype(jnp.int32)
    flat_t = jnp.repeat(jnp.arange(n, dtype=jnp.int32), TOP_K)
    order = jnp.argsort(flat_e)
    se, st, sw = flat_e[order], flat_t[order], top_w.reshape(-1)[order]
    counts = jnp.bincount(flat_e, length=N_EXPERTS)
    padded = (counts + MOE_BLOCK - 1) // MOE_BLOCK * MOE_BLOCK
    pad_end = jnp.cumsum(padded)
    pad_start = pad_end - padded
    grp_start = jnp.cumsum(counts) - counts
    dest = pad_start[se] + jnp.arange(n * TOP_K, dtype=jnp.int32) - grp_start[se]
    n_blocks = -(-(n * TOP_K) // MOE_BLOCK) + N_EXPERTS
    cap = n_blocks * MOE_BLOCK
    buf_t = jnp.full((cap,), n, jnp.int32).at[dest].set(st)
    buf_w = jnp.zeros((cap,), h.dtype).at[dest].set(sw)
    block_e = jnp.minimum(jnp.searchsorted(pad_end, jnp.arange(n_blocks, dtype=jnp.int32) * MOE_BLOCK, side='right'), N_EXPERTS - 1)
    xb = jnp.concatenate([h, jnp.zeros((1, d), h.dtype)], axis=0)[buf_t].reshape(n_blocks, MOE_BLOCK, d)
    w_gate, b_gate = prm['w_gate'][l], prm['b_gate'][l]
    w_up, b_up = prm['w_up'][l], prm['b_up'][l]
    w_down, b_down = prm['w_down'][l], prm['b_down'][l]

    def expert_block(args):
        xblk, e = args
        gl = jnp.minimum(xblk @ w_gate[e] + b_gate[e], SWIGLU_LIMIT)
        lin = jnp.clip(xblk @ w_up[e] + b_up[e], -SWIGLU_LIMIT, SWIGLU_LIMIT)
        act = gl * jax.nn.sigmoid(SWIGLU_ALPHA * gl) * (lin + 1.0)
        return act @ w_down[e] + b_down[e]

    yb = lax.map(expert_block, (xb, block_e)).reshape(cap, d)
    out = jnp.zeros((n + 1, d), h.dtype).at[buf_t].add(yb * buf_w[:, None])
    return out[:n]


def trunk_layer(x, cond, prm, l, ctx, rope):
    mod = cond @ prm['w_mod'][l] + prm['b_mod'][l]
    sh1, sc1, g1, sh2, sc2, g2 = (m[:, None, :] for m in jnp.split(mod, 6, axis=-1))
    h = layer_norm(x) * (1.0 + sc1) + sh1
    mix, ctx_out = token_mixers(h, prm, l, ctx, rope)
    x = layer_norm(ALPHA * x + g1 * mix, prm['ln1_g'][l], prm['ln1_b'][l])
    h = layer_norm(x) * (1.0 + sc2) + sh2
    b, t, d = h.shape
    ffn = moe_ffn(h.reshape(b * t, d), prm, l).reshape(b, t, d)
    x = layer_norm(ALPHA * x + g2 * ffn, prm['ln2_g'][l], prm['ln2_b'][l])
    return x, ctx_out


def setup_inputs(seed: int = 0) -> dict:
    key = jax.random.key(seed)
    ks = iter(jax.random.split(key, 64))
    L = DEPTH

    def nrm(shape, scale):
        return jax.random.normal(next(ks), shape, F32) * scale

    def gain(shape):
        return 1.0 + nrm(shape, 0.02)

    def unif(shape, lo, hi):
        return jax.random.uniform(next(ks), shape, F32, lo, hi)

    return {
        'x_prompt': nrm((BATCH, SEQ, D_MODEL), 1.0),
        'x_sample': nrm((DEC_BATCH, DEC_SEQ, D_MODEL), 1.0),
        'cache_ckv': nrm((DEC_BATCH, DEPTH, PAST_LEN, KV_LORA), 1.0),
        'cache_krope': nrm((DEC_BATCH, DEPTH, PAST_LEN, ROPE_DIM), 1.0),
        'state_rwkv_fwd': nrm((DEC_BATCH, DEPTH, RWKV_HEADS, RWKV_HEAD, RWKV_HEAD), 0.1),
        'state_rwkv_bwd': nrm((DEC_BATCH, DEPTH, RWKV_HEADS, RWKV_HEAD, RWKV_HEAD), 0.1),
        'c': nrm((DEC_BATCH, D_MODEL), 1.0),
        'c_ctx': nrm((D_MODEL,), 1.0),
        'w_mod': nrm((L, D_MODEL, 6 * D_MODEL), 0.5 * D_MODEL ** -0.5),
        'b_mod': nrm((L, 6 * D_MODEL), 0.01),
        'w_in': nrm((L, D_MODEL, N_IN), D_MODEL ** -0.5),
        'gmlp_ln_g': gain((L, D_A)),
        'gmlp_ln_b': nrm((L, D_A), 0.01),
        'gmlp_ws': nrm((L, GMLP_GROUPS, CHUNK, CHUNK), CHUNK ** -0.5),
        'gmlp_bs': 1.0 + nrm((L, GMLP_GROUPS, CHUNK), 0.1),
        'q_norm_g': gain((L, Q_LORA)),
        'w_uq': nrm((L, Q_LORA, MLA_HEADS * QK_DIM), Q_LORA ** -0.5),
        'kv_norm_g': gain((L, KV_LORA)),
        'w_ukv': nrm((L, KV_LORA, MLA_HEADS * (NOPE_DIM + V_DIM)), KV_LORA ** -0.5),
        'rwkv_mu': unif((L, RWKV_COLS), 0.0, 1.0),
        'rwkv_w0': unif((L, 2, D_C), -6.0, 1.0),
        'rwkv_w2': nrm((L, 2, DECAY_LORA, D_C), 0.1 * DECAY_LORA ** -0.5),
        'rwkv_a0': nrm((L, 2, D_C), 0.3),
        'rwkv_a2': nrm((L, 2, AAA_LORA, D_C), 0.1 * AAA_LORA ** -0.5),
        'rwkv_g2': nrm((L, GATE_LORA, D_C), GATE_LORA ** -0.5),
        'rwkv_kk': 0.85 + nrm((L, D_C), 0.05),
        'rwkv_ka': 1.0 + nrm((L, 2, D_C), 0.05),
        'rwkv_rk': nrm((L, RWKV_HEADS, RWKV_HEAD), 0.1),
        'rwkv_lnx_g': gain((L, D_C)),
        'rwkv_lnx_b': nrm((L, D_C), 0.01),
        'w_branch': nrm((L, N_BRANCH, BRANCH_W, D_MODEL), BETA * BRANCH_W ** -0.5),
        'w_o': nrm((L, D_MODEL, D_MODEL), BETA * D_MODEL ** -0.5),
        'ln1_g': gain((L, D_MODEL)),
        'ln1_b': nrm((L, D_MODEL), 0.01),
        'router_w': nrm((L, D_MODEL, N_EXPERTS), D_MODEL ** -0.5),
        'router_b': nrm((L, N_EXPERTS), 0.01),
        'w_gate': nrm((L, N_EXPERTS, D_MODEL, EXPERT_FF), D_MODEL ** -0.5),
        'b_gate': nrm((L, N_EXPERTS, EXPERT_FF), 0.01),
        'w_up': nrm((L, N_EXPERTS, D_MODEL, EXPERT_FF), D_MODEL ** -0.5),
        'b_up': nrm((L, N_EXPERTS, EXPERT_FF), 0.01),
        'w_down': nrm((L, N_EXPERTS, EXPERT_FF, D_MODEL), BETA * EXPERT_FF ** -0.5),
        'b_down': nrm((L, N_EXPERTS, D_MODEL), 0.01),
        'ln2_g': gain((L, D_MODEL)),
        'ln2_b': nrm((L, D_MODEL), 0.01),
    }


def reference(x_prompt, x_sample, cache_ckv, cache_krope, state_rwkv_fwd, state_rwkv_bwd, c, c_ctx,
              w_mod, b_mod, w_in, gmlp_ln_g, gmlp_ln_b, gmlp_ws, gmlp_bs, q_norm_g, w_uq, kv_norm_g, w_ukv,
              rwkv_mu, rwkv_w0, rwkv_w2, rwkv_a0, rwkv_a2, rwkv_g2, rwkv_kk, rwkv_ka, rwkv_rk, rwkv_lnx_g, rwkv_lnx_b,
              w_branch, w_o, ln1_g, ln1_b, router_w, router_b, w_gate, b_gate, w_up, b_up, w_down, b_down,
              ln2_g, ln2_b):
    prm = dict(w_mod=w_mod, b_mod=b_mod, w_in=w_in, gmlp_ln_g=gmlp_ln_g, gmlp_ln_b=gmlp_ln_b,
               gmlp_ws=gmlp_ws, gmlp_bs=gmlp_bs, q_norm_g=q_norm_g, w_uq=w_uq, kv_norm_g=kv_norm_g, w_ukv=w_ukv,
               rwkv_mu=rwkv_mu, rwkv_w0=rwkv_w0, rwkv_w2=rwkv_w2, rwkv_a0=rwkv_a0, rwkv_a2=rwkv_a2,
               rwkv_g2=rwkv_g2, rwkv_kk=rwkv_kk, rwkv_ka=rwkv_ka, rwkv_rk=rwkv_rk, rwkv_lnx_g=rwkv_lnx_g,
               rwkv_lnx_b=rwkv_lnx_b, w_branch=w_branch, w_o=w_o, ln1_g=ln1_g, ln1_b=ln1_b,
               router_w=router_w, router_b=router_b, w_gate=w_gate, b_gate=b_gate, w_up=w_up, b_up=b_up,
               w_down=w_down, b_down=b_down, ln2_g=ln2_g, ln2_b=ln2_b)

    x = x_prompt
    cond_ctx = jax.nn.silu(c_ctx)[None, :]
    ckv_l, kr_l, sf_l, sb_l = [], [], [], []
    for l in range(DEPTH):
        x, (ckv, kr, s_f, s_b) = trunk_layer(x, cond_ctx, prm, l, None, None)
        ckv_l.append(ckv)
        kr_l.append(kr)
        sf_l.append(s_f)
        sb_l.append(s_b)
    y_prompt = x

    x = x_sample
    cond = jax.nn.silu(c)
    rope = axial_rope_tables(x.shape[1])
    for l in range(DEPTH):
        ctx = (cache_ckv[:, l], cache_krope[:, l], state_rwkv_fwd[:, l], state_rwkv_bwd[:, l])
        x, _ = trunk_layer(x, cond, prm, l, ctx, rope)
    y_sample = x

    new_cache_ckv = jnp.stack(ckv_l, axis=1)
    new_cache_krope = jnp.stack(kr_l, axis=1)
    new_state_rwkv_fwd = jnp.stack(sf_l, axis=1)
    new_state_rwkv_bwd = jnp.stack(sb_l, axis=1)
    return (y_prompt, y_sample, new_cache_ckv, new_cache_krope, new_state_rwkv_fwd, new_state_rwkv_bwd)
```

```python
import functools
import math

import jax
import jax.numpy as jnp
from jax import lax
from jax.experimental import pallas as pl
from jax.experimental.pallas import tpu as pltpu

F32 = jnp.float32
BF16 = jnp.bfloat16
HI = lax.Precision.HIGHEST

DEPTH_NORM = 4
GRID_W = 64
CHUNK = 128
D = 2048
GMLP_GW = 128
GMLP_GROUPS = D // GMLP_GW
V_DIM = 128
NOPE = 128
ROPE = 64
QK_DIM = NOPE + ROPE
HEADS = D // V_DIM
Q_LORA = 512
KV_LORA = 512
ROPE_THETA = 10000.0
RH = 64
RHEADS = D // RH
DECAY_LORA = 96
AAA_LORA = 96
GATE_LORA = 256
GN_EPS = 64e-5
N_EXPERTS = 32
TOP_K = 4
SWIGLU_LIMIT = 7.0
SWIGLU_ALPHA = 1.702
LN_EPS = 1e-5
RMS_EPS = 1e-6
ALPHA = (2 * DEPTH_NORM) ** 0.25

LANES = 128
VMEM_LIMIT = 48 * 1024 * 1024

TB = 256
RB = 128
SC = 64
HG = 4
BM = 256
LORA_PAD = 128
RW_COLS = 3 * D + 4 * LORA_PAD + GATE_LORA
MLA_COLS = Q_LORA + KV_LORA + 2 * ROPE


def _cp(*sem):
    return pltpu.CompilerParams(dimension_semantics=sem, vmem_limit_bytes=VMEM_LIMIT)


def _ln(x):
    mu = jnp.mean(x, -1, keepdims=True)
    xc = x - mu
    var = jnp.mean(xc * xc, -1, keepdims=True)
    return xc * lax.rsqrt(var + LN_EPS)


def _sigmoid(x):
    return 1.0 / (1.0 + jnp.exp(-x))


def _gelu_tanh(x):
    return 0.5 * x * (1.0 + jnp.tanh(math.sqrt(2.0 / math.pi) * (x + 0.044715 * (x * x * x))))


def _mod_kernel(c_ref, w_ref, b_ref, o_ref):
    c = c_ref[...]
    s = (c * _sigmoid(c)).astype(BF16)
    o_ref[...] = jnp.dot(s, w_ref[...].astype(BF16), preferred_element_type=F32) + b_ref[...]


def modulation(cond8, w_mod, b_mod):
    L = w_mod.shape[0]
    tn = 1024
    return pl.pallas_call(
        _mod_kernel,
        grid=(L, 6 * D // tn),
        in_specs=[pl.BlockSpec((8, D), lambda l, j: (0, 0)),
                  pl.BlockSpec((None, D, tn), lambda l, j: (l, 0, j)),
                  pl.BlockSpec((None, 1, tn), lambda l, j: (l, 0, j))],
        out_specs=pl.BlockSpec((None, 8, tn), lambda l, j: (l, 0, j)),
        out_shape=jax.ShapeDtypeStruct((L, 8, 6 * D), F32),
        compiler_params=_cp("arbitrary", "arbitrary"),
        name="modulation",
    )(cond8, w_mod, b_mod.reshape(L, 1, 6 * D))


def _lnmod_kernel(x_ref, mod_ref, h_ref):
    m = mod_ref[...]
    h_ref[...] = (_ln(x_ref[...]) * (1.0 + m[1:2]) + m[0:1]).astype(BF16)


def ln_modulate(x, mod, modrow):
    n = x.shape[0]
    return pl.pallas_call(
        _lnmod_kernel,
        grid=(n // TB,),
        in_specs=[pl.BlockSpec((TB, D), lambda i: (i, 0)),
                  pl.BlockSpec((None, 6, D), lambda i: (modrow(i), 0, 0))],
        out_specs=pl.BlockSpec((TB, D), lambda i: (i, 0)),
        out_shape=jax.ShapeDtypeStruct((n, D), BF16),
        compiler_params=_cp("arbitrary"),
        name="ln_modulate",
    )(x, mod)


def _mm_kernel(x_ref, w_ref, o_ref):
    o_ref[...] = jnp.dot(x_ref[...], w_ref[...], preferred_element_type=F32).astype(o_ref.dtype)


def matmul(x, w, tn, name, out_dtype=F32):
    m, k = x.shape
    nn = w.shape[1]
    tm = 512 if m % 512 == 0 else TB
    return pl.pallas_call(
        _mm_kernel,
        grid=(nn // tn, m // tm),
        in_specs=[pl.BlockSpec((tm, k), lambda j, i: (i, 0)),
                  pl.BlockSpec((k, tn), lambda j, i: (0, j))],
        out_specs=pl.BlockSpec((tm, tn), lambda j, i: (i, j)),
        out_shape=jax.ShapeDtypeStruct((m, nn), out_dtype),
        compiler_params=_cp("arbitrary", "arbitrary"),
        name=name,
    )(x, w)


def _gmlp_kernel(p_ref, g_ref, b_ref, ws_ref, bs_ref, o_ref):
    u = _gelu_tanh(p_ref[:, :D])
    v = _gelu_tanh(p_ref[:, D:])
    v = (_ln(v) * g_ref[...] + b_ref[...]).astype(BF16)
    for g in range(GMLP_GROUPS):
        sl = slice(g * GMLP_GW, (g + 1) * GMLP_GW)
        mixed = jnp.dot(ws_ref[g], v[:, sl], preferred_element_type=F32) + bs_ref[:, sl]
        o_ref[:, sl] = (u[:, sl] * mixed).astype(BF16)


def gmlp(p_gm, ln_g, ln_b, ws, bs_full):
    n = p_gm.shape[0]
    return pl.pallas_call(
        _gmlp_kernel,
        grid=(n // CHUNK,),
        in_specs=[pl.BlockSpec((CHUNK, 2 * D), lambda i: (i, 0)),
                  pl.BlockSpec((1, D), lambda i: (0, 0)),
                  pl.BlockSpec((1, D), lambda i: (0, 0)),
                  pl.BlockSpec((GMLP_GROUPS, CHUNK, CHUNK), lambda i: (0, 0, 0)),
                  pl.BlockSpec((CHUNK, D), lambda i: (0, 0))],
        out_specs=pl.BlockSpec((CHUNK, D), lambda i: (i, 0)),
        out_shape=jax.ShapeDtypeStruct((n, D), BF16),
        compiler_params=_cp("arbitrary"),
        name="gmlp",
    )(p_gm, ln_g, ln_b, ws, bs_full)


def _qproj_kernel(cq_ref, g_ref, w_ref, tab_ref, q_ref):
    cq = cq_ref[...]
    n = (cq * lax.rsqrt(jnp.mean(cq * cq, -1, keepdims=True) + RMS_EPS) * g_ref[...]).astype(BF16)
    tab = tab_ref[...]
    hw = 2 * LANES
    for h in range(HEADS):
        sl = slice(h * hw, (h + 1) * hw)
        q_ref[:, sl] = (jnp.dot(n, w_ref[:, sl], preferred_element_type=F32) * tab).astype(BF16)


def q_proj(p_mla, g, wq, tabq):
    n = p_mla.shape[0]
    hw = 2 * LANES
    return pl.pallas_call(
        _qproj_kernel,
        grid=(n // TB,),
        in_specs=[pl.BlockSpec((TB, Q_LORA), lambda i: (i, 0)),
                  pl.BlockSpec((1, Q_LORA), lambda i: (0, 0)),
                  pl.BlockSpec((Q_LORA, HEADS * hw), lambda i: (0, 0)),
                  pl.BlockSpec((TB, hw), lambda i: (i, 0))],
        out_specs=pl.BlockSpec((TB, HEADS * hw), lambda i: (i, 0)),
        out_shape=jax.ShapeDtypeStruct((n, HEADS * hw), BF16),
        compiler_params=_cp("arbitrary"),
        name="mla_q_proj",
    )(p_mla, g, wq, tabq)


def _kvproj_kernel(ckv_ref, kr_ref, g_ref, wk_ref, wv_ref, tab_ref, kcat_ref, v_ref, ckvn_ref, *, normalize):
    ckv = ckv_ref[...]
    if normalize:
        ckv = ckv * lax.rsqrt(jnp.mean(ckv * ckv, -1, keepdims=True) + RMS_EPS) * g_ref[...]
    ckvn_ref[...] = ckv
    nb = ckv.astype(BF16)
    t = kr_ref[...] * tab_ref[...]
    khi = (t + pltpu.roll(t, ROPE, axis=1)).astype(BF16)
    kn = jnp.dot(nb, wk_ref[...], preferred_element_type=F32).astype(BF16)
    v_ref[...] = jnp.dot(nb, wv_ref[...], preferred_element_type=F32).astype(BF16)
    hw = 2 * LANES
    for h in range(HEADS):
        kcat_ref[:, h * hw:h * hw + LANES] = kn[:, h * LANES:(h + 1) * LANES]
        kcat_ref[:, h * hw + LANES:(h + 1) * hw] = khi


def kv_proj(ckv_src, ckv_col, kr_src, kr_col, g, wk, wv, tabk, normalize):
    n = ckv_src.shape[0]
    hw = 2 * LANES
    return pl.pallas_call(
        functools.partial(_kvproj_kernel, normalize=normalize),
        grid=(n // TB,),
        in_specs=[pl.BlockSpec((TB, KV_LORA), lambda i: (i, ckv_col)),
                  pl.BlockSpec((TB, LANES), lambda i: (i, kr_col)),
                  pl.BlockSpec((1, KV_LORA), lambda i: (0, 0)),
                  pl.BlockSpec((KV_LORA, D), lambda i: (0, 0)),
                  pl.BlockSpec((KV_LORA, D), lambda i: (0, 0)),
                  pl.BlockSpec((TB, LANES), lambda i: (i, 0))],
        out_specs=[pl.BlockSpec((TB, HEADS * hw), lambda i: (i, 0)),
                   pl.BlockSpec((TB, D), lambda i: (i, 0)),
                   pl.BlockSpec((TB, KV_LORA), lambda i: (i, 0))],
        out_shape=[jax.ShapeDtypeStruct((n, HEADS * hw), BF16),
                   jax.ShapeDtypeStruct((n, D), BF16),
                   jax.ShapeDtypeStruct((n, KV_LORA), F32)],
        compiler_params=_cp("arbitrary"),
        name="mla_kv_proj",
    )(ckv_src, kr_src, g, wk, wv, tabk)


def _attn_kernel(q_ref, k_ref, v_ref, o_ref):
    s = lax.dot_general(q_ref[...], k_ref[...], (((1,), (1,)), ((), ())), preferred_element_type=F32)
    s = s * (QK_DIM ** -0.5)
    e = jnp.exp(s - jnp.max(s, -1, keepdims=True))
    p = (e * (1.0 / jnp.sum(e, -1, keepdims=True))).astype(BF16)
    o_ref[...] = jnp.dot(p, v_ref[...], preferred_element_type=F32).astype(BF16)


def attention(q, q_row0, kcat, v, n_seq, t_q, n_keys, name):
    hw = 2 * LANES
    qb = t_q // TB
    qoff = q_row0 // TB
    return pl.pallas_call(
        _attn_kernel,
        grid=(n_seq, HEADS, qb),
        in_specs=[pl.BlockSpec((TB, hw), lambda s, h, i: (qoff + s * qb + i, h)),
                  pl.BlockSpec((n_keys, hw), lambda s, h, i: (s, h)),
                  pl.BlockSpec((n_keys, V_DIM), lambda s, h, i: (s, h))],
        out_specs=pl.BlockSpec((TB, V_DIM), lambda s, h, i: (s * qb + i, h)),
        out_shape=jax.ShapeDtypeStruct((n_seq * t_q, D), BF16),
        compiler_params=_cp("arbitrary", "arbitrary", "arbitrary"),
        name=name,
    )(q, kcat, v)


def _head_ones():
    r = lax.broadcasted_iota(jnp.int32, (LANES, LANES), 0) >= RH
    c = lax.broadcasted_iota(jnp.int32, (LANES, LANES), 1) >= RH
    return (r == c).astype(F32)


def _head_sum(x, ones_bd):
    parts = [jnp.dot(x[:, i * LANES:(i + 1) * LANES], ones_bd, precision=HI, preferred_element_type=F32)
             for i in range(x.shape[1] // LANES)]
    return jnp.concatenate(parts, axis=1)


def _rwkv_prep_kernel(p_ref, prev_ref, next_ref, mu_ref, kkw_ref, w0_ref, w2_ref, a0_ref, a2_ref, ka_ref,
                      rk_ref, g2_ref,
                      r_o, v_o, kk_o, lwf_o, lwb_o, kdf_o, kdb_o, bqf_o, bqb_o, bonus_o, g_o,
                      *, nb_ctx, bps_ctx, bps_lat):
    i = pl.program_id(0)
    is_lat = i >= nb_ctx
    pos = jnp.where(is_lat, lax.rem(i - nb_ctx, bps_lat), lax.rem(i, bps_ctx))
    last = jnp.where(is_lat, bps_lat - 1, bps_ctx - 1)
    has_prev = (pos != 0).astype(F32)
    has_next = (pos != last).astype(F32)
    rows = lax.broadcasted_iota(jnp.int32, (RB, 1), 0)

    def shifted(lo, hi):
        p = p_ref[:, lo:hi]
        prev = jnp.where(rows == 0, prev_ref[7:8, lo:hi] * has_prev, pltpu.roll(p, 1, axis=0))
        nxt = jnp.where(rows == RB - 1, next_ref[0:1, lo:hi] * has_next, pltpu.roll(p, RB - 1, axis=0))
        return p + (0.5 * (prev + nxt) - p) * mu_ref[:, lo:hi]

    ones_bd = _head_ones()
    r = shifted(0, D)
    k = shifted(D, 2 * D)
    v = shifted(2 * D, 3 * D)
    r_o[...] = r
    v_o[...] = v
    kkf = k * kkw_ref[...]
    nrm = jnp.sqrt(_head_sum(kkf * kkf, ones_bd))
    kk = kkf / jnp.maximum(nrm, 1e-12)
    kk_o[...] = kk
    bonus_o[...] = _head_sum(r * k * rk_ref[...], ones_bd) * v
    base = 3 * D
    outs = ((lwf_o, kdf_o, bqf_o), (lwb_o, kdb_o, bqb_o))
    for d in range(2):
        wl = shifted(base + d * LORA_PAD, base + (d + 1) * LORA_PAD)
        z = w0_ref[d:d + 1, :] + jnp.dot(jnp.tanh(wl).astype(BF16), w2_ref[d], preferred_element_type=F32)
        y = -z
        softplus = jnp.maximum(y, 0.0) + jnp.log(1.0 + jnp.exp(-jnp.abs(y)))
        logw = -softplus - 0.5
        outs[d][0][...] = -jnp.exp(logw)
        al = shifted(base + (2 + d) * LORA_PAD, base + (3 + d) * LORA_PAD)
        a = _sigmoid(a0_ref[d:d + 1, :] + jnp.dot(al.astype(BF16), a2_ref[d], preferred_element_type=F32))
        outs[d][1][...] = k * (1.0 + (a - 1.0) * ka_ref[d:d + 1, :])
        outs[d][2][...] = kk * a
    gl = shifted(base + 4 * LORA_PAD, base + 4 * LORA_PAD + GATE_LORA)
    g_o[...] = jnp.dot(_sigmoid(gl).astype(BF16), g2_ref[...], preferred_element_type=F32)


def rwkv_prep(p_rw, mu, kkw, w0, w2, a0, a2, ka, rk, g2, nb_ctx, bps_ctx, bps_lat):
    n = p_rw.shape[0]
    nb = n // RB
    sub = RB // 8
    row = lambda i: (i, 0)
    const2 = lambda i: (0, 0)
    out = jax.ShapeDtypeStruct((n, D), F32)
    return pl.pallas_call(
        functools.partial(_rwkv_prep_kernel, nb_ctx=nb_ctx, bps_ctx=bps_ctx, bps_lat=bps_lat),
        grid=(nb,),
        in_specs=[pl.BlockSpec((RB, RW_COLS), row),
                  pl.BlockSpec((8, RW_COLS), lambda i: (jnp.maximum(i * sub - 1, 0), 0)),
                  pl.BlockSpec((8, RW_COLS), lambda i: (jnp.minimum((i + 1) * sub, nb * sub - 1), 0)),
                  pl.BlockSpec((1, RW_COLS), const2),
                  pl.BlockSpec((1, D), const2),
                  pl.BlockSpec((2, D), const2),
                  pl.BlockSpec((2, LORA_PAD, D), lambda i: (0, 0, 0)),
                  pl.BlockSpec((2, D), const2),
                  pl.BlockSpec((2, LORA_PAD, D), lambda i: (0, 0, 0)),
                  pl.BlockSpec((2, D), const2),
                  pl.BlockSpec((1, D), const2),
                  pl.BlockSpec((GATE_LORA, D), const2)],
        out_specs=[pl.BlockSpec((RB, D), row)] * 11,
        out_shape=[out] * 11,
        compiler_params=_cp("arbitrary"),
        name="rwkv_prep",
    )(p_rw, p_rw, p_rw, mu, kkw, w0, w2, a0, a2, ka, rk, g2)


def _bmm(a, b):
    return jnp.einsum("hmk,hkn->hmn", a, b, precision=HI, preferred_element_type=F32)


def _bmm_nt(a, b):
    return jnp.einsum("hmk,hnk->hmn", a, b, precision=HI, preferred_element_type=F32)


def _bmm_tn(a, b):
    return jnp.einsum("hkm,hkn->hmn", a, b, precision=HI, preferred_element_type=F32)


def _wkv_chunk(r, v, kk, lw, kd, bq, st, reverse):
    ti = lax.broadcasted_iota(jnp.int32, (SC, SC), 0)
    si = lax.broadcasted_iota(jnp.int32, (SC, SC), 1)
    if reverse:
        incl, strict, last = si >= ti, si > ti, 0
    else:
        incl, strict, last = si <= ti, si < ti, SC - 1
    eye = (ti == si).astype(F32)
    hg = lw.shape[0]
    tri = jnp.broadcast_to(incl.astype(F32)[None], (hg, SC, SC))
    L = _bmm(tri, lw)
    ltot = L[:, last:last + 1, :]
    ax = -kk * jnp.exp(L - lw)
    rt = r * jnp.exp(L)
    en = jnp.exp(-L)
    bt = bq * en
    kt = kd * en
    et = jnp.exp(ltot - L)
    bc = bq * et
    kc = kd * et
    a_ab = jnp.where(strict, _bmm_nt(ax, bt), 0.0)
    a_ak = jnp.where(strict, _bmm_nt(ax, kt), 0.0)
    a_rb = jnp.where(incl, _bmm_nt(rt, bt), 0.0)
    a_rk = jnp.where(incl, _bmm_nt(rt, kt), 0.0)
    npow = a_ab
    tinv = eye + npow
    for _ in range(int(math.log2(SC)) - 1):
        npow = _bmm(npow, npow)
        tinv = _bmm(tinv, eye + npow)
    axp = _bmm(tinv, ax)
    vu = _bmm(tinv, _bmm(a_ak, v))
    rp = rt + _bmm(a_rb, axp)
    y0 = _bmm(a_rb, vu) + _bmm(a_rk, v)
    g = eye * jnp.exp(ltot) + _bmm_tn(bc, axp)
    hm = _bmm_tn(bc, vu) + _bmm_tn(kc, v)
    y = _bmm(rp, st) + y0
    st_new = _bmm(g, st) + hm
    return y, st_new


def _scan_kernel(rf, vf, kkf, lwf, kdf, bqf, rb, vb, kkb, lwb, kdb, bqb, s0f, s0b,
                 yf_ref, yb_ref, sfo, sbo, stf, stb, *, nc):
    c = pl.program_id(2)

    @pl.when(c == 0)
    def _():
        stf[...] = s0f[...]
        stb[...] = s0b[...]

    def heads(ref):
        x = ref[...]
        return jnp.stack([x[:, h * RH:(h + 1) * RH] for h in range(HG)], axis=0)

    for reverse, refs, st, yref in ((False, (rf, vf, kkf, lwf, kdf, bqf), stf, yf_ref),
                                    (True, (rb, vb, kkb, lwb, kdb, bqb), stb, yb_ref)):
        r, v, kk, lw, kd, bq = (heads(x) for x in refs)
        y, st_new = _wkv_chunk(r, v, kk, lw, kd, bq, st[...], reverse)
        st[...] = st_new
        yref[...] = jnp.concatenate([y[h] for h in range(HG)], axis=1)

    @pl.when(c == nc - 1)
    def _():
        sfo[...] = stf[...]
        sbo[...] = stb[...]


def rwkv_scan(r, v, kk, lwf, lwb, kdf, kdb, bqf, bqb, s0f, s0b, row0, n_seq, t_len, name):
    nc = t_len // SC
    off = row0 // SC
    cw = HG * RH
    fwd = lambda s, h, c: (off + s * nc + c, h)
    bwd = lambda s, h, c: (off + s * nc + (nc - 1 - c), h)
    blk = lambda im: pl.BlockSpec((SC, cw), im)
    st_spec = pl.BlockSpec((None, HG, RH, RH), lambda s, h, c: (s, h, 0, 0))
    n_rows = n_seq * t_len
    return pl.pallas_call(
        functools.partial(_scan_kernel, nc=nc),
        grid=(n_seq, RHEADS // HG, nc),
        in_specs=[blk(fwd)] * 6 + [blk(bwd)] * 6 + [st_spec, st_spec],
        out_specs=[pl.BlockSpec((SC, cw), lambda s, h, c: (s * nc + c, h)),
                   pl.BlockSpec((SC, cw), lambda s, h, c: (s * nc + (nc - 1 - c), h)),
                   st_spec, st_spec],
        out_shape=[jax.ShapeDtypeStruct((n_rows, D), F32), jax.ShapeDtypeStruct((n_rows, D), F32),
                   jax.ShapeDtypeStruct(s0f.shape, F32), jax.ShapeDtypeStruct(s0b.shape, F32)],
        scratch_shapes=[pltpu.VMEM((HG, RH, RH), F32), pltpu.VMEM((HG, RH, RH), F32)],
        compiler_params=_cp("arbitrary", "arbitrary", "arbitrary"),
        name=name,
    )(r, v, kk, lwf, kdf, bqf, r, v, kk, lwb, kdb, bqb, s0f, s0b)


def _rwkv_fin_kernel(yf_ref, yb_ref, bonus_ref, g_ref, lg_ref, lb_ref, o_ref):
    ones_bd = _head_ones()
    y = yf_ref[...] + yb_ref[...]
    mu = _head_sum(y, ones_bd) * (1.0 / RH)
    yc = y - mu
    var = _head_sum(yc * yc, ones_bd) * (1.0 / RH)
    yn = yc * lax.rsqrt(var + GN_EPS) * lg_ref[...] + lb_ref[...]
    o_ref[...] = ((yn + bonus_ref[...]) * g_ref[...]).astype(BF16)


def rwkv_finish(yf, yb, bonus, g, lnx_g, lnx_b):
    n = yf.shape[0]
    row = lambda i: (i, 0)
    const2 = lambda i: (0, 0)
    return pl.pallas_call(
        _rwkv_fin_kernel,
        grid=(n // TB,),
        in_specs=[pl.BlockSpec((TB, D), row)] * 4 + [pl.BlockSpec((1, D), const2)] * 2,
        out_specs=pl.BlockSpec((TB, D), row),
        out_shape=jax.ShapeDtypeStruct((n, D), BF16),
        compiler_params=_cp("arbitrary"),
        name="rwkv_finish",
    )(yf, yb, bonus, g, lnx_g, lnx_b)


def _merge_kernel(oa_ref, ob_ref, oc_ref, w_ref, ga_ref, gb_ref, gc_ref, z_ref):
    acc = None
    for n, (o_ref, g_ref) in enumerate(((oa_ref, ga_ref), (ob_ref, gb_ref), (oc_ref, gc_ref))):
        y = _sigmoid(g_ref[...]) * jnp.dot(o_ref[...], w_ref[n], preferred_element_type=F32)
        acc = y if acc is None else acc + y
    z_ref[...] = acc.astype(BF16)


def branch_merge(o_a, o_b, o_c, w_branch, p_gate):
    n = o_a.shape[0]
    tn = 512
    tm = 512 if n % 512 == 0 else TB
    nj = D // tn
    o_spec = pl.BlockSpec((tm, D), lambda j, i: (i, 0))
    gate = lambda b: pl.BlockSpec((tm, tn), lambda j, i: (i, b * nj + j))
    return pl.pallas_call(
        _merge_kernel,
        grid=(nj, n // tm),
        in_specs=[o_spec, o_spec, o_spec,
                  pl.BlockSpec((3, D, tn), lambda j, i: (0, 0, j)),
                  gate(0), gate(1), gate(2)],
        out_specs=pl.BlockSpec((tm, tn), lambda j, i: (i, j)),
        out_shape=jax.ShapeDtypeStruct((n, D), BF16),
        compiler_params=_cp("arbitrary", "arbitrary"),
        name="branch_merge",
    )(o_a, o_b, o_c, w_branch, p_gate, p_gate, p_gate)


def _oproj_kernel(z_ref, wo_ref, x_ref, mod_ref, g_ref, b_ref, x1_ref, h2_ref):
    mix = jnp.dot(z_ref[...], wo_ref[...], preferred_element_type=F32)
    m = mod_ref[...]
    x1 = _ln(ALPHA * x_ref[...] + m[2:3] * mix) * g_ref[...] + b_ref[...]
    x1_ref[...] = x1
    h2_ref[...] = _ln(x1) * (1.0 + m[4:5]) + m[3:4]


def out_proj(z, w_o, x, mod, modrow, ln_g, ln_b):
    n = z.shape[0]
    row = lambda i: (i, 0)
    const2 = lambda i: (0, 0)
    return pl.pallas_call(
        _oproj_kernel,
        grid=(n // TB,),
        in_specs=[pl.BlockSpec((TB, D), row),
                  pl.BlockSpec((D, D), const2),
                  pl.BlockSpec((TB, D), row),
                  pl.BlockSpec((None, 6, D), lambda i: (modrow(i), 0, 0)),
                  pl.BlockSpec((1, D), const2),
                  pl.BlockSpec((1, D), const2)],
        out_specs=[pl.BlockSpec((TB, D), row), pl.BlockSpec((TB, D), row)],
        out_shape=[jax.ShapeDtypeStruct((n, D), F32), jax.ShapeDtypeStruct((n, D), F32)],
        compiler_params=_cp("arbitrary"),
        name="out_proj_ln",
    )(z, w_o, x, mod, ln_g, ln_b)


def _router_kernel(h_ref, w_ref, b_ref, idx_ref, wts_ref, hist_ref):
    logits = jnp.dot(h_ref[...], w_ref[...], precision=HI, preferred_element_type=F32) + b_ref[...]
    lane = lax.broadcasted_iota(jnp.int32, (TB, LANES), 1).astype(F32)
    cur = logits
    idx_out = jnp.zeros((TB, LANES), F32)
    val_out = jnp.zeros((TB, LANES), F32)
    hist = jnp.zeros((TB, LANES), F32)
    top = None
    for k in range(TOP_K):
        m = jnp.max(cur, -1, keepdims=True)
        am = jnp.min(jnp.where(cur == m, lane, float(LANES)), -1, keepdims=True)
        sel = lane == am
        top = m if top is None else top
        idx_out = jnp.where(lane == k, am, idx_out)
        val_out = jnp.where(lane == k, jnp.exp(m - top), val_out)
        hist = hist + sel.astype(F32)
        cur = jnp.where(sel, -jnp.inf, cur)
    idx_ref[...] = idx_out.astype(jnp.int32)
    wts_ref[...] = val_out * (1.0 / jnp.sum(val_out, -1, keepdims=True))
    hist_ref[...] = jnp.sum(hist, 0, keepdims=True)


def router(h2, rw_pad, rb_pad):
    n = h2.shape[0]
    nb = n // TB
    return pl.pallas_call(
        _router_kernel,
        grid=(nb,),
        in_specs=[pl.BlockSpec((TB, D), lambda i: (i, 0)),
                  pl.BlockSpec((D, LANES), lambda i: (0, 0)),
                  pl.BlockSpec((1, LANES), lambda i: (0, 0))],
        out_specs=[pl.BlockSpec((TB, LANES), lambda i: (i, 0)),
                   pl.BlockSpec((TB, LANES), lambda i: (i, 0)),
                   pl.BlockSpec((None, 1, LANES), lambda i: (i, 0, 0))],
        out_shape=[jax.ShapeDtypeStruct((n, LANES), jnp.int32),
                   jax.ShapeDtypeStruct((n, LANES), F32),
                   jax.ShapeDtypeStruct((nb, 1, LANES), F32)],
        compiler_params=_cp("arbitrary"),
        name="moe_router",
    )(h2, rw_pad, rb_pad)


def _dest_kernel(idx_ref, base_ref, dest_ref):
    lane = lax.broadcasted_iota(jnp.int32, (TB, LANES), 1).astype(F32)
    ri = lax.broadcasted_iota(jnp.int32, (TB, TB), 0)
    ci = lax.broadcasted_iota(jnp.int32, (TB, TB), 1)
    lower = (ci < ri).astype(BF16)
    idx = idx_ref[...].astype(F32)
    run = base_ref[...]
    dest = jnp.zeros((TB, LANES), F32)
    for k in range(TOP_K):
        e_k = jnp.sum(jnp.where(lane == k, idx, 0), -1, keepdims=True)
        onehot = lane == e_k
        oh = onehot.astype(F32)
        before = jnp.dot(lower, oh.astype(BF16), preferred_element_type=F32)
        d_k = jnp.sum(jnp.where(onehot, run + before, 0.0), -1, keepdims=True)
        dest = jnp.where(lane == k, d_k, dest)
        run = run + jnp.sum(oh, 0, keepdims=True)
    dest_ref[...] = dest.astype(jnp.int32)


def moe_dest(idx, base):
    n = idx.shape[0]
    nb = n // TB
    return pl.pallas_call(
        _dest_kernel,
        grid=(nb,),
        in_specs=[pl.BlockSpec((TB, LANES), lambda i: (i, 0)),
                  pl.BlockSpec((None, 1, LANES), lambda i: (i, 0, 0))],
        out_specs=pl.BlockSpec((TB, LANES), lambda i: (i, 0)),
        out_shape=jax.ShapeDtypeStruct((n, LANES), jnp.int32),
        compiler_params=_cp("arbitrary"),
        name="moe_dest",
    )(idx, base)


def _row_copy(src, src_row, dst, dst_row, sem):
    return pltpu.make_async_copy(src.at[pl.ds(src_row, 1)], dst.at[pl.ds(dst_row, 1)], sem)


def _dispatch_kernel(dest_ref, h_ref, xb_in, xb_out, sem):
    del xb_in

    def issue(t, carry):
        for k in range(TOP_K):
            _row_copy(h_ref, t, xb_out, dest_ref[0, t * TOP_K + k], sem).start()
        return carry

    lax.fori_loop(0, TB, issue, 0)

    def drain(t, carry):
        for k in range(TOP_K):
            _row_copy(h_ref, t, xb_out, dest_ref[0, t * TOP_K + k], sem).wait()
        return carry

    lax.fori_loop(0, TB, drain, 0)


def moe_dispatch(dest_flat, h2, xb_init):
    n = h2.shape[0]
    nb = n // TB
    return pl.pallas_call(
        _dispatch_kernel,
        grid=(nb,),
        in_specs=[pl.BlockSpec((None, 1, TB * TOP_K), lambda i: (i, 0, 0), memory_space=pltpu.SMEM),
                  pl.BlockSpec((TB, D), lambda i: (i, 0)),
                  pl.BlockSpec(memory_space=pl.ANY)],
        out_specs=pl.BlockSpec(memory_space=pl.ANY),
        out_shape=jax.ShapeDtypeStruct(xb_init.shape, xb_init.dtype),
        scratch_shapes=[pltpu.SemaphoreType.DMA(())],
        input_output_aliases={2: 0},
        compiler_params=_cp("arbitrary"),
        name="moe_dispatch",
    )(dest_flat, h2, xb_init)


def _expert_up_kernel(be_ref, nv_ref, x_ref, wg_ref, bg_ref, wu_ref, bu_ref, act_ref):
    used = pl.program_id(1) < nv_ref[0]

    @pl.when(jnp.logical_not(used))
    def _():
        act_ref[...] = jnp.zeros_like(act_ref)

    @pl.when(used)
    def _():
        xb = x_ref[...].astype(BF16)
        gl = jnp.minimum(jnp.dot(xb, wg_ref[...], preferred_element_type=F32) + bg_ref[...], SWIGLU_LIMIT)
        lin = jnp.clip(jnp.dot(xb, wu_ref[...], preferred_element_type=F32) + bu_ref[...],
                       -SWIGLU_LIMIT, SWIGLU_LIMIT)
        act_ref[...] = (gl * _sigmoid(SWIGLU_ALPHA * gl) * (lin + 1.0)).astype(BF16)


def expert_up(block_e, n_valid, xb, w_gate, b_gate, w_up, b_up, layer):
    cap = xb.shape[0]
    tn = 1024
    rowblk = lambda j, i, be, nv: (jnp.minimum(i, nv[0] - 1), 0)
    wspec = pl.BlockSpec((None, None, D, tn), lambda j, i, be, nv: (layer, be[i], 0, j))
    bspec = pl.BlockSpec((None, None, 1, tn), lambda j, i, be, nv: (layer, be[i], 0, j))
    return pl.pallas_call(
        _expert_up_kernel,
        grid_spec=pltpu.PrefetchScalarGridSpec(
            num_scalar_prefetch=2,
            grid=(D // tn, cap // BM),
            in_specs=[pl.BlockSpec((BM, D), rowblk), wspec, bspec, wspec, bspec],
            out_specs=pl.BlockSpec((BM, tn), lambda j, i, be, nv: (i, j))),
        out_shape=jax.ShapeDtypeStruct((cap, D), BF16),
        compiler_params=_cp("arbitrary", "arbitrary"),
        name="moe_expert_up",
    )(block_e, n_valid, xb, w_gate, b_gate, w_up, b_up)


def _expert_down_kernel(be_ref, nv_ref, a_ref, wd_ref, bd_ref, y_ref):
    used = pl.program_id(1) < nv_ref[0]

    @pl.when(jnp.logical_not(used))
    def _():
        y_ref[...] = jnp.zeros_like(y_ref)

    @pl.when(used)
    def _():
        y_ref[...] = jnp.dot(a_ref[...], wd_ref[...], preferred_element_type=F32) + bd_ref[...]


def expert_down(block_e, n_valid, act, w_down, b_down, layer):
    cap = act.shape[0]
    tn = 1024
    return pl.pallas_call(
        _expert_down_kernel,
        grid_spec=pltpu.PrefetchScalarGridSpec(
            num_scalar_prefetch=2,
            grid=(D // tn, cap // BM),
            in_specs=[pl.BlockSpec((BM, D), lambda j, i, be, nv: (jnp.minimum(i, nv[0] - 1), 0)),
                      pl.BlockSpec((None, None, D, tn), lambda j, i, be, nv: (layer, be[i], 0, j)),
                      pl.BlockSpec((None, None, 1, tn), lambda j, i, be, nv: (layer, be[i], 0, j))],
            out_specs=pl.BlockSpec((BM, tn), lambda j, i, be, nv: (i, j))),
        out_shape=jax.ShapeDtypeStruct((cap, D), F32),
        compiler_params=_cp("arbitrary", "arbitrary"),
        name="moe_expert_down",
    )(block_e, n_valid, act, w_down, b_down)


def _combine_kernel(dest_ref, wts_ref, x1_ref, mod_ref, g_ref, b_ref, modn_ref, yb_ref,
                    x2_ref, hn_ref, rows, sem):
    def issue(t, carry):
        for k in range(TOP_K):
            _row_copy(yb_ref, dest_ref[0, t * TOP_K + k], rows.at[k], t, sem).start()
        return carry

    lax.fori_loop(0, TB, issue, 0)

    def drain(t, carry):
        for k in range(TOP_K):
            _row_copy(yb_ref, dest_ref[0, t * TOP_K + k], rows.at[k], t, sem).wait()
        return carry

    lax.fori_loop(0, TB, drain, 0)
    w = wts_ref[...]
    ffn = rows[0] * w[:, 0:1]
    for k in range(1, TOP_K):
        ffn = ffn + rows[k] * w[:, k:k + 1]
    m = mod_ref[...]
    x2 = _ln(ALPHA * x1_ref[...] + m[5:6] * ffn) * g_ref[...] + b_ref[...]
    x2_ref[...] = x2
    mn = modn_ref[...]
    hn_ref[...] = (_ln(x2) * (1.0 + mn[1:2]) + mn[0:1]).astype(BF16)


def moe_combine(dest_flat, wts, x1, mod, mod_next, modrow, ln_g, ln_b, yb):
    n = x1.shape[0]
    row = lambda i: (i, 0)
    const2 = lambda i: (0, 0)
    modspec = pl.BlockSpec((None, 6, D), lambda i: (modrow(i), 0, 0))
    return pl.pallas_call(
        _combine_kernel,
        grid=(n // TB,),
        in_specs=[pl.BlockSpec((None, 1, TB * TOP_K), lambda i: (i, 0, 0), memory_space=pltpu.SMEM),
                  pl.BlockSpec((TB, LANES), row),
                  pl.BlockSpec((TB, D), row),
                  modspec,
                  pl.BlockSpec((1, D), const2),
                  pl.BlockSpec((1, D), const2),
                  modspec,
                  pl.BlockSpec(memory_space=pl.ANY)],
        out_specs=[pl.BlockSpec((TB, D), row), pl.BlockSpec((TB, D), row)],
        out_shape=[jax.ShapeDtypeStruct((n, D), F32), jax.ShapeDtypeStruct((n, D), BF16)],
        scratch_shapes=[pltpu.VMEM((TOP_K, TB, D), F32), pltpu.SemaphoreType.DMA(())],
        compiler_params=_cp("arbitrary"),
        name="moe_combine",
    )(dest_flat, wts, x1, mod, ln_g, ln_b, mod_next, yb)


def moe_layout(hist):
    h = hist[:, 0, :].astype(jnp.int32)
    counts = jnp.sum(h, 0)
    padded = (counts + BM - 1) // BM * BM
    pad_end = jnp.cumsum(padded)
    pad_start = pad_end - padded
    before = jnp.cumsum(h, 0) - h
    base = (pad_start[None, :] + before).astype(F32)[:, None, :]
    return base, pad_end


def _rot_cols(w):
    q = ROPE // 4
    return jnp.concatenate([-w[..., q:2 * q], w[..., 0:q], -w[..., 3 * q:4 * q], w[..., 2 * q:3 * q]], axis=-1)


def _rope_tables(n):
    rows = n // GRID_W
    row_id = jnp.repeat(jnp.arange(rows, dtype=F32), GRID_W)
    col_id = jnp.tile(jnp.arange(GRID_W, dtype=F32), rows)
    half = ROPE // 2
    inv_freq = ROPE_THETA ** (-jnp.arange(0, half, 2, dtype=F32) / half)
    ang_r = row_id[:, None] * inv_freq
    ang_c = col_id[:, None] * inv_freq
    cos = jnp.concatenate([jnp.cos(ang_r)] * 2 + [jnp.cos(ang_c)] * 2, axis=-1)
    sin = jnp.concatenate([jnp.sin(ang_r)] * 2 + [jnp.sin(ang_c)] * 2, axis=-1)
    return cos, sin


def _pad_rows(w, rows):
    return jnp.pad(w, [(0, 0)] * (w.ndim - 2) + [(0, rows - w.shape[-2]), (0, 0)])


def kernel(x_prompt, x_sample, cache_ckv, cache_krope, state_rwkv_fwd, state_rwkv_bwd, c, c_ctx, w_mod, b_mod, w_in, gmlp_ln_g, gmlp_ln_b, gmlp_ws, gmlp_bs, q_norm_g, w_uq, kv_norm_g, w_ukv, rwkv_mu, rwkv_w0, rwkv_w2, rwkv_a0, rwkv_a2, rwkv_g2, rwkv_kk, rwkv_ka, rwkv_rk, rwkv_lnx_g, rwkv_lnx_b, w_branch, w_o, ln1_g, ln1_b, router_w, router_b, w_gate, b_gate, w_up, b_up, w_down, b_down, ln2_g, ln2_b):
    L = w_mod.shape[0]
    b_ctx, seq, _ = x_prompt.shape
    b_lat, t_lat, _ = x_sample.shape
    past = cache_ckv.shape[2]
    n_ctx = b_ctx * seq
    n_lat = b_lat * t_lat
    n = n_ctx + n_lat
    assert seq % TB == 0 and t_lat % TB == 0 and past % TB == 0 and b_lat + 1 <= 8
    nb_ctx = n_ctx // TB
    bps_lat = t_lat // TB

    def modrow(i):
        return jnp.where(i < nb_ctx, 0, 1 + (i - nb_ctx) // bps_lat)

    cond8 = jnp.zeros((8, D), F32).at[0].set(c_ctx).at[1:1 + b_lat].set(c)
    mod_all = modulation(cond8, w_mod, b_mod).reshape(L, 8, 6, D)

    gm_end = 2 * D
    mla_end = gm_end + Q_LORA + KV_LORA + ROPE
    rw_end = mla_end + 3 * D + 2 * DECAY_LORA + 2 * AAA_LORA + GATE_LORA
    w_gm = w_in[:, :, :gm_end].astype(BF16)
    w_kr = w_in[:, :, mla_end - ROPE:mla_end]
    w_mla = jnp.concatenate([w_in[:, :, gm_end:mla_end], _rot_cols(w_kr)], axis=-1).astype(BF16)
    rw = w_in[:, :, mla_end:rw_end]
    segs = [rw[:, :, :3 * D]]
    mu_segs = [rwkv_mu[:, :3 * D]]
    o = 3 * D
    for width in (DECAY_LORA, DECAY_LORA, AAA_LORA, AAA_LORA):
        segs.append(jnp.pad(rw[:, :, o:o + width], ((0, 0), (0, 0), (0, LORA_PAD - width))))
        mu_segs.append(jnp.pad(rwkv_mu[:, o:o + width], ((0, 0), (0, LORA_PAD - width))))
        o += width
    segs.append(rw[:, :, o:])
    mu_segs.append(rwkv_mu[:, o:])
    w_rw = jnp.concatenate(segs, axis=-1).astype(BF16)
    mu_rw = jnp.concatenate(mu_segs, axis=-1)[:, None, :]
    w_gt = w_in[:, :, rw_end:].astype(BF16)
    w2p = _pad_rows(rwkv_w2, LORA_PAD).astype(BF16)
    a2p = _pad_rows(rwkv_a2, LORA_PAD).astype(BF16)
    g2b = rwkv_g2.astype(BF16)
    wq4 = w_uq.reshape(L, Q_LORA, HEADS, QK_DIM)
    wq_r = wq4[..., NOPE:]
    wq = jnp.concatenate([wq4[..., :NOPE], wq_r, _rot_cols(wq_r)], axis=-1)
    wq = wq.reshape(L, Q_LORA, HEADS * 2 * LANES).astype(BF16)
    wkv4 = w_ukv.reshape(L, KV_LORA, HEADS, NOPE + V_DIM)
    wk = wkv4[..., :NOPE].reshape(L, KV_LORA, D).astype(BF16)
    wv = wkv4[..., NOPE:].reshape(L, KV_LORA, D).astype(BF16)
    ws_b = gmlp_ws.astype(BF16)
    bs_full = jnp.repeat(jnp.swapaxes(gmlp_bs, 1, 2), GMLP_GW, axis=2)
    w_br = w_branch.astype(BF16)
    w_ob = w_o.astype(BF16)
    rw_pad = jnp.pad(router_w, ((0, 0), (0, 0), (0, LANES - N_EXPERTS)))
    rb_pad = jnp.pad(router_b, ((0, 0), (0, LANES - N_EXPERTS)), constant_values=-1e30)[:, None, :]
    wg_b = w_gate.astype(BF16)
    wu_b = w_up.astype(BF16)
    wd_b = w_down.astype(BF16)
    bg4 = b_gate[:, :, None, :]
    bu4 = b_up[:, :, None, :]
    bd4 = b_down[:, :, None, :]

    cos, sin = _rope_tables(t_lat)
    cos_all = jnp.concatenate([jnp.ones((n_ctx, ROPE), F32), jnp.tile(cos, (b_lat, 1))], axis=0)
    sin_all = jnp.concatenate([jnp.zeros((n_ctx, ROPE), F32), jnp.tile(sin, (b_lat, 1))], axis=0)
    tabk = jnp.concatenate([cos_all, sin_all], axis=-1)
    tabq = jnp.concatenate([jnp.ones((n, NOPE), F32), tabk], axis=-1)
    tab_cache = jnp.concatenate([jnp.ones((b_lat * past, ROPE), F32), jnp.zeros((b_lat * past, ROPE), F32)], -1)

    zeros_state = jnp.zeros((b_ctx, RHEADS, RH, RH), F32)
    sf_lat = jnp.swapaxes(state_rwkv_fwd, -1, -2)
    sb_lat = jnp.swapaxes(state_rwkv_bwd, -1, -2)

    cap = (n * TOP_K // BM + N_EXPERTS) * BM
    x = jnp.concatenate([x_prompt.reshape(n_ctx, D), x_sample.reshape(n_lat, D)], axis=0)
    h = ln_modulate(x, mod_all[0], modrow)
    ckv_out, kr_out, sf_out, sb_out = [], [], [], []
    for l in range(L):
        mod = mod_all[l]
        p_gm = matmul(h, w_gm[l], 1024, "in_proj_gmlp")
        p_mla = matmul(h, w_mla[l], MLA_COLS, "in_proj_mla")
        p_rw = matmul(h, w_rw[l], 768, "in_proj_rwkv")
        p_gate = matmul(h, w_gt[l], 1024, "in_proj_gate")

        o_a = gmlp(p_gm, gmlp_ln_g[l][None], gmlp_ln_b[l][None], ws_b[l], bs_full[l])

        q = q_proj(p_mla, q_norm_g[l][None], wq[l], tabq)
        kcat, vv, ckvn = kv_proj(p_mla, 1, p_mla, (Q_LORA + KV_LORA) // LANES, kv_norm_g[l][None],
                                 wk[l], wv[l], tabk, True)
        cache_c = cache_ckv[:, l].reshape(b_lat * past, KV_LORA)
        cache_r = jnp.pad(cache_krope[:, l].reshape(b_lat * past, ROPE), ((0, 0), (0, ROPE)))
        kcat_c, vv_c, _ = kv_proj(cache_c, 0, cache_r, 0, kv_norm_g[l][None], wk[l], wv[l], tab_cache, False)
        hw2 = HEADS * 2 * LANES
        kcat_lat = jnp.concatenate([kcat_c.reshape(b_lat, past, hw2), kcat[n_ctx:].reshape(b_lat, t_lat, hw2)],
                                   axis=1).reshape(b_lat * (past + t_lat), hw2)
        vv_lat = jnp.concatenate([vv_c.reshape(b_lat, past, D), vv[n_ctx:].reshape(b_lat, t_lat, D)],
                                 axis=1).reshape(b_lat * (past + t_lat), D)
        o_b_ctx = attention(q, 0, kcat, vv, b_ctx, seq, seq, "mla_attention_ctx")
        o_b_lat = attention(q, n_ctx, kcat_lat, vv_lat, b_lat, t_lat, past + t_lat, "mla_attention_lat")
        o_b = jnp.concatenate([o_b_ctx, o_b_lat], axis=0)
        ckv_out.append(ckvn[:n_ctx].reshape(b_ctx, seq, KV_LORA))
        kr_out.append(p_mla[:n_ctx, Q_LORA + KV_LORA:Q_LORA + KV_LORA + ROPE].reshape(b_ctx, seq, ROPE))

        (r, v, kk, lwf, lwb, kdf, kdb, bqf, bqb, bonus, gg) = rwkv_prep(
            p_rw, mu_rw[l], rwkv_kk[l][None], rwkv_w0[l], w2p[l], rwkv_a0[l], a2p[l], rwkv_ka[l],
            rwkv_rk[l].reshape(1, D), g2b[l], n_ctx // RB, seq // RB, t_lat // RB)
        yf_c, yb_c, sf_c, sb_c = rwkv_scan(r, v, kk, lwf, lwb, kdf, kdb, bqf, bqb, zeros_state, zeros_state,
                                           0, b_ctx, seq, "rwkv_scan_ctx")
        yf_l, yb_l, _, _ = rwkv_scan(r, v, kk, lwf, lwb, kdf, kdb, bqf, bqb, sf_lat[:, l], sb_lat[:, l],
                                     n_ctx, b_lat, t_lat, "rwkv_scan_lat")
        sf_out.append(jnp.swapaxes(sf_c, -1, -2))
        sb_out.append(jnp.swapaxes(sb_c, -1, -2))
        o_c = rwkv_finish(jnp.concatenate([yf_c, yf_l], 0), jnp.concatenate([yb_c, yb_l], 0), bonus, gg,
                          rwkv_lnx_g[l][None], rwkv_lnx_b[l][None])

        z = branch_merge(o_a, o_b, o_c, w_br[l], p_gate)
        x1, h2 = out_proj(z, w_ob[l], x, mod, modrow, ln1_g[l][None], ln1_b[l][None])

        idx, wts, hist = router(h2, rw_pad[l], rb_pad[l])
        base, pad_end = moe_layout(hist)
        dest = moe_dest(idx, base)
        dest_flat = dest[:, :TOP_K].reshape(n // TB, 1, TB * TOP_K)
        xb = moe_dispatch(dest_flat, h2, jnp.zeros((cap, D), F32))
        n_valid = (pad_end[N_EXPERTS - 1] // BM).astype(jnp.int32)
        blk_start = jnp.minimum(jnp.arange(cap // BM, dtype=jnp.int32), n_valid - 1) * BM
        block_e = jnp.sum((pad_end[None, :N_EXPERTS] <= blk_start[:, None]).astype(jnp.int32), axis=1)
        block_e = jnp.minimum(block_e, N_EXPERTS - 1)
        nv = n_valid.reshape(1)
        act = expert_up(block_e, nv, xb, wg_b, bg4, wu_b, bu4, l)
        yb = expert_down(block_e, nv, act, wd_b, bd4, l)
        mod_next = mod_all[min(l + 1, L - 1)]
        x, h = moe_combine(dest_flat, wts, x1, mod, mod_next, modrow, ln2_g[l][None], ln2_b[l][None], yb)

    y_prompt = x[:n_ctx].reshape(b_ctx, seq, D)
    y_sample = x[n_ctx:].reshape(b_lat, t_lat, D)
    return (y_prompt, y_sample, jnp.stack(ckv_out, axis=1), jnp.stack(kr_out, axis=1),
            jnp.stack(sf_out, axis=1), jnp.stack(sb_out, axis=1))
```

```python
import functools
import math

import jax
import jax.numpy as jnp
from jax import lax
from jax.experimental import pallas as pl
from jax.experimental.pallas import tpu as pltpu

F32 = jnp.float32
BF16 = jnp.bfloat16
HI = lax.Precision.HIGHEST

DEPTH_NORM = 4
GRID_W = 64
CHUNK = 128
D = 2048
GMLP_GW = 128
GMLP_GROUPS = D // GMLP_GW
V_DIM = 128
NOPE = 128
ROPE = 64
QK_DIM = NOPE + ROPE
HEADS = D // V_DIM
Q_LORA = 512
KV_LORA = 512
ROPE_THETA = 10000.0
RH = 64
RHEADS = D // RH
DECAY_LORA = 96
AAA_LORA = 96
GATE_LORA = 256
GN_EPS = 64e-5
N_EXPERTS = 32
TOP_K = 4
SWIGLU_LIMIT = 7.0
SWIGLU_ALPHA = 1.702
LN_EPS = 1e-5
RMS_EPS = 1e-6
ALPHA = (2 * DEPTH_NORM) ** 0.25

LANES = 128
VMEM_LIMIT = 48 * 1024 * 1024

TB = 256
RB = 128
SC = 64
HG = 8
BM = 256
EXPERT_TN = 1024
EXPERT_VMEM_LIMIT = 56 * 1024 * 1024
LORA_PAD = 128
RW_COLS = 3 * D + 4 * LORA_PAD + GATE_LORA
MLA_COLS = Q_LORA + KV_LORA + 2 * ROPE


def _cp(*sem):
    return pltpu.CompilerParams(dimension_semantics=sem, vmem_limit_bytes=VMEM_LIMIT)


def _ln(x):
    mu = jnp.mean(x, -1, keepdims=True)
    xc = x - mu
    var = jnp.mean(xc * xc, -1, keepdims=True)
    return xc * lax.rsqrt(var + LN_EPS)


def _sigmoid(x):
    return 1.0 / (1.0 + jnp.exp(-x))


def _gelu_tanh(x):
    return 0.5 * x * (1.0 + jnp.tanh(math.sqrt(2.0 / math.pi) * (x + 0.044715 * (x * x * x))))


def _mod_kernel(c_ref, w_ref, b_ref, o_ref):
    c = c_ref[...]
    s = (c * _sigmoid(c)).astype(BF16)
    o_ref[...] = jnp.dot(s, w_ref[...].astype(BF16), preferred_element_type=F32) + b_ref[...]


def modulation(cond8, w_mod, b_mod):
    L = w_mod.shape[0]
    tn = 1024
    return pl.pallas_call(
        _mod_kernel,
        grid=(L, 6 * D // tn),
        in_specs=[pl.BlockSpec((8, D), lambda l, j: (0, 0)),
                  pl.BlockSpec((None, D, tn), lambda l, j: (l, 0, j)),
                  pl.BlockSpec((None, 1, tn), lambda l, j: (l, 0, j))],
        out_specs=pl.BlockSpec((None, 8, tn), lambda l, j: (l, 0, j)),
        out_shape=jax.ShapeDtypeStruct((L, 8, 6 * D), F32),
        compiler_params=_cp("arbitrary", "arbitrary"),
        name="modulation",
    )(cond8, w_mod, b_mod.reshape(L, 1, 6 * D))


def _lnmod_kernel(x_ref, mod_ref, h_ref):
    m = mod_ref[...]
    h_ref[...] = (_ln(x_ref[...]) * (1.0 + m[1:2]) + m[0:1]).astype(BF16)


def ln_modulate(x, mod, modrow):
    n = x.shape[0]
    return pl.pallas_call(
        _lnmod_kernel,
        grid=(n // TB,),
        in_specs=[pl.BlockSpec((TB, D), lambda i: (i, 0)),
                  pl.BlockSpec((None, 6, D), lambda i: (modrow(i), 0, 0))],
        out_specs=pl.BlockSpec((TB, D), lambda i: (i, 0)),
        out_shape=jax.ShapeDtypeStruct((n, D), BF16),
        compiler_params=_cp("arbitrary"),
        name="ln_modulate",
    )(x, mod)


def _mm_kernel(x_ref, w_ref, o_ref):
    o_ref[...] = jnp.dot(x_ref[...], w_ref[...], preferred_element_type=F32).astype(o_ref.dtype)


def matmul(x, w, tn, name, out_dtype=F32):
    m, k = x.shape
    nn = w.shape[1]
    tm = 512 if m % 512 == 0 else TB
    return pl.pallas_call(
        _mm_kernel,
        grid=(nn // tn, m // tm),
        in_specs=[pl.BlockSpec((tm, k), lambda j, i: (i, 0)),
                  pl.BlockSpec((k, tn), lambda j, i: (0, j))],
        out_specs=pl.BlockSpec((tm, tn), lambda j, i: (i, j)),
        out_shape=jax.ShapeDtypeStruct((m, nn), out_dtype),
        compiler_params=_cp("arbitrary", "arbitrary"),
        name=name,
    )(x, w)


def _gmlp_kernel(p_ref, g_ref, b_ref, ws_ref, bs_ref, o_ref):
    u = _gelu_tanh(p_ref[:, :D])
    v = _gelu_tanh(p_ref[:, D:])
    v = (_ln(v) * g_ref[...] + b_ref[...]).astype(BF16)
    for g in range(GMLP_GROUPS):
        sl = slice(g * GMLP_GW, (g + 1) * GMLP_GW)
        mixed = jnp.dot(ws_ref[g], v[:, sl], preferred_element_type=F32) + bs_ref[:, sl]
        o_ref[:, sl] = (u[:, sl] * mixed).astype(BF16)


def gmlp(p_gm, ln_g, ln_b, ws, bs_full):
    n = p_gm.shape[0]
    return pl.pallas_call(
        _gmlp_kernel,
        grid=(n // CHUNK,),
        in_specs=[pl.BlockSpec((CHUNK, 2 * D), lambda i: (i, 0)),
                  pl.BlockSpec((1, D), lambda i: (0, 0)),
                  pl.BlockSpec((1, D), lambda i: (0, 0)),
                  pl.BlockSpec((GMLP_GROUPS, CHUNK, CHUNK), lambda i: (0, 0, 0)),
                  pl.BlockSpec((CHUNK, D), lambda i: (0, 0))],
        out_specs=pl.BlockSpec((CHUNK, D), lambda i: (i, 0)),
        out_shape=jax.ShapeDtypeStruct((n, D), BF16),
        compiler_params=_cp("arbitrary"),
        name="gmlp",
    )(p_gm, ln_g, ln_b, ws, bs_full)


def _qproj_kernel(cq_ref, g_ref, w_ref, tab_ref, q_ref):
    cq = cq_ref[...]
    n = (cq * lax.rsqrt(jnp.mean(cq * cq, -1, keepdims=True) + RMS_EPS) * g_ref[...]).astype(BF16)
    tab = tab_ref[...]
    hw = 2 * LANES
    for h in range(HEADS):
        sl = slice(h * hw, (h + 1) * hw)
        q_ref[:, sl] = (jnp.dot(n, w_ref[:, sl], preferred_element_type=F32) * tab).astype(BF16)


def q_proj(p_mla, g, wq, tabq):
    n = p_mla.shape[0]
    hw = 2 * LANES
    return pl.pallas_call(
        _qproj_kernel,
        grid=(n // TB,),
        in_specs=[pl.BlockSpec((TB, Q_LORA), lambda i: (i, 0)),
                  pl.BlockSpec((1, Q_LORA), lambda i: (0, 0)),
                  pl.BlockSpec((Q_LORA, HEADS * hw), lambda i: (0, 0)),
                  pl.BlockSpec((TB, hw), lambda i: (i, 0))],
        out_specs=pl.BlockSpec((TB, HEADS * hw), lambda i: (i, 0)),
        out_shape=jax.ShapeDtypeStruct((n, HEADS * hw), BF16),
        compiler_params=_cp("arbitrary"),
        name="mla_q_proj",
    )(p_mla, g, wq, tabq)


def _kvproj_kernel(ckv_ref, kr_ref, g_ref, wk_ref, wv_ref, tab_ref, kcat_ref, v_ref, ckvn_ref, *, normalize):
    ckv = ckv_ref[...]
    if normalize:
        ckv = ckv * lax.rsqrt(jnp.mean(ckv * ckv, -1, keepdims=True) + RMS_EPS) * g_ref[...]
    ckvn_ref[...] = ckv
    nb = ckv.astype(BF16)
    t = kr_ref[...] * tab_ref[...]
    khi = (t + pltpu.roll(t, ROPE, axis=1)).astype(BF16)
    kn = jnp.dot(nb, wk_ref[...], preferred_element_type=F32).astype(BF16)
    v_ref[...] = jnp.dot(nb, wv_ref[...], preferred_element_type=F32).astype(BF16)
    hw = 2 * LANES
    for h in range(HEADS):
        kcat_ref[:, h * hw:h * hw + LANES] = kn[:, h * LANES:(h + 1) * LANES]
        kcat_ref[:, h * hw + LANES:(h + 1) * hw] = khi


def kv_proj(ckv_src, ckv_col, kr_src, kr_col, g, wk, wv, tabk, normalize):
    n = ckv_src.shape[0]
    hw = 2 * LANES
    return pl.pallas_call(
        functools.partial(_kvproj_kernel, normalize=normalize),
        grid=(n // TB,),
        in_specs=[pl.BlockSpec((TB, KV_LORA), lambda i: (i, ckv_col)),
                  pl.BlockSpec((TB, LANES), lambda i: (i, kr_col)),
                  pl.BlockSpec((1, KV_LORA), lambda i: (0, 0)),
                  pl.BlockSpec((KV_LORA, D), lambda i: (0, 0)),
                  pl.BlockSpec((KV_LORA, D), lambda i: (0, 0)),
                  pl.BlockSpec((TB, LANES), lambda i: (i, 0))],
        out_specs=[pl.BlockSpec((TB, HEADS * hw), lambda i: (i, 0)),
                   pl.BlockSpec((TB, D), lambda i: (i, 0)),
                   pl.BlockSpec((TB, KV_LORA), lambda i: (i, 0))],
        out_shape=[jax.ShapeDtypeStruct((n, HEADS * hw), BF16),
                   jax.ShapeDtypeStruct((n, D), BF16),
                   jax.ShapeDtypeStruct((n, KV_LORA), F32)],
        compiler_params=_cp("arbitrary"),
        name="mla_kv_proj",
    )(ckv_src, kr_src, g, wk, wv, tabk)


def _attn_kernel(q_ref, k_ref, v_ref, o_ref, *, nh):
    hw = 2 * LANES
    for h in range(nh):
        s = lax.dot_general(q_ref[:, h * hw:(h + 1) * hw], k_ref[:, h * hw:(h + 1) * hw],
                            (((1,), (1,)), ((), ())), preferred_element_type=F32)
        s = s * (QK_DIM ** -0.5)
        e = jnp.exp(s - jnp.max(s, -1, keepdims=True))
        p = (e * (1.0 / jnp.sum(e, -1, keepdims=True))).astype(BF16)
        o_ref[:, h * V_DIM:(h + 1) * V_DIM] = jnp.dot(
            p, v_ref[:, h * V_DIM:(h + 1) * V_DIM], preferred_element_type=F32).astype(BF16)


def attention(q, q_row0, kcat, v, n_seq, t_q, n_keys, nh, name):
    hw = 2 * LANES
    qb = t_q // TB
    qoff = q_row0 // TB
    return pl.pallas_call(
        functools.partial(_attn_kernel, nh=nh),
        grid=(n_seq, HEADS // nh, qb),
        in_specs=[pl.BlockSpec((TB, nh * hw), lambda s, h, i: (qoff + s * qb + i, h)),
                  pl.BlockSpec((n_keys, nh * hw), lambda s, h, i: (s, h)),
                  pl.BlockSpec((n_keys, nh * V_DIM), lambda s, h, i: (s, h))],
        out_specs=pl.BlockSpec((TB, nh * V_DIM), lambda s, h, i: (s * qb + i, h)),
        out_shape=jax.ShapeDtypeStruct((n_seq * t_q, D), BF16),
        compiler_params=_cp("arbitrary", "arbitrary", "arbitrary"),
        name=name,
    )(q, kcat, v)


def _head_ones():
    r = lax.broadcasted_iota(jnp.int32, (LANES, LANES), 0) >= RH
    c = lax.broadcasted_iota(jnp.int32, (LANES, LANES), 1) >= RH
    return (r == c).astype(F32)


def _head_sum(x, ones_bd):
    parts = [jnp.dot(x[:, i * LANES:(i + 1) * LANES], ones_bd, precision=HI, preferred_element_type=F32)
             for i in range(x.shape[1] // LANES)]
    return jnp.concatenate(parts, axis=1)


def _rwkv_prep_kernel(p_ref, prev_ref, next_ref, mu_ref, kkw_ref, w0_ref, w2_ref, a0_ref, a2_ref, ka_ref,
                      rk_ref, g2_ref,
                      r_o, v_o, kk_o, lwf_o, lwb_o, kdf_o, kdb_o, bqf_o, bqb_o, bonus_o, g_o,
                      *, nb_ctx, bps_ctx, bps_lat):
    i = pl.program_id(0)
    is_lat = i >= nb_ctx
    pos = jnp.where(is_lat, lax.rem(i - nb_ctx, bps_lat), lax.rem(i, bps_ctx))
    last = jnp.where(is_lat, bps_lat - 1, bps_ctx - 1)
    has_prev = (pos != 0).astype(F32)
    has_next = (pos != last).astype(F32)
    rows = lax.broadcasted_iota(jnp.int32, (RB, 1), 0)

    def shifted(lo, hi):
        p = p_ref[:, lo:hi]
        prev = jnp.where(rows == 0, prev_ref[7:8, lo:hi] * has_prev, pltpu.roll(p, 1, axis=0))
        nxt = jnp.where(rows == RB - 1, next_ref[0:1, lo:hi] * has_next, pltpu.roll(p, RB - 1, axis=0))
        return p + (0.5 * (prev + nxt) - p) * mu_ref[:, lo:hi]

    ones_bd = _head_ones()
    r = shifted(0, D)
    k = shifted(D, 2 * D)
    v = shifted(2 * D, 3 * D)
    r_o[...] = r
    v_o[...] = v
    kkf = k * kkw_ref[...]
    nrm = jnp.sqrt(_head_sum(kkf * kkf, ones_bd))
    kk = kkf / jnp.maximum(nrm, 1e-12)
    kk_o[...] = kk
    bonus_o[...] = _head_sum(r * k * rk_ref[...], ones_bd) * v
    base = 3 * D
    outs = ((lwf_o, kdf_o, bqf_o), (lwb_o, kdb_o, bqb_o))
    for d in range(2):
        wl = shifted(base + d * LORA_PAD, base + (d + 1) * LORA_PAD)
        z = w0_ref[d:d + 1, :] + jnp.dot(jnp.tanh(wl).astype(BF16), w2_ref[d], preferred_element_type=F32)
        y = -z
        softplus = jnp.maximum(y, 0.0) + jnp.log(1.0 + jnp.exp(-jnp.abs(y)))
        logw = -softplus - 0.5
        outs[d][0][...] = -jnp.exp(logw)
        al = shifted(base + (2 + d) * LORA_PAD, base + (3 + d) * LORA_PAD)
        a = _sigmoid(a0_ref[d:d + 1, :] + jnp.dot(al.astype(BF16), a2_ref[d], preferred_element_type=F32))
        outs[d][1][...] = k * (1.0 + (a - 1.0) * ka_ref[d:d + 1, :])
        outs[d][2][...] = kk * a
    gl = shifted(base + 4 * LORA_PAD, base + 4 * LORA_PAD + GATE_LORA)
    g_o[...] = jnp.dot(_sigmoid(gl).astype(BF16), g2_ref[...], preferred_element_type=F32)


def rwkv_prep(p_rw, mu, kkw, w0, w2, a0, a2, ka, rk, g2, nb_ctx, bps_ctx, bps_lat):
    n = p_rw.shape[0]
    nb = n // RB
    sub = RB // 8
    row = lambda i: (i, 0)
    const2 = lambda i: (0, 0)
    out = jax.ShapeDtypeStruct((n, D), F32)
    return pl.pallas_call(
        functools.partial(_rwkv_prep_kernel, nb_ctx=nb_ctx, bps_ctx=bps_ctx, bps_lat=bps_lat),
        grid=(nb,),
        in_specs=[pl.BlockSpec((RB, RW_COLS), row),
                  pl.BlockSpec((8, RW_COLS), lambda i: (jnp.maximum(i * sub - 1, 0), 0)),
                  pl.BlockSpec((8, RW_COLS), lambda i: (jnp.minimum((i + 1) * sub, nb * sub - 1), 0)),
                  pl.BlockSpec((1, RW_COLS), const2),
                  pl.BlockSpec((1, D), const2),
                  pl.BlockSpec((2, D), const2),
                  pl.BlockSpec((2, LORA_PAD, D), lambda i: (0, 0, 0)),
                  pl.BlockSpec((2, D), const2),
                  pl.BlockSpec((2, LORA_PAD, D), lambda i: (0, 0, 0)),
                  pl.BlockSpec((2, D), const2),
                  pl.BlockSpec((1, D), const2),
                  pl.BlockSpec((GATE_LORA, D), const2)],
        out_specs=[pl.BlockSpec((RB, D), row)] * 11,
        out_shape=[out] * 11,
        compiler_params=_cp("arbitrary"),
        name="rwkv_prep",
    )(p_rw, p_rw, p_rw, mu, kkw, w0, w2, a0, a2, ka, rk, g2)


def _mm(eq, a, b):
    return jnp.einsum(eq, a.astype(BF16), b.astype(BF16), preferred_element_type=F32)


def _split_bf16(x):
    hi = x.astype(BF16)
    return hi, (x - hi.astype(F32)).astype(BF16)


def _mm3(eq, a, b):
    ah, al = _split_bf16(a)
    bh, bl = _split_bf16(b)
    mm = lambda x, y: jnp.einsum(eq, x, y, preferred_element_type=F32)
    return mm(ah, bh) + (mm(ah, bl) + mm(al, bh))


_NN = "hmk,hkn->hmn"
_NT = "hmk,hnk->hmn"
_TN = "hkm,hkn->hmn"


def _scan_sum(x, reverse):
    rows = lax.broadcasted_iota(jnp.int32, (SC, 1), 0)
    shift = 1
    while shift < SC:
        if reverse:
            x = x + jnp.where(rows < SC - shift, pltpu.roll(x, SC - shift, axis=0), 0.0)
        else:
            x = x + jnp.where(rows >= shift, pltpu.roll(x, shift, axis=0), 0.0)
        shift *= 2
    return x


def _wkv_chunk(refs, s, reverse):
    r, v, kk, lw, kd, bq = (x[...] for x in refs)
    last = 0 if reverse else SC - 1
    L = _scan_sum(lw, reverse)
    ltot = L[last:last + 1, :]
    en = jnp.exp(-L)
    et = jnp.exp(ltot - L)

    def heads(x):
        return jnp.stack([x[:, h * RH:(h + 1) * RH] for h in range(HG)], axis=0)

    ax = heads(-kk * jnp.exp(L - lw))
    rt = heads(r * jnp.exp(L))
    bt = heads(bq * en)
    kt = heads(kd * en)
    bc = heads(bq * et)
    kc = heads(kd * et)
    ptot = heads(jnp.exp(ltot))
    v = heads(v)

    ti = lax.broadcasted_iota(jnp.int32, (SC, SC), 0)
    si = lax.broadcasted_iota(jnp.int32, (SC, SC), 1)
    strict = (si > ti) if reverse else (si < ti)
    incl = (si >= ti) if reverse else (si <= ti)
    both = jnp.concatenate([strict, incl], axis=0)
    lhs = jnp.concatenate([ax, rt], axis=1)
    sb = jnp.where(both, _mm3(_NT, lhs, bt), 0.0)
    sk = jnp.where(both, _mm3(_NT, lhs, kt), 0.0)
    a_ab, a_rb = sb[:, :SC], sb[:, SC:]
    a_ak, a_rk = sk[:, :SC], sk[:, SC:]
    npow = a_ab
    e = a_ab
    for _ in range(int(math.log2(SC)) - 1):
        npow = _mm(_NN, npow, npow)
        e = e + npow + _mm(_NN, e, npow)
    ws = _mm3(_NT, lhs, s)
    w = ws[:, :SC] + _mm3(_NN, a_ak, v)
    u = w + _mm(_NN, e, w)
    rho = (w - u) + _mm3(_NN, a_ab, u)
    u = u + rho + _mm(_NN, e, rho)
    y = ws[:, SC:] + _mm(_NN, a_rb, u) + _mm(_NN, a_rk, v)
    s_new = s * ptot + _mm3(_TN, jnp.concatenate([u, v], axis=1), jnp.concatenate([bc, kc], axis=1))
    return jnp.concatenate([y[h] for h in range(HG)], axis=1), s_new


def _scan_kernel(rf, vf, kkf, lwf, kdf, bqf, rb, vb, kkb, lwb, kdb, bqb, s0f, s0b,
                 yf_ref, yb_ref, sfo, sbo, stf, stb, *, nc):
    c = pl.program_id(2)

    @pl.when(c == 0)
    def _():
        stf[...] = s0f[...]
        stb[...] = s0b[...]

    for reverse, refs, st, yref in ((False, (rf, vf, kkf, lwf, kdf, bqf), stf, yf_ref),
                                    (True, (rb, vb, kkb, lwb, kdb, bqb), stb, yb_ref)):
        y, s_new = _wkv_chunk(refs, st[...], reverse)
        st[...] = s_new
        yref[...] = y

    @pl.when(c == nc - 1)
    def _():
        sfo[...] = stf[...]
        sbo[...] = stb[...]


def rwkv_scan(r, v, kk, lwf, lwb, kdf, kdb, bqf, bqb, s0f, s0b, row0, n_seq, t_len, name):
    nc = t_len // SC
    off = row0 // SC
    cw = HG * RH
    fwd = lambda s, h, c: (off + s * nc + c, h)
    bwd = lambda s, h, c: (off + s * nc + (nc - 1 - c), h)
    blk = lambda im: pl.BlockSpec((SC, cw), im)
    st_spec = pl.BlockSpec((None, HG, RH, RH), lambda s, h, c: (s, h, 0, 0))
    n_rows = n_seq * t_len
    return pl.pallas_call(
        functools.partial(_scan_kernel, nc=nc),
        grid=(n_seq, RHEADS // HG, nc),
        in_specs=[blk(fwd)] * 6 + [blk(bwd)] * 6 + [st_spec, st_spec],
        out_specs=[pl.BlockSpec((SC, cw), lambda s, h, c: (s * nc + c, h)),
                   pl.BlockSpec((SC, cw), lambda s, h, c: (s * nc + (nc - 1 - c), h)),
                   st_spec, st_spec],
        out_shape=[jax.ShapeDtypeStruct((n_rows, D), F32), jax.ShapeDtypeStruct((n_rows, D), F32),
                   jax.ShapeDtypeStruct(s0f.shape, F32), jax.ShapeDtypeStruct(s0b.shape, F32)],
        scratch_shapes=[pltpu.VMEM((HG, RH, RH), F32), pltpu.VMEM((HG, RH, RH), F32)],
        compiler_params=_cp("arbitrary", "arbitrary", "arbitrary"),
        name=name,
    )(r, v, kk, lwf, kdf, bqf, r, v, kk, lwb, kdb, bqb, s0f, s0b)


def _rwkv_fin_kernel(yf_ref, yb_ref, bonus_ref, g_ref, lg_ref, lb_ref, o_ref):
    ones_bd = _head_ones()
    y = yf_ref[...] + yb_ref[...]
    mu = _head_sum(y, ones_bd) * (1.0 / RH)
    yc = y - mu
    var = _head_sum(yc * yc, ones_bd) * (1.0 / RH)
    yn = yc * lax.rsqrt(var + GN_EPS) * lg_ref[...] + lb_ref[...]
    o_ref[...] = ((yn + bonus_ref[...]) * g_ref[...]).astype(BF16)


def rwkv_finish(yf, yb, bonus, g, lnx_g, lnx_b):
    n = yf.shape[0]
    row = lambda i: (i, 0)
    const2 = lambda i: (0, 0)
    return pl.pallas_call(
        _rwkv_fin_kernel,
        grid=(n // TB,),
        in_specs=[pl.BlockSpec((TB, D), row)] * 4 + [pl.BlockSpec((1, D), const2)] * 2,
        out_specs=pl.BlockSpec((TB, D), row),
        out_shape=jax.ShapeDtypeStruct((n, D), BF16),
        compiler_params=_cp("arbitrary"),
        name="rwkv_finish",
    )(yf, yb, bonus, g, lnx_g, lnx_b)


def _merge_kernel(oa_ref, ob_ref, oc_ref, w_ref, ga_ref, gb_ref, gc_ref, z_ref):
    acc = None
    for n, (o_ref, g_ref) in enumerate(((oa_ref, ga_ref), (ob_ref, gb_ref), (oc_ref, gc_ref))):
        y = _sigmoid(g_ref[...]) * jnp.dot(o_ref[...], w_ref[n], preferred_element_type=F32)
        acc = y if acc is None else acc + y
    z_ref[...] = acc.astype(BF16)


def branch_merge(o_a, o_b, o_c, w_branch, p_gate):
    n = o_a.shape[0]
    tn = 512
    tm = 512 if n % 512 == 0 else TB
    nj = D // tn
    o_spec = pl.BlockSpec((tm, D), lambda j, i: (i, 0))
    gate = lambda b: pl.BlockSpec((tm, tn), lambda j, i: (i, b * nj + j))
    return pl.pallas_call(
        _merge_kernel,
        grid=(nj, n // tm),
        in_specs=[o_spec, o_spec, o_spec,
                  pl.BlockSpec((3, D, tn), lambda j, i: (0, 0, j)),
                  gate(0), gate(1), gate(2)],
        out_specs=pl.BlockSpec((tm, tn), lambda j, i: (i, j)),
        out_shape=jax.ShapeDtypeStruct((n, D), BF16),
        compiler_params=_cp("arbitrary", "arbitrary"),
        name="branch_merge",
    )(o_a, o_b, o_c, w_branch, p_gate, p_gate, p_gate)


def _oproj_kernel(z_ref, wo_ref, x_ref, mod_ref, g_ref, b_ref, x1_ref, h2_ref):
    mix = jnp.dot(z_ref[...], wo_ref[...], preferred_element_type=F32)
    m = mod_ref[...]
    x1 = _ln(ALPHA * x_ref[...] + m[2:3] * mix) * g_ref[...] + b_ref[...]
    x1_ref[...] = x1
    h2_ref[...] = _ln(x1) * (1.0 + m[4:5]) + m[3:4]


def out_proj(z, w_o, x, mod, modrow, ln_g, ln_b):
    n = z.shape[0]
    row = lambda i: (i, 0)
    const2 = lambda i: (0, 0)
    return pl.pallas_call(
        _oproj_kernel,
        grid=(n // TB,),
        in_specs=[pl.BlockSpec((TB, D), row),
                  pl.BlockSpec((D, D), const2),
                  pl.BlockSpec((TB, D), row),
                  pl.BlockSpec((None, 6, D), lambda i: (modrow(i), 0, 0)),
                  pl.BlockSpec((1, D), const2),
                  pl.BlockSpec((1, D), const2)],
        out_specs=[pl.BlockSpec((TB, D), row), pl.BlockSpec((TB, D), row)],
        out_shape=[jax.ShapeDtypeStruct((n, D), F32), jax.ShapeDtypeStruct((n, D), F32)],
        compiler_params=_cp("arbitrary"),
        name="out_proj_ln",
    )(z, w_o, x, mod, ln_g, ln_b)


def _router_kernel(h_ref, w_ref, b_ref, idx_ref, wts_ref, hist_ref):
    logits = jnp.dot(h_ref[...], w_ref[...], precision=HI, preferred_element_type=F32) + b_ref[...]
    lane = lax.broadcasted_iota(jnp.int32, (TB, LANES), 1).astype(F32)
    cur = logits
    idx_out = jnp.zeros((TB, LANES), F32)
    val_out = jnp.zeros((TB, LANES), F32)
    hist = jnp.zeros((TB, LANES), F32)
    top = None
    for k in range(TOP_K):
        m = jnp.max(cur, -1, keepdims=True)
        am = jnp.min(jnp.where(cur == m, lane, float(LANES)), -1, keepdims=True)
        sel = lane == am
        top = m if top is None else top
        idx_out = jnp.where(lane == k, am, idx_out)
        val_out = jnp.where(lane == k, jnp.exp(m - top), val_out)
        hist = hist + sel.astype(F32)
        cur = jnp.where(sel, -jnp.inf, cur)
    idx_ref[...] = idx_out.astype(jnp.int32)
    wts_ref[...] = val_out * (1.0 / jnp.sum(val_out, -1, keepdims=True))
    hist_ref[...] = jnp.sum(hist, 0, keepdims=True)


def router(h2, rw_pad, rb_pad):
    n = h2.shape[0]
    nb = n // TB
    return pl.pallas_call(
        _router_kernel,
        grid=(nb,),
        in_specs=[pl.BlockSpec((TB, D), lambda i: (i, 0)),
                  pl.BlockSpec((D, LANES), lambda i: (0, 0)),
                  pl.BlockSpec((1, LANES), lambda i: (0, 0))],
        out_specs=[pl.BlockSpec((TB, LANES), lambda i: (i, 0)),
                   pl.BlockSpec((TB, LANES), lambda i: (i, 0)),
                   pl.BlockSpec((None, 1, LANES), lambda i: (i, 0, 0))],
        out_shape=[jax.ShapeDtypeStruct((n, LANES), jnp.int32),
                   jax.ShapeDtypeStruct((n, LANES), F32),
                   jax.ShapeDtypeStruct((nb, 1, LANES), F32)],
        compiler_params=_cp("arbitrary"),
        name="moe_router",
    )(h2, rw_pad, rb_pad)


def _dest_kernel(idx_ref, base_ref, dest_ref):
    lane = lax.broadcasted_iota(jnp.int32, (TB, LANES), 1).astype(F32)
    ri = lax.broadcasted_iota(jnp.int32, (TB, TB), 0)
    ci = lax.broadcasted_iota(jnp.int32, (TB, TB), 1)
    lower = (ci < ri).astype(BF16)
    idx = idx_ref[...].astype(F32)
    run = base_ref[...]
    dest = jnp.zeros((TB, LANES), F32)
    for k in range(TOP_K):
        e_k = jnp.sum(jnp.where(lane == k, idx, 0), -1, keepdims=True)
        onehot = lane == e_k
        oh = onehot.astype(F32)
        before = jnp.dot(lower, oh.astype(BF16), preferred_element_type=F32)
        d_k = jnp.sum(jnp.where(onehot, run + before, 0.0), -1, keepdims=True)
        dest = jnp.where(lane == k, d_k, dest)
        run = run + jnp.sum(oh, 0, keepdims=True)
    dest_ref[...] = dest.astype(jnp.int32)


def moe_dest(idx, base):
    n = idx.shape[0]
    nb = n // TB
    return pl.pallas_call(
        _dest_kernel,
        grid=(nb,),
        in_specs=[pl.BlockSpec((TB, LANES), lambda i: (i, 0)),
                  pl.BlockSpec((None, 1, LANES), lambda i: (i, 0, 0))],
        out_specs=pl.BlockSpec((TB, LANES), lambda i: (i, 0)),
        out_shape=jax.ShapeDtypeStruct((n, LANES), jnp.int32),
        compiler_params=_cp("arbitrary"),
        name="moe_dest",
    )(idx, base)


def _row_copy(src, src_row, dst, dst_row, sem):
    return pltpu.make_async_copy(src.at[pl.ds(src_row, 1)], dst.at[pl.ds(dst_row, 1)], sem)


def _dispatch_kernel(dest_ref, h_ref, xb_in, xb_out, sem):
    del xb_in

    def issue(t, carry):
        for k in range(TOP_K):
            _row_copy(h_ref, t, xb_out, dest_ref[0, t * TOP_K + k], sem).start()
        return carry

    lax.fori_loop(0, TB, issue, 0)

    def drain(t, carry):
        for k in range(TOP_K):
            _row_copy(h_ref, t, xb_out, dest_ref[0, t * TOP_K + k], sem).wait()
        return carry

    lax.fori_loop(0, TB, drain, 0)


def moe_dispatch(dest_flat, h2, xb_init):
    n = h2.shape[0]
    nb = n // TB
    return pl.pallas_call(
        _dispatch_kernel,
        grid=(nb,),
        in_specs=[pl.BlockSpec((None, 1, TB * TOP_K), lambda i: (i, 0, 0), memory_space=pltpu.SMEM),
                  pl.BlockSpec((TB, D), lambda i: (i, 0)),
                  pl.BlockSpec(memory_space=pl.ANY)],
        out_specs=pl.BlockSpec(memory_space=pl.ANY),
        out_shape=jax.ShapeDtypeStruct(xb_init.shape, xb_init.dtype),
        scratch_shapes=[pltpu.SemaphoreType.DMA(())],
        input_output_aliases={2: 0},
        compiler_params=_cp("arbitrary"),
        name="moe_dispatch",
    )(dest_flat, h2, xb_init)


def _expert_changed(be_ref):
    i = pl.program_id(1)
    return jnp.logical_or(i == 0, be_ref[i] != be_ref[jnp.maximum(i - 1, 0)])


def _expert_up_kernel(be_ref, nv_ref, x_ref, wg_ref, bg_ref, wu_ref, bu_ref, act_ref, wg_bf, wu_bf):
    used = pl.program_id(1) < nv_ref[0]

    @pl.when(jnp.logical_not(used))
    def _():
        act_ref[...] = jnp.zeros_like(act_ref)

    @pl.when(jnp.logical_and(used, _expert_changed(be_ref)))
    def _():
        wg_bf[...] = wg_ref[...].astype(BF16)
        wu_bf[...] = wu_ref[...].astype(BF16)

    @pl.when(used)
    def _():
        xb = x_ref[...].astype(BF16)
        gl = jnp.minimum(jnp.dot(xb, wg_bf[...], preferred_element_type=F32) + bg_ref[...], SWIGLU_LIMIT)
        lin = jnp.clip(jnp.dot(xb, wu_bf[...], preferred_element_type=F32) + bu_ref[...],
                       -SWIGLU_LIMIT, SWIGLU_LIMIT)
        act_ref[...] = (gl * _sigmoid(SWIGLU_ALPHA * gl) * (lin + 1.0)).astype(BF16)


def expert_up(block_e, n_valid, xb, w_gate, b_gate, w_up, b_up, layer):
    cap = xb.shape[0]
    tn = EXPERT_TN
    rowblk = lambda j, i, be, nv: (jnp.minimum(i, nv[0] - 1), 0)
    wspec = pl.BlockSpec((None, None, D, tn), lambda j, i, be, nv: (layer, be[i], 0, j))
    bspec = pl.BlockSpec((None, None, 1, tn), lambda j, i, be, nv: (layer, be[i], 0, j))
    return pl.pallas_call(
        _expert_up_kernel,
        grid_spec=pltpu.PrefetchScalarGridSpec(
            num_scalar_prefetch=2,
            grid=(D // tn, cap // BM),
            in_specs=[pl.BlockSpec((BM, D), rowblk), wspec, bspec, wspec, bspec],
            out_specs=pl.BlockSpec((BM, tn), lambda j, i, be, nv: (i, j)),
            scratch_shapes=[pltpu.VMEM((D, tn), BF16), pltpu.VMEM((D, tn), BF16)]),
        out_shape=jax.ShapeDtypeStruct((cap, D), BF16),
        compiler_params=pltpu.CompilerParams(dimension_semantics=("arbitrary", "arbitrary"),
                                             vmem_limit_bytes=EXPERT_VMEM_LIMIT),
        name="moe_expert_up",
    )(block_e, n_valid, xb, w_gate, b_gate, w_up, b_up)


def _expert_down_kernel(be_ref, nv_ref, a_ref, wd_ref, bd_ref, y_ref, wd_bf):
    used = pl.program_id(1) < nv_ref[0]

    @pl.when(jnp.logical_not(used))
    def _():
        y_ref[...] = jnp.zeros_like(y_ref)

    @pl.when(jnp.logical_and(used, _expert_changed(be_ref)))
    def _():
        wd_bf[...] = wd_ref[...].astype(BF16)

    @pl.when(used)
    def _():
        y_ref[...] = jnp.dot(a_ref[...], wd_bf[...], preferred_element_type=F32) + bd_ref[...]


def expert_down(block_e, n_valid, act, w_down, b_down, layer):
    cap = act.shape[0]
    tn = EXPERT_TN
    return pl.pallas_call(
        _expert_down_kernel,
        grid_spec=pltpu.PrefetchScalarGridSpec(
            num_scalar_prefetch=2,
            grid=(D // tn, cap // BM),
            in_specs=[pl.BlockSpec((BM, D), lambda j, i, be, nv: (jnp.minimum(i, nv[0] - 1), 0)),
                      pl.BlockSpec((None, None, D, tn), lambda j, i, be, nv: (layer, be[i], 0, j)),
                      pl.BlockSpec((None, None, 1, tn), lambda j, i, be, nv: (layer, be[i], 0, j))],
            out_specs=pl.BlockSpec((BM, tn), lambda j, i, be, nv: (i, j)),
            scratch_shapes=[pltpu.VMEM((D, tn), BF16)]),
        out_shape=jax.ShapeDtypeStruct((cap, D), F32),
        compiler_params=pltpu.CompilerParams(dimension_semantics=("arbitrary", "arbitrary"),
                                             vmem_limit_bytes=EXPERT_VMEM_LIMIT),
        name="moe_expert_down",
    )(block_e, n_valid, act, w_down, b_down)


def _combine_kernel(dest_ref, wts_ref, x1_ref, mod_ref, g_ref, b_ref, modn_ref, yb_ref,
                    x2_ref, hn_ref, rows, sem):
    def issue(t, carry):
        for k in range(TOP_K):
            _row_copy(yb_ref, dest_ref[0, t * TOP_K + k], rows.at[k], t, sem).start()
        return carry

    lax.fori_loop(0, TB, issue, 0)

    def drain(t, carry):
        for k in range(TOP_K):
            _row_copy(yb_ref, dest_ref[0, t * TOP_K + k], rows.at[k], t, sem).wait()
        return carry

    lax.fori_loop(0, TB, drain, 0)
    w = wts_ref[...]
    ffn = rows[0] * w[:, 0:1]
    for k in range(1, TOP_K):
        ffn = ffn + rows[k] * w[:, k:k + 1]
    m = mod_ref[...]
    x2 = _ln(ALPHA * x1_ref[...] + m[5:6] * ffn) * g_ref[...] + b_ref[...]
    x2_ref[...] = x2
    mn = modn_ref[...]
    hn_ref[...] = (_ln(x2) * (1.0 + mn[1:2]) + mn[0:1]).astype(BF16)


def moe_combine(dest_flat, wts, x1, mod, mod_next, modrow, ln_g, ln_b, yb):
    n = x1.shape[0]
    row = lambda i: (i, 0)
    const2 = lambda i: (0, 0)
    modspec = pl.BlockSpec((None, 6, D), lambda i: (modrow(i), 0, 0))
    return pl.pallas_call(
        _combine_kernel,
        grid=(n // TB,),
        in_specs=[pl.BlockSpec((None, 1, TB * TOP_K), lambda i: (i, 0, 0), memory_space=pltpu.SMEM),
                  pl.BlockSpec((TB, LANES), row),
                  pl.BlockSpec((TB, D), row),
                  modspec,
                  pl.BlockSpec((1, D), const2),
                  pl.BlockSpec((1, D), const2),
                  modspec,
                  pl.BlockSpec(memory_space=pl.ANY)],
        out_specs=[pl.BlockSpec((TB, D), row), pl.BlockSpec((TB, D), row)],
        out_shape=[jax.ShapeDtypeStruct((n, D), F32), jax.ShapeDtypeStruct((n, D), BF16)],
        scratch_shapes=[pltpu.VMEM((TOP_K, TB, D), F32), pltpu.SemaphoreType.DMA(())],
        compiler_params=_cp("arbitrary"),
        name="moe_combine",
    )(dest_flat, wts, x1, mod, ln_g, ln_b, mod_next, yb)


def moe_layout(hist):
    h = hist[:, 0, :].astype(jnp.int32)
    counts = jnp.sum(h, 0)
    padded = (counts + BM - 1) // BM * BM
    pad_end = jnp.cumsum(padded)
    pad_start = pad_end - padded
    before = jnp.cumsum(h, 0) - h
    base = (pad_start[None, :] + before).astype(F32)[:, None, :]
    return base, pad_end


def _rot_cols(w):
    q = ROPE // 4
    return jnp.concatenate([-w[..., q:2 * q], w[..., 0:q], -w[..., 3 * q:4 * q], w[..., 2 * q:3 * q]], axis=-1)


def _rope_tables(n):
    rows = n // GRID_W
    row_id = jnp.repeat(jnp.arange(rows, dtype=F32), GRID_W)
    col_id = jnp.tile(jnp.arange(GRID_W, dtype=F32), rows)
    half = ROPE // 2
    inv_freq = ROPE_THETA ** (-jnp.arange(0, half, 2, dtype=F32) / half)
    ang_r = row_id[:, None] * inv_freq
    ang_c = col_id[:, None] * inv_freq
    cos = jnp.concatenate([jnp.cos(ang_r)] * 2 + [jnp.cos(ang_c)] * 2, axis=-1)
    sin = jnp.concatenate([jnp.sin(ang_r)] * 2 + [jnp.sin(ang_c)] * 2, axis=-1)
    return cos, sin


def _pad_rows(w, rows):
    return jnp.pad(w, [(0, 0)] * (w.ndim - 2) + [(0, rows - w.shape[-2]), (0, 0)])


def kernel(x_prompt, x_sample, cache_ckv, cache_krope, state_rwkv_fwd, state_rwkv_bwd, c, c_ctx, w_mod, b_mod, w_in, gmlp_ln_g, gmlp_ln_b, gmlp_ws, gmlp_bs, q_norm_g, w_uq, kv_norm_g, w_ukv, rwkv_mu, rwkv_w0, rwkv_w2, rwkv_a0, rwkv_a2, rwkv_g2, rwkv_kk, rwkv_ka, rwkv_rk, rwkv_lnx_g, rwkv_lnx_b, w_branch, w_o, ln1_g, ln1_b, router_w, router_b, w_gate, b_gate, w_up, b_up, w_down, b_down, ln2_g, ln2_b):
    L = w_mod.shape[0]
    b_ctx, seq, _ = x_prompt.shape
    b_lat, t_lat, _ = x_sample.shape
    past = cache_ckv.shape[2]
    n_ctx = b_ctx * seq
    n_lat = b_lat * t_lat
    n = n_ctx + n_lat
    assert seq % TB == 0 and t_lat % TB == 0 and past % TB == 0 and b_lat + 1 <= 8
    nb_ctx = n_ctx // TB
    bps_lat = t_lat // TB

    def modrow(i):
        return jnp.where(i < nb_ctx, 0, 1 + (i - nb_ctx) // bps_lat)

    cond8 = jnp.zeros((8, D), F32).at[0].set(c_ctx).at[1:1 + b_lat].set(c)
    mod_all = modulation(cond8, w_mod, b_mod).reshape(L, 8, 6, D)

    gm_end = 2 * D
    mla_end = gm_end + Q_LORA + KV_LORA + ROPE
    rw_end = mla_end + 3 * D + 2 * DECAY_LORA + 2 * AAA_LORA + GATE_LORA
    w_gm = w_in[:, :, :gm_end].astype(BF16)
    w_kr = w_in[:, :, mla_end - ROPE:mla_end]
    w_mla = jnp.concatenate([w_in[:, :, gm_end:mla_end], _rot_cols(w_kr)], axis=-1).astype(BF16)
    rw = w_in[:, :, mla_end:rw_end]
    segs = [rw[:, :, :3 * D]]
    mu_segs = [rwkv_mu[:, :3 * D]]
    o = 3 * D
    for width in (DECAY_LORA, DECAY_LORA, AAA_LORA, AAA_LORA):
        segs.append(jnp.pad(rw[:, :, o:o + width], ((0, 0), (0, 0), (0, LORA_PAD - width))))
        mu_segs.append(jnp.pad(rwkv_mu[:, o:o + width], ((0, 0), (0, LORA_PAD - width))))
        o += width
    segs.append(rw[:, :, o:])
    mu_segs.append(rwkv_mu[:, o:])
    w_rw = jnp.concatenate(segs, axis=-1).astype(BF16)
    mu_rw = jnp.concatenate(mu_segs, axis=-1)[:, None, :]
    w_gt = w_in[:, :, rw_end:].astype(BF16)
    w2p = _pad_rows(rwkv_w2, LORA_PAD).astype(BF16)
    a2p = _pad_rows(rwkv_a2, LORA_PAD).astype(BF16)
    g2b = rwkv_g2.astype(BF16)
    wq4 = w_uq.reshape(L, Q_LORA, HEADS, QK_DIM)
    wq_r = wq4[..., NOPE:]
    wq = jnp.concatenate([wq4[..., :NOPE], wq_r, _rot_cols(wq_r)], axis=-1)
    wq = wq.reshape(L, Q_LORA, HEADS * 2 * LANES).astype(BF16)
    wkv4 = w_ukv.reshape(L, KV_LORA, HEADS, NOPE + V_DIM)
    wk = wkv4[..., :NOPE].reshape(L, KV_LORA, D).astype(BF16)
    wv = wkv4[..., NOPE:].reshape(L, KV_LORA, D).astype(BF16)
    ws_b = gmlp_ws.astype(BF16)
    bs_full = jnp.repeat(jnp.swapaxes(gmlp_bs, 1, 2), GMLP_GW, axis=2)
    w_br = w_branch.astype(BF16)
    w_ob = w_o.astype(BF16)
    rw_pad = jnp.pad(router_w, ((0, 0), (0, 0), (0, LANES - N_EXPERTS)))
    rb_pad = jnp.pad(router_b, ((0, 0), (0, LANES - N_EXPERTS)), constant_values=-1e30)[:, None, :]
    bg4 = b_gate[:, :, None, :]
    bu4 = b_up[:, :, None, :]
    bd4 = b_down[:, :, None, :]

    cos, sin = _rope_tables(t_lat)
    cos_all = jnp.concatenate([jnp.ones((n_ctx, ROPE), F32), jnp.tile(cos, (b_lat, 1))], axis=0)
    sin_all = jnp.concatenate([jnp.zeros((n_ctx, ROPE), F32), jnp.tile(sin, (b_lat, 1))], axis=0)
    tabk = jnp.concatenate([cos_all, sin_all], axis=-1)
    tabq = jnp.concatenate([jnp.ones((n, NOPE), F32), tabk], axis=-1)
    tab_cache = jnp.concatenate([jnp.ones((b_lat * past, ROPE), F32), jnp.zeros((b_lat * past, ROPE), F32)], -1)

    zeros_state = jnp.zeros((b_ctx, RHEADS, RH, RH), F32)
    sf_lat = state_rwkv_fwd
    sb_lat = state_rwkv_bwd

    cap = (n * TOP_K // BM + N_EXPERTS) * BM
    x = jnp.concatenate([x_prompt.reshape(n_ctx, D), x_sample.reshape(n_lat, D)], axis=0)
    h = ln_modulate(x, mod_all[0], modrow)
    ckv_out, kr_out, sf_out, sb_out = [], [], [], []
    for l in range(L):
        mod = mod_all[l]
        p_gm = matmul(h, w_gm[l], 1024, "in_proj_gmlp")
        p_mla = matmul(h, w_mla[l], MLA_COLS, "in_proj_mla")
        p_rw = matmul(h, w_rw[l], 768, "in_proj_rwkv")
        p_gate = matmul(h, w_gt[l], 1024, "in_proj_gate")

        o_a = gmlp(p_gm, gmlp_ln_g[l][None], gmlp_ln_b[l][None], ws_b[l], bs_full[l])

        q = q_proj(p_mla, q_norm_g[l][None], wq[l], tabq)
        kcat, vv, ckvn = kv_proj(p_mla, 1, p_mla, (Q_LORA + KV_LORA) // LANES, kv_norm_g[l][None],
                                 wk[l], wv[l], tabk, True)
        cache_c = cache_ckv[:, l].reshape(b_lat * past, KV_LORA)
        cache_r = jnp.pad(cache_krope[:, l].reshape(b_lat * past, ROPE), ((0, 0), (0, ROPE)))
        kcat_c, vv_c, _ = kv_proj(cache_c, 0, cache_r, 0, kv_norm_g[l][None], wk[l], wv[l], tab_cache, False)
        hw2 = HEADS * 2 * LANES
        kcat_lat = jnp.concatenate([kcat_c.reshape(b_lat, past, hw2), kcat[n_ctx:].reshape(b_lat, t_lat, hw2)],
                                   axis=1).reshape(b_lat * (past + t_lat), hw2)
        vv_lat = jnp.concatenate([vv_c.reshape(b_lat, past, D), vv[n_ctx:].reshape(b_lat, t_lat, D)],
                                 axis=1).reshape(b_lat * (past + t_lat), D)
        o_b_ctx = attention(q, 0, kcat, vv, b_ctx, seq, seq, HEADS, "mla_attention_ctx")
        o_b_lat = attention(q, n_ctx, kcat_lat, vv_lat, b_lat, t_lat, past + t_lat, 1, "mla_attention_lat")
        o_b = jnp.concatenate([o_b_ctx, o_b_lat], axis=0)
        ckv_out.append(ckvn[:n_ctx].reshape(b_ctx, seq, KV_LORA))
        kr_out.append(p_mla[:n_ctx, Q_LORA + KV_LORA:Q_LORA + KV_LORA + ROPE].reshape(b_ctx, seq, ROPE))

        (r, v, kk, lwf, lwb, kdf, kdb, bqf, bqb, bonus, gg) = rwkv_prep(
            p_rw, mu_rw[l], rwkv_kk[l][None], rwkv_w0[l], w2p[l], rwkv_a0[l], a2p[l], rwkv_ka[l],
            rwkv_rk[l].reshape(1, D), g2b[l], n_ctx // RB, seq // RB, t_lat // RB)
        yf_c, yb_c, sf_c, sb_c = rwkv_scan(r, v, kk, lwf, lwb, kdf, kdb, bqf, bqb, zeros_state, zeros_state,
                                           0, b_ctx, seq, "rwkv_scan_ctx")
        yf_l, yb_l, _, _ = rwkv_scan(r, v, kk, lwf, lwb, kdf, kdb, bqf, bqb, sf_lat[:, l], sb_lat[:, l],
                                     n_ctx, b_lat, t_lat, "rwkv_scan_lat")
        sf_out.append(sf_c)
        sb_out.append(sb_c)
        o_c = rwkv_finish(jnp.concatenate([yf_c, yf_l], 0), jnp.concatenate([yb_c, yb_l], 0), bonus, gg,
                          rwkv_lnx_g[l][None], rwkv_lnx_b[l][None])

        z = branch_merge(o_a, o_b, o_c, w_br[l], p_gate)
        x1, h2 = out_proj(z, w_ob[l], x, mod, modrow, ln1_g[l][None], ln1_b[l][None])

        idx, wts, hist = router(h2, rw_pad[l], rb_pad[l])
        base, pad_end = moe_layout(hist)
        dest = moe_dest(idx, base)
        dest_flat = dest[:, :TOP_K].reshape(n // TB, 1, TB * TOP_K)
        xb = moe_dispatch(dest_flat, h2, jnp.zeros((cap, D), F32))
        n_valid = (pad_end[N_EXPERTS - 1] // BM).astype(jnp.int32)
        blk_start = jnp.minimum(jnp.arange(cap // BM, dtype=jnp.int32), n_valid - 1) * BM
        block_e = jnp.sum((pad_end[None, :N_EXPERTS] <= blk_start[:, None]).astype(jnp.int32), axis=1)
        block_e = jnp.minimum(block_e, N_EXPERTS - 1)
        nv = n_valid.reshape(1)
        act = expert_up(block_e, nv, xb, w_gate, bg4, w_up, bu4, l)
        yb = expert_down(block_e, nv, act, w_down, bd4, l)
        mod_next = mod_all[min(l + 1, L - 1)]
        x, h = moe_combine(dest_flat, wts, x1, mod, mod_next, modrow, ln2_g[l][None], ln2_b[l][None], yb)

    y_prompt = x[:n_ctx].reshape(b_ctx, seq, D)
    y_sample = x[n_ctx:].reshape(b_lat, t_lat, D)
    return (y_prompt, y_sample, jnp.stack(ckv_out, axis=1), jnp.stack(kr_out, axis=1),
            jnp.stack(sf_out, axis=1), jnp.stack(sb_out, axis=1))
```

```python
import functools
import math

import jax
import jax.numpy as jnp
from jax import lax
from jax.experimental import pallas as pl
from jax.experimental.pallas import tpu as pltpu

F32 = jnp.float32
BF16 = jnp.bfloat16
HI = lax.Precision.HIGHEST

DEPTH_NORM = 4
GRID_W = 64
CHUNK = 128
D = 2048
GMLP_GW = 128
GMLP_GROUPS = D // GMLP_GW
V_DIM = 128
NOPE = 128
ROPE = 64
QK_DIM = NOPE + ROPE
HEADS = D // V_DIM
Q_LORA = 512
KV_LORA = 512
ROPE_THETA = 10000.0
RH = 64
RHEADS = D // RH
DECAY_LORA = 96
AAA_LORA = 96
GATE_LORA = 256
GN_EPS = 64e-5
N_EXPERTS = 32
TOP_K = 4
SWIGLU_LIMIT = 7.0
SWIGLU_ALPHA = 1.702
LN_EPS = 1e-5
RMS_EPS = 1e-6
ALPHA = (2 * DEPTH_NORM) ** 0.25

LANES = 128
VMEM_LIMIT = 48 * 1024 * 1024

TB = 256
RB = 128
SC = 64
HG = 16
BM = 256
EXPERT_TN = 1024
EXPERT_VMEM_LIMIT = 56 * 1024 * 1024
LORA_PAD = 128
RW_COLS = 3 * D + 4 * LORA_PAD + GATE_LORA
MLA_COLS = Q_LORA + KV_LORA + 2 * ROPE


def _cp(*sem):
    return pltpu.CompilerParams(dimension_semantics=sem, vmem_limit_bytes=VMEM_LIMIT)


def _ln(x):
    mu = jnp.mean(x, -1, keepdims=True)
    xc = x - mu
    var = jnp.mean(xc * xc, -1, keepdims=True)
    return xc * lax.rsqrt(var + LN_EPS)


def _sigmoid(x):
    return 1.0 / (1.0 + jnp.exp(-x))


def _gelu_tanh(x):
    return 0.5 * x * (1.0 + jnp.tanh(math.sqrt(2.0 / math.pi) * (x + 0.044715 * (x * x * x))))


def _mod_kernel(c_ref, w_ref, b_ref, o_ref):
    c = c_ref[...]
    s = (c * _sigmoid(c)).astype(BF16)
    o_ref[...] = jnp.dot(s, w_ref[...].astype(BF16), preferred_element_type=F32) + b_ref[...]


def modulation(cond8, w_mod, b_mod):
    L = w_mod.shape[0]
    tn = 1024
    return pl.pallas_call(
        _mod_kernel,
        grid=(L, 6 * D // tn),
        in_specs=[pl.BlockSpec((8, D), lambda l, j: (0, 0)),
                  pl.BlockSpec((None, D, tn), lambda l, j: (l, 0, j)),
                  pl.BlockSpec((None, 1, tn), lambda l, j: (l, 0, j))],
        out_specs=pl.BlockSpec((None, 8, tn), lambda l, j: (l, 0, j)),
        out_shape=jax.ShapeDtypeStruct((L, 8, 6 * D), F32),
        compiler_params=_cp("arbitrary", "arbitrary"),
        name="modulation",
    )(cond8, w_mod, b_mod.reshape(L, 1, 6 * D))


def _lnmod_kernel(x_ref, mod_ref, h_ref):
    m = mod_ref[...]
    h_ref[...] = (_ln(x_ref[...]) * (1.0 + m[1:2]) + m[0:1]).astype(BF16)


def ln_modulate(x, mod, modrow):
    n = x.shape[0]
    return pl.pallas_call(
        _lnmod_kernel,
        grid=(n // TB,),
        in_specs=[pl.BlockSpec((TB, D), lambda i: (i, 0)),
                  pl.BlockSpec((None, 6, D), lambda i: (modrow(i), 0, 0))],
        out_specs=pl.BlockSpec((TB, D), lambda i: (i, 0)),
        out_shape=jax.ShapeDtypeStruct((n, D), BF16),
        compiler_params=_cp("arbitrary"),
        name="ln_modulate",
    )(x, mod)


def _mm_kernel(x_ref, w_ref, o_ref):
    o_ref[...] = jnp.dot(x_ref[...], w_ref[...], preferred_element_type=F32).astype(o_ref.dtype)


def matmul(x, w, tn, name, out_dtype=F32):
    m, k = x.shape
    nn = w.shape[1]
    tm = 512 if m % 512 == 0 else TB
    return pl.pallas_call(
        _mm_kernel,
        grid=(nn // tn, m // tm),
        in_specs=[pl.BlockSpec((tm, k), lambda j, i: (i, 0)),
                  pl.BlockSpec((k, tn), lambda j, i: (0, j))],
        out_specs=pl.BlockSpec((tm, tn), lambda j, i: (i, j)),
        out_shape=jax.ShapeDtypeStruct((m, nn), out_dtype),
        compiler_params=_cp("arbitrary", "arbitrary"),
        name=name,
    )(x, w)


def _gmlp_kernel(p_ref, g_ref, b_ref, ws_ref, bs_ref, o_ref):
    u = _gelu_tanh(p_ref[:, :D])
    v = _gelu_tanh(p_ref[:, D:])
    v = (_ln(v) * g_ref[...] + b_ref[...]).astype(BF16)
    for g in range(GMLP_GROUPS):
        sl = slice(g * GMLP_GW, (g + 1) * GMLP_GW)
        mixed = jnp.dot(ws_ref[g], v[:, sl], preferred_element_type=F32) + bs_ref[:, sl]
        o_ref[:, sl] = (u[:, sl] * mixed).astype(BF16)


def gmlp(p_gm, ln_g, ln_b, ws, bs_full):
    n = p_gm.shape[0]
    return pl.pallas_call(
        _gmlp_kernel,
        grid=(n // CHUNK,),
        in_specs=[pl.BlockSpec((CHUNK, 2 * D), lambda i: (i, 0)),
                  pl.BlockSpec((1, D), lambda i: (0, 0)),
                  pl.BlockSpec((1, D), lambda i: (0, 0)),
                  pl.BlockSpec((GMLP_GROUPS, CHUNK, CHUNK), lambda i: (0, 0, 0)),
                  pl.BlockSpec((CHUNK, D), lambda i: (0, 0))],
        out_specs=pl.BlockSpec((CHUNK, D), lambda i: (i, 0)),
        out_shape=jax.ShapeDtypeStruct((n, D), BF16),
        compiler_params=_cp("arbitrary"),
        name="gmlp",
    )(p_gm, ln_g, ln_b, ws, bs_full)


def _qproj_kernel(cq_ref, g_ref, w_ref, tab_ref, q_ref):
    cq = cq_ref[...]
    n = (cq * lax.rsqrt(jnp.mean(cq * cq, -1, keepdims=True) + RMS_EPS) * g_ref[...]).astype(BF16)
    tab = tab_ref[...]
    hw = 2 * LANES
    for h in range(HEADS):
        sl = slice(h * hw, (h + 1) * hw)
        q_ref[:, sl] = (jnp.dot(n, w_ref[:, sl], preferred_element_type=F32) * tab).astype(BF16)


def q_proj(p_mla, g, wq, tabq):
    n = p_mla.shape[0]
    hw = 2 * LANES
    return pl.pallas_call(
        _qproj_kernel,
        grid=(n // TB,),
        in_specs=[pl.BlockSpec((TB, Q_LORA), lambda i: (i, 0)),
                  pl.BlockSpec((1, Q_LORA), lambda i: (0, 0)),
                  pl.BlockSpec((Q_LORA, HEADS * hw), lambda i: (0, 0)),
                  pl.BlockSpec((TB, hw), lambda i: (i, 0))],
        out_specs=pl.BlockSpec((TB, HEADS * hw), lambda i: (i, 0)),
        out_shape=jax.ShapeDtypeStruct((n, HEADS * hw), BF16),
        compiler_params=_cp("arbitrary"),
        name="mla_q_proj",
    )(p_mla, g, wq, tabq)


def _kvproj_kernel(ckv_ref, kr_ref, g_ref, wk_ref, wv_ref, tab_ref, kcat_ref, v_ref, ckvn_ref, *, normalize):
    ckv = ckv_ref[...]
    if normalize:
        ckv = ckv * lax.rsqrt(jnp.mean(ckv * ckv, -1, keepdims=True) + RMS_EPS) * g_ref[...]
    ckvn_ref[...] = ckv
    nb = ckv.astype(BF16)
    t = kr_ref[...] * tab_ref[...]
    khi = (t + pltpu.roll(t, ROPE, axis=1)).astype(BF16)
    kn = jnp.dot(nb, wk_ref[...], preferred_element_type=F32).astype(BF16)
    v_ref[...] = jnp.dot(nb, wv_ref[...], preferred_element_type=F32).astype(BF16)
    hw = 2 * LANES
    for h in range(HEADS):
        kcat_ref[:, h * hw:h * hw + LANES] = kn[:, h * LANES:(h + 1) * LANES]
        kcat_ref[:, h * hw + LANES:(h + 1) * hw] = khi


def kv_proj(ckv_src, ckv_col, kr_src, kr_col, g, wk, wv, tabk, normalize):
    n = ckv_src.shape[0]
    hw = 2 * LANES
    return pl.pallas_call(
        functools.partial(_kvproj_kernel, normalize=normalize),
        grid=(n // TB,),
        in_specs=[pl.BlockSpec((TB, KV_LORA), lambda i: (i, ckv_col)),
                  pl.BlockSpec((TB, LANES), lambda i: (i, kr_col)),
                  pl.BlockSpec((1, KV_LORA), lambda i: (0, 0)),
                  pl.BlockSpec((KV_LORA, D), lambda i: (0, 0)),
                  pl.BlockSpec((KV_LORA, D), lambda i: (0, 0)),
                  pl.BlockSpec((TB, LANES), lambda i: (i, 0))],
        out_specs=[pl.BlockSpec((TB, HEADS * hw), lambda i: (i, 0)),
                   pl.BlockSpec((TB, D), lambda i: (i, 0)),
                   pl.BlockSpec((TB, KV_LORA), lambda i: (i, 0))],
        out_shape=[jax.ShapeDtypeStruct((n, HEADS * hw), BF16),
                   jax.ShapeDtypeStruct((n, D), BF16),
                   jax.ShapeDtypeStruct((n, KV_LORA), F32)],
        compiler_params=_cp("arbitrary"),
        name="mla_kv_proj",
    )(ckv_src, kr_src, g, wk, wv, tabk)


def _attn_kernel(q_ref, k_ref, v_ref, o_ref, *, nh):
    hw = 2 * LANES
    for h in range(nh):
        s = lax.dot_general(q_ref[:, h * hw:(h + 1) * hw], k_ref[:, h * hw:(h + 1) * hw],
                            (((1,), (1,)), ((), ())), preferred_element_type=F32)
        s = s * (QK_DIM ** -0.5)
        e = jnp.exp(s - jnp.max(s, -1, keepdims=True))
        p = (e * (1.0 / jnp.sum(e, -1, keepdims=True))).astype(BF16)
        o_ref[:, h * V_DIM:(h + 1) * V_DIM] = jnp.dot(
            p, v_ref[:, h * V_DIM:(h + 1) * V_DIM], preferred_element_type=F32).astype(BF16)


def attention(q, q_row0, kcat, v, n_seq, t_q, n_keys, nh, name):
    hw = 2 * LANES
    qb = t_q // TB
    qoff = q_row0 // TB
    return pl.pallas_call(
        functools.partial(_attn_kernel, nh=nh),
        grid=(n_seq, HEADS // nh, qb),
        in_specs=[pl.BlockSpec((TB, nh * hw), lambda s, h, i: (qoff + s * qb + i, h)),
                  pl.BlockSpec((n_keys, nh * hw), lambda s, h, i: (s, h)),
                  pl.BlockSpec((n_keys, nh * V_DIM), lambda s, h, i: (s, h))],
        out_specs=pl.BlockSpec((TB, nh * V_DIM), lambda s, h, i: (s * qb + i, h)),
        out_shape=jax.ShapeDtypeStruct((n_seq * t_q, D), BF16),
        compiler_params=_cp("arbitrary", "arbitrary", "arbitrary"),
        name=name,
    )(q, kcat, v)


def _head_ones():
    r = lax.broadcasted_iota(jnp.int32, (LANES, LANES), 0) >= RH
    c = lax.broadcasted_iota(jnp.int32, (LANES, LANES), 1) >= RH
    return (r == c).astype(F32)


def _head_sum(x, ones_bd):
    parts = [jnp.dot(x[:, i * LANES:(i + 1) * LANES], ones_bd, precision=HI, preferred_element_type=F32)
             for i in range(x.shape[1] // LANES)]
    return jnp.concatenate(parts, axis=1)


def _rwkv_prep_kernel(p_ref, prev_ref, next_ref, mu_ref, kkw_ref, w0_ref, w2_ref, a0_ref, a2_ref, ka_ref,
                      rk_ref, g2_ref,
                      r_o, v_o, kk_o, lwf_o, lwb_o, kdf_o, kdb_o, bqf_o, bqb_o, bonus_o, g_o,
                      *, nb_ctx, bps_ctx, bps_lat):
    i = pl.program_id(0)
    is_lat = i >= nb_ctx
    pos = jnp.where(is_lat, lax.rem(i - nb_ctx, bps_lat), lax.rem(i, bps_ctx))
    last = jnp.where(is_lat, bps_lat - 1, bps_ctx - 1)
    has_prev = (pos != 0).astype(F32)
    has_next = (pos != last).astype(F32)
    rows = lax.broadcasted_iota(jnp.int32, (RB, 1), 0)

    def shifted(lo, hi):
        p = p_ref[:, lo:hi]
        prev = jnp.where(rows == 0, prev_ref[7:8, lo:hi] * has_prev, pltpu.roll(p, 1, axis=0))
        nxt = jnp.where(rows == RB - 1, next_ref[0:1, lo:hi] * has_next, pltpu.roll(p, RB - 1, axis=0))
        return p + (0.5 * (prev + nxt) - p) * mu_ref[:, lo:hi]

    ones_bd = _head_ones()
    r = shifted(0, D)
    k = shifted(D, 2 * D)
    v = shifted(2 * D, 3 * D)
    r_o[...] = r
    v_o[...] = v
    kkf = k * kkw_ref[...]
    nrm = jnp.sqrt(_head_sum(kkf * kkf, ones_bd))
    kk = kkf / jnp.maximum(nrm, 1e-12)
    kk_o[...] = kk
    bonus_o[...] = _head_sum(r * k * rk_ref[...], ones_bd) * v
    base = 3 * D
    outs = ((lwf_o, kdf_o, bqf_o), (lwb_o, kdb_o, bqb_o))
    for d in range(2):
        wl = shifted(base + d * LORA_PAD, base + (d + 1) * LORA_PAD)
        z = w0_ref[d:d + 1, :] + jnp.dot(jnp.tanh(wl).astype(BF16), w2_ref[d], preferred_element_type=F32)
        y = -z
        softplus = jnp.maximum(y, 0.0) + jnp.log(1.0 + jnp.exp(-jnp.abs(y)))
        logw = -softplus - 0.5
        outs[d][0][...] = -jnp.exp(logw)
        al = shifted(base + (2 + d) * LORA_PAD, base + (3 + d) * LORA_PAD)
        a = _sigmoid(a0_ref[d:d + 1, :] + jnp.dot(al.astype(BF16), a2_ref[d], preferred_element_type=F32))
        outs[d][1][...] = k * (1.0 + (a - 1.0) * ka_ref[d:d + 1, :])
        outs[d][2][...] = kk * a
    gl = shifted(base + 4 * LORA_PAD, base + 4 * LORA_PAD + GATE_LORA)
    g_o[...] = jnp.dot(_sigmoid(gl).astype(BF16), g2_ref[...], preferred_element_type=F32)


def rwkv_prep(p_rw, mu, kkw, w0, w2, a0, a2, ka, rk, g2, nb_ctx, bps_ctx, bps_lat):
    n = p_rw.shape[0]
    nb = n // RB
    sub = RB // 8
    row = lambda i: (i, 0)
    const2 = lambda i: (0, 0)
    out = jax.ShapeDtypeStruct((n, D), F32)
    return pl.pallas_call(
        functools.partial(_rwkv_prep_kernel, nb_ctx=nb_ctx, bps_ctx=bps_ctx, bps_lat=bps_lat),
        grid=(nb,),
        in_specs=[pl.BlockSpec((RB, RW_COLS), row),
                  pl.BlockSpec((8, RW_COLS), lambda i: (jnp.maximum(i * sub - 1, 0), 0)),
                  pl.BlockSpec((8, RW_COLS), lambda i: (jnp.minimum((i + 1) * sub, nb * sub - 1), 0)),
                  pl.BlockSpec((1, RW_COLS), const2),
                  pl.BlockSpec((1, D), const2),
                  pl.BlockSpec((2, D), const2),
                  pl.BlockSpec((2, LORA_PAD, D), lambda i: (0, 0, 0)),
                  pl.BlockSpec((2, D), const2),
                  pl.BlockSpec((2, LORA_PAD, D), lambda i: (0, 0, 0)),
                  pl.BlockSpec((2, D), const2),
                  pl.BlockSpec((1, D), const2),
                  pl.BlockSpec((GATE_LORA, D), const2)],
        out_specs=[pl.BlockSpec((RB, D), row)] * 11,
        out_shape=[out] * 11,
        compiler_params=_cp("arbitrary"),
        name="rwkv_prep",
    )(p_rw, p_rw, p_rw, mu, kkw, w0, w2, a0, a2, ka, rk, g2)


def _mm(eq, a, b):
    return jnp.einsum(eq, a.astype(BF16), b.astype(BF16), preferred_element_type=F32)


def _split_bf16(x):
    hi = x.astype(BF16)
    return hi, (x - hi.astype(F32)).astype(BF16)


def _mm3(eq, a, b):
    ah, al = _split_bf16(a)
    bh, bl = _split_bf16(b)
    mm = lambda x, y: jnp.einsum(eq, x, y, preferred_element_type=F32)
    return mm(ah, bh) + (mm(ah, bl) + mm(al, bh))


_NN = "hmk,hkn->hmn"
_NT = "hmk,hnk->hmn"
_TN = "hkm,hkn->hmn"


def _scan_sum(x, reverse):
    rows = lax.broadcasted_iota(jnp.int32, (SC, 1), 0)
    shift = 1
    while shift < SC:
        if reverse:
            x = x + jnp.where(rows < SC - shift, pltpu.roll(x, SC - shift, axis=0), 0.0)
        else:
            x = x + jnp.where(rows >= shift, pltpu.roll(x, shift, axis=0), 0.0)
        shift *= 2
    return x


def _wkv_chunk(refs, s, reverse):
    r, v, kk, lw, kd, bq = (x[...] for x in refs)
    last = 0 if reverse else SC - 1
    L = _scan_sum(lw, reverse)
    ltot = L[last:last + 1, :]
    en = jnp.exp(-L)
    et = jnp.exp(ltot - L)

    def heads(x):
        return jnp.stack([x[:, h * RH:(h + 1) * RH] for h in range(HG)], axis=0)

    ax = heads(-kk * jnp.exp(L - lw))
    rt = heads(r * jnp.exp(L))
    bt = heads(bq * en)
    kt = heads(kd * en)
    bc = heads(bq * et)
    kc = heads(kd * et)
    ptot = heads(jnp.exp(ltot))
    v = heads(v)

    ti = lax.broadcasted_iota(jnp.int32, (SC, SC), 0)
    si = lax.broadcasted_iota(jnp.int32, (SC, SC), 1)
    strict = (si > ti) if reverse else (si < ti)
    incl = (si >= ti) if reverse else (si <= ti)
    a_ab = jnp.where(strict, _mm3(_NT, ax, bt), 0.0)
    a_ak = jnp.where(strict, _mm3(_NT, ax, kt), 0.0)
    a_rb = jnp.where(incl, _mm(_NT, rt, bt), 0.0)
    a_rk = jnp.where(incl, _mm(_NT, rt, kt), 0.0)
    npow = a_ab
    e = a_ab
    for _ in range(int(math.log2(SC)) - 1):
        npow = _mm(_NN, npow, npow)
        e = e + npow + _mm(_NN, e, npow)
    ws = _mm(_NT, jnp.concatenate([ax, rt], axis=1), s)
    w = ws[:, :SC] + _mm(_NN, a_ak, v)
    u = w + _mm(_NN, e, w)
    rho = (w - u) + _mm3(_NN, a_ab, u)
    u = u + rho + _mm(_NN, e, rho)
    y = ws[:, SC:] + _mm(_NN, a_rb, u) + _mm(_NN, a_rk, v)
    s_new = s * ptot + _mm3(_TN, jnp.concatenate([u, v], axis=1), jnp.concatenate([bc, kc], axis=1))
    return jnp.concatenate([y[h] for h in range(HG)], axis=1), s_new


def _scan_kernel(rf, vf, kkf, lwf, kdf, bqf, rb, vb, kkb, lwb, kdb, bqb, s0f, s0b,
                 yf_ref, yb_ref, sfo, sbo, stf, stb, *, nc):
    c = pl.program_id(2)

    @pl.when(c == 0)
    def _():
        stf[...] = s0f[...]
        stb[...] = s0b[...]

    for reverse, refs, st, yref in ((False, (rf, vf, kkf, lwf, kdf, bqf), stf, yf_ref),
                                    (True, (rb, vb, kkb, lwb, kdb, bqb), stb, yb_ref)):
        y, s_new = _wkv_chunk(refs, st[...], reverse)
        st[...] = s_new
        yref[...] = y

    @pl.when(c == nc - 1)
    def _():
        sfo[...] = stf[...]
        sbo[...] = stb[...]


def rwkv_scan(r, v, kk, lwf, lwb, kdf, kdb, bqf, bqb, s0f, s0b, row0, n_seq, t_len, name):
    nc = t_len // SC
    off = row0 // SC
    cw = HG * RH
    fwd = lambda s, h, c: (off + s * nc + c, h)
    bwd = lambda s, h, c: (off + s * nc + (nc - 1 - c), h)
    blk = lambda im: pl.BlockSpec((SC, cw), im)
    st_spec = pl.BlockSpec((None, HG, RH, RH), lambda s, h, c: (s, h, 0, 0))
    n_rows = n_seq * t_len
    return pl.pallas_call(
        functools.partial(_scan_kernel, nc=nc),
        grid=(n_seq, RHEADS // HG, nc),
        in_specs=[blk(fwd)] * 6 + [blk(bwd)] * 6 + [st_spec, st_spec],
        out_specs=[pl.BlockSpec((SC, cw), lambda s, h, c: (s * nc + c, h)),
                   pl.BlockSpec((SC, cw), lambda s, h, c: (s * nc + (nc - 1 - c), h)),
                   st_spec, st_spec],
        out_shape=[jax.ShapeDtypeStruct((n_rows, D), F32), jax.ShapeDtypeStruct((n_rows, D), F32),
                   jax.ShapeDtypeStruct(s0f.shape, F32), jax.ShapeDtypeStruct(s0b.shape, F32)],
        scratch_shapes=[pltpu.VMEM((HG, RH, RH), F32), pltpu.VMEM((HG, RH, RH), F32)],
        compiler_params=_cp("arbitrary", "arbitrary", "arbitrary"),
        name=name,
    )(r, v, kk, lwf, kdf, bqf, r, v, kk, lwb, kdb, bqb, s0f, s0b)


def _rwkv_fin_kernel(yf_ref, yb_ref, bonus_ref, g_ref, lg_ref, lb_ref, o_ref):
    ones_bd = _head_ones()
    y = yf_ref[...] + yb_ref[...]
    mu = _head_sum(y, ones_bd) * (1.0 / RH)
    yc = y - mu
    var = _head_sum(yc * yc, ones_bd) * (1.0 / RH)
    yn = yc * lax.rsqrt(var + GN_EPS) * lg_ref[...] + lb_ref[...]
    o_ref[...] = ((yn + bonus_ref[...]) * g_ref[...]).astype(BF16)


def rwkv_finish(yf, yb, bonus, g, lnx_g, lnx_b):
    n = yf.shape[0]
    row = lambda i: (i, 0)
    const2 = lambda i: (0, 0)
    return pl.pallas_call(
        _rwkv_fin_kernel,
        grid=(n // TB,),
        in_specs=[pl.BlockSpec((TB, D), row)] * 4 + [pl.BlockSpec((1, D), const2)] * 2,
        out_specs=pl.BlockSpec((TB, D), row),
        out_shape=jax.ShapeDtypeStruct((n, D), BF16),
        compiler_params=_cp("arbitrary"),
        name="rwkv_finish",
    )(yf, yb, bonus, g, lnx_g, lnx_b)


def _merge_kernel(oa_ref, ob_ref, oc_ref, w_ref, ga_ref, gb_ref, gc_ref, z_ref):
    acc = None
    for n, (o_ref, g_ref) in enumerate(((oa_ref, ga_ref), (ob_ref, gb_ref), (oc_ref, gc_ref))):
        y = _sigmoid(g_ref[...]) * jnp.dot(o_ref[...], w_ref[n], preferred_element_type=F32)
        acc = y if acc is None else acc + y
    z_ref[...] = acc.astype(BF16)


def branch_merge(o_a, o_b, o_c, w_branch, p_gate):
    n = o_a.shape[0]
    tn = 512
    tm = 512 if n % 512 == 0 else TB
    nj = D // tn
    o_spec = pl.BlockSpec((tm, D), lambda j, i: (i, 0))
    gate = lambda b: pl.BlockSpec((tm, tn), lambda j, i: (i, b * nj + j))
    return pl.pallas_call(
        _merge_kernel,
        grid=(nj, n // tm),
        in_specs=[o_spec, o_spec, o_spec,
                  pl.BlockSpec((3, D, tn), lambda j, i: (0, 0, j)),
                  gate(0), gate(1), gate(2)],
        out_specs=pl.BlockSpec((tm, tn), lambda j, i: (i, j)),
        out_shape=jax.ShapeDtypeStruct((n, D), BF16),
        compiler_params=_cp("arbitrary", "arbitrary"),
        name="branch_merge",
    )(o_a, o_b, o_c, w_branch, p_gate, p_gate, p_gate)


def _oproj_kernel(z_ref, wo_ref, x_ref, mod_ref, g_ref, b_ref, x1_ref, h2_ref):
    mix = jnp.dot(z_ref[...], wo_ref[...], preferred_element_type=F32)
    m = mod_ref[...]
    x1 = _ln(ALPHA * x_ref[...] + m[2:3] * mix) * g_ref[...] + b_ref[...]
    x1_ref[...] = x1
    h2_ref[...] = _ln(x1) * (1.0 + m[4:5]) + m[3:4]


def out_proj(z, w_o, x, mod, modrow, ln_g, ln_b):
    n = z.shape[0]
    row = lambda i: (i, 0)
    const2 = lambda i: (0, 0)
    return pl.pallas_call(
        _oproj_kernel,
        grid=(n // TB,),
        in_specs=[pl.BlockSpec((TB, D), row),
                  pl.BlockSpec((D, D), const2),
                  pl.BlockSpec((TB, D), row),
                  pl.BlockSpec((None, 6, D), lambda i: (modrow(i), 0, 0)),
                  pl.BlockSpec((1, D), const2),
                  pl.BlockSpec((1, D), const2)],
        out_specs=[pl.BlockSpec((TB, D), row), pl.BlockSpec((TB, D), row)],
        out_shape=[jax.ShapeDtypeStruct((n, D), F32), jax.ShapeDtypeStruct((n, D), F32)],
        compiler_params=_cp("arbitrary"),
        name="out_proj_ln",
    )(z, w_o, x, mod, ln_g, ln_b)


def _router_kernel(h_ref, w_ref, b_ref, idx_ref, wts_ref, hist_ref):
    logits = jnp.dot(h_ref[...], w_ref[...], precision=HI, preferred_element_type=F32) + b_ref[...]
    lane = lax.broadcasted_iota(jnp.int32, (TB, LANES), 1).astype(F32)
    cur = logits
    idx_out = jnp.zeros((TB, LANES), F32)
    val_out = jnp.zeros((TB, LANES), F32)
    hist = jnp.zeros((TB, LANES), F32)
    top = None
    for k in range(TOP_K):
        m = jnp.max(cur, -1, keepdims=True)
        am = jnp.min(jnp.where(cur == m, lane, float(LANES)), -1, keepdims=True)
        sel = lane == am
        top = m if top is None else top
        idx_out = jnp.where(lane == k, am, idx_out)
        val_out = jnp.where(lane == k, jnp.exp(m - top), val_out)
        hist = hist + sel.astype(F32)
        cur = jnp.where(sel, -jnp.inf, cur)
    idx_ref[...] = idx_out.astype(jnp.int32)
    wts_ref[...] = val_out * (1.0 / jnp.sum(val_out, -1, keepdims=True))
    hist_ref[...] = jnp.sum(hist, 0, keepdims=True)


def router(h2, rw_pad, rb_pad):
    n = h2.shape[0]
    nb = n // TB
    return pl.pallas_call(
        _router_kernel,
        grid=(nb,),
        in_specs=[pl.BlockSpec((TB, D), lambda i: (i, 0)),
                  pl.BlockSpec((D, LANES), lambda i: (0, 0)),
                  pl.BlockSpec((1, LANES), lambda i: (0, 0))],
        out_specs=[pl.BlockSpec((TB, LANES), lambda i: (i, 0)),
                   pl.BlockSpec((TB, LANES), lambda i: (i, 0)),
                   pl.BlockSpec((None, 1, LANES), lambda i: (i, 0, 0))],
        out_shape=[jax.ShapeDtypeStruct((n, LANES), jnp.int32),
                   jax.ShapeDtypeStruct((n, LANES), F32),
                   jax.ShapeDtypeStruct((nb, 1, LANES), F32)],
        compiler_params=_cp("arbitrary"),
        name="moe_router",
    )(h2, rw_pad, rb_pad)


def _dest_kernel(idx_ref, base_ref, dest_ref):
    lane = lax.broadcasted_iota(jnp.int32, (TB, LANES), 1).astype(F32)
    ri = lax.broadcasted_iota(jnp.int32, (TB, TB), 0)
    ci = lax.broadcasted_iota(jnp.int32, (TB, TB), 1)
    lower = (ci < ri).astype(BF16)
    idx = idx_ref[...].astype(F32)
    run = base_ref[...]
    dest = jnp.zeros((TB, LANES), F32)
    for k in range(TOP_K):
        e_k = jnp.sum(jnp.where(lane == k, idx, 0), -1, keepdims=True)
        onehot = lane == e_k
        oh = onehot.astype(F32)
        before = jnp.dot(lower, oh.astype(BF16), preferred_element_type=F32)
        d_k = jnp.sum(jnp.where(onehot, run + before, 0.0), -1, keepdims=True)
        dest = jnp.where(lane == k, d_k, dest)
        run = run + jnp.sum(oh, 0, keepdims=True)
    dest_ref[...] = dest.astype(jnp.int32)


def moe_dest(idx, base):
    n = idx.shape[0]
    nb = n // TB
    return pl.pallas_call(
        _dest_kernel,
        grid=(nb,),
        in_specs=[pl.BlockSpec((TB, LANES), lambda i: (i, 0)),
                  pl.BlockSpec((None, 1, LANES), lambda i: (i, 0, 0))],
        out_specs=pl.BlockSpec((TB, LANES), lambda i: (i, 0)),
        out_shape=jax.ShapeDtypeStruct((n, LANES), jnp.int32),
        compiler_params=_cp("arbitrary"),
        name="moe_dest",
    )(idx, base)


def _row_copy(src, src_row, dst, dst_row, sem):
    return pltpu.make_async_copy(src.at[pl.ds(src_row, 1)], dst.at[pl.ds(dst_row, 1)], sem)


def _dispatch_kernel(dest_ref, h_ref, xb_in, xb_out, sem):
    del xb_in

    def issue(t, carry):
        for k in range(TOP_K):
            _row_copy(h_ref, t, xb_out, dest_ref[0, t * TOP_K + k], sem).start()
        return carry

    lax.fori_loop(0, TB, issue, 0)

    def drain(t, carry):
        for k in range(TOP_K):
            _row_copy(h_ref, t, xb_out, dest_ref[0, t * TOP_K + k], sem).wait()
        return carry

    lax.fori_loop(0, TB, drain, 0)


def moe_dispatch(dest_flat, h2, xb_init):
    n = h2.shape[0]
    nb = n // TB
    return pl.pallas_call(
        _dispatch_kernel,
        grid=(nb,),
        in_specs=[pl.BlockSpec((None, 1, TB * TOP_K), lambda i: (i, 0, 0), memory_space=pltpu.SMEM),
                  pl.BlockSpec((TB, D), lambda i: (i, 0)),
                  pl.BlockSpec(memory_space=pl.ANY)],
        out_specs=pl.BlockSpec(memory_space=pl.ANY),
        out_shape=jax.ShapeDtypeStruct(xb_init.shape, xb_init.dtype),
        scratch_shapes=[pltpu.SemaphoreType.DMA(())],
        input_output_aliases={2: 0},
        compiler_params=_cp("arbitrary"),
        name="moe_dispatch",
    )(dest_flat, h2, xb_init)


def _expert_changed(be_ref):
    i = pl.program_id(1)
    return jnp.logical_or(i == 0, be_ref[i] != be_ref[jnp.maximum(i - 1, 0)])


def _expert_up_kernel(be_ref, nv_ref, x_ref, wg_ref, bg_ref, wu_ref, bu_ref, act_ref, wg_bf, wu_bf):
    used = pl.program_id(1) < nv_ref[0]

    @pl.when(jnp.logical_not(used))
    def _():
        act_ref[...] = jnp.zeros_like(act_ref)

    @pl.when(jnp.logical_and(used, _expert_changed(be_ref)))
    def _():
        wg_bf[...] = wg_ref[...].astype(BF16)
        wu_bf[...] = wu_ref[...].astype(BF16)

    @pl.when(used)
    def _():
        xb = x_ref[...].astype(BF16)
        gl = jnp.minimum(jnp.dot(xb, wg_bf[...], preferred_element_type=F32) + bg_ref[...], SWIGLU_LIMIT)
        lin = jnp.clip(jnp.dot(xb, wu_bf[...], preferred_element_type=F32) + bu_ref[...],
                       -SWIGLU_LIMIT, SWIGLU_LIMIT)
        act_ref[...] = (gl * _sigmoid(SWIGLU_ALPHA * gl) * (lin + 1.0)).astype(BF16)


def expert_up(block_e, n_valid, xb, w_gate, b_gate, w_up, b_up, layer):
    cap = xb.shape[0]
    tn = EXPERT_TN
    rowblk = lambda j, i, be, nv: (jnp.minimum(i, nv[0] - 1), 0)
    wspec = pl.BlockSpec((None, None, D, tn), lambda j, i, be, nv: (layer, be[i], 0, j))
    bspec = pl.BlockSpec((None, None, 1, tn), lambda j, i, be, nv: (layer, be[i], 0, j))
    return pl.pallas_call(
        _expert_up_kernel,
        grid_spec=pltpu.PrefetchScalarGridSpec(
            num_scalar_prefetch=2,
            grid=(D // tn, cap // BM),
            in_specs=[pl.BlockSpec((BM, D), rowblk), wspec, bspec, wspec, bspec],
            out_specs=pl.BlockSpec((BM, tn), lambda j, i, be, nv: (i, j)),
            scratch_shapes=[pltpu.VMEM((D, tn), BF16), pltpu.VMEM((D, tn), BF16)]),
        out_shape=jax.ShapeDtypeStruct((cap, D), BF16),
        compiler_params=pltpu.CompilerParams(dimension_semantics=("arbitrary", "arbitrary"),
                                             vmem_limit_bytes=EXPERT_VMEM_LIMIT),
        name="moe_expert_up",
    )(block_e, n_valid, xb, w_gate, b_gate, w_up, b_up)


def _expert_down_kernel(be_ref, nv_ref, a_ref, wd_ref, bd_ref, y_ref, wd_bf):
    used = pl.program_id(1) < nv_ref[0]

    @pl.when(jnp.logical_not(used))
    def _():
        y_ref[...] = jnp.zeros_like(y_ref)

    @pl.when(jnp.logical_and(used, _expert_changed(be_ref)))
    def _():
        wd_bf[...] = wd_ref[...].astype(BF16)

    @pl.when(used)
    def _():
        y_ref[...] = jnp.dot(a_ref[...], wd_bf[...], preferred_element_type=F32) + bd_ref[...]


def expert_down(block_e, n_valid, act, w_down, b_down, layer):
    cap = act.shape[0]
    tn = EXPERT_TN
    return pl.pallas_call(
        _expert_down_kernel,
        grid_spec=pltpu.PrefetchScalarGridSpec(
            num_scalar_prefetch=2,
            grid=(D // tn, cap // BM),
            in_specs=[pl.BlockSpec((BM, D), lambda j, i, be, nv: (jnp.minimum(i, nv[0] - 1), 0)),
                      pl.BlockSpec((None, None, D, tn), lambda j, i, be, nv: (layer, be[i], 0, j)),
                      pl.BlockSpec((None, None, 1, tn), lambda j, i, be, nv: (layer, be[i], 0, j))],
            out_specs=pl.BlockSpec((BM, tn), lambda j, i, be, nv: (i, j)),
            scratch_shapes=[pltpu.VMEM((D, tn), BF16)]),
        out_shape=jax.ShapeDtypeStruct((cap, D), F32),
        compiler_params=pltpu.CompilerParams(dimension_semantics=("arbitrary", "arbitrary"),
                                             vmem_limit_bytes=EXPERT_VMEM_LIMIT),
        name="moe_expert_down",
    )(block_e, n_valid, act, w_down, b_down)


def _combine_kernel(dest_ref, wts_ref, x1_ref, mod_ref, g_ref, b_ref, modn_ref, yb_ref,
                    x2_ref, hn_ref, rows, sem):
    def issue(t, carry):
        for k in range(TOP_K):
            _row_copy(yb_ref, dest_ref[0, t * TOP_K + k], rows.at[k], t, sem).start()
        return carry

    lax.fori_loop(0, TB, issue, 0)

    def drain(t, carry):
        for k in range(TOP_K):
            _row_copy(yb_ref, dest_ref[0, t * TOP_K + k], rows.at[k], t, sem).wait()
        return carry

    lax.fori_loop(0, TB, drain, 0)
    w = wts_ref[...]
    ffn = rows[0] * w[:, 0:1]
    for k in range(1, TOP_K):
        ffn = ffn + rows[k] * w[:, k:k + 1]
    m = mod_ref[...]
    x2 = _ln(ALPHA * x1_ref[...] + m[5:6] * ffn) * g_ref[...] + b_ref[...]
    x2_ref[...] = x2
    mn = modn_ref[...]
    hn_ref[...] = (_ln(x2) * (1.0 + mn[1:2]) + mn[0:1]).astype(BF16)


def moe_combine(dest_flat, wts, x1, mod, mod_next, modrow, ln_g, ln_b, yb):
    n = x1.shape[0]
    row = lambda i: (i, 0)
    const2 = lambda i: (0, 0)
    modspec = pl.BlockSpec((None, 6, D), lambda i: (modrow(i), 0, 0))
    return pl.pallas_call(
        _combine_kernel,
        grid=(n // TB,),
        in_specs=[pl.BlockSpec((None, 1, TB * TOP_K), lambda i: (i, 0, 0), memory_space=pltpu.SMEM),
                  pl.BlockSpec((TB, LANES), row),
                  pl.BlockSpec((TB, D), row),
                  modspec,
                  pl.BlockSpec((1, D), const2),
                  pl.BlockSpec((1, D), const2),
                  modspec,
                  pl.BlockSpec(memory_space=pl.ANY)],
        out_specs=[pl.BlockSpec((TB, D), row), pl.BlockSpec((TB, D), row)],
        out_shape=[jax.ShapeDtypeStruct((n, D), F32), jax.ShapeDtypeStruct((n, D), BF16)],
        scratch_shapes=[pltpu.VMEM((TOP_K, TB, D), F32), pltpu.SemaphoreType.DMA(())],
        compiler_params=_cp("arbitrary"),
        name="moe_combine",
    )(dest_flat, wts, x1, mod, ln_g, ln_b, mod_next, yb)


def moe_layout(hist):
    h = hist[:, 0, :].astype(jnp.int32)
    counts = jnp.sum(h, 0)
    padded = (counts + BM - 1) // BM * BM
    pad_end = jnp.cumsum(padded)
    pad_start = pad_end - padded
    before = jnp.cumsum(h, 0) - h
    base = (pad_start[None, :] + before).astype(F32)[:, None, :]
    return base, pad_end


def _rot_cols(w):
    q = ROPE // 4
    return jnp.concatenate([-w[..., q:2 * q], w[..., 0:q], -w[..., 3 * q:4 * q], w[..., 2 * q:3 * q]], axis=-1)


def _rope_tables(n):
    rows = n // GRID_W
    row_id = jnp.repeat(jnp.arange(rows, dtype=F32), GRID_W)
    col_id = jnp.tile(jnp.arange(GRID_W, dtype=F32), rows)
    half = ROPE // 2
    inv_freq = ROPE_THETA ** (-jnp.arange(0, half, 2, dtype=F32) / half)
    ang_r = row_id[:, None] * inv_freq
    ang_c = col_id[:, None] * inv_freq
    cos = jnp.concatenate([jnp.cos(ang_r)] * 2 + [jnp.cos(ang_c)] * 2, axis=-1)
    sin = jnp.concatenate([jnp.sin(ang_r)] * 2 + [jnp.sin(ang_c)] * 2, axis=-1)
    return cos, sin


def _pad_rows(w, rows):
    return jnp.pad(w, [(0, 0)] * (w.ndim - 2) + [(0, rows - w.shape[-2]), (0, 0)])


def kernel(x_prompt, x_sample, cache_ckv, cache_krope, state_rwkv_fwd, state_rwkv_bwd, c, c_ctx, w_mod, b_mod, w_in, gmlp_ln_g, gmlp_ln_b, gmlp_ws, gmlp_bs, q_norm_g, w_uq, kv_norm_g, w_ukv, rwkv_mu, rwkv_w0, rwkv_w2, rwkv_a0, rwkv_a2, rwkv_g2, rwkv_kk, rwkv_ka, rwkv_rk, rwkv_lnx_g, rwkv_lnx_b, w_branch, w_o, ln1_g, ln1_b, router_w, router_b, w_gate, b_gate, w_up, b_up, w_down, b_down, ln2_g, ln2_b):
    L = w_mod.shape[0]
    b_ctx, seq, _ = x_prompt.shape
    b_lat, t_lat, _ = x_sample.shape
    past = cache_ckv.shape[2]
    n_ctx = b_ctx * seq
    n_lat = b_lat * t_lat
    n = n_ctx + n_lat
    assert seq % TB == 0 and t_lat % TB == 0 and past % TB == 0 and b_lat + 1 <= 8
    nb_ctx = n_ctx // TB
    bps_lat = t_lat // TB

    def modrow(i):
        return jnp.where(i < nb_ctx, 0, 1 + (i - nb_ctx) // bps_lat)

    cond8 = jnp.zeros((8, D), F32).at[0].set(c_ctx).at[1:1 + b_lat].set(c)
    mod_all = modulation(cond8, w_mod, b_mod).reshape(L, 8, 6, D)

    gm_end = 2 * D
    mla_end = gm_end + Q_LORA + KV_LORA + ROPE
    rw_end = mla_end + 3 * D + 2 * DECAY_LORA + 2 * AAA_LORA + GATE_LORA
    w_gm = w_in[:, :, :gm_end].astype(BF16)
    w_kr = w_in[:, :, mla_end - ROPE:mla_end]
    w_mla = jnp.concatenate([w_in[:, :, gm_end:mla_end], _rot_cols(w_kr)], axis=-1).astype(BF16)
    rw = w_in[:, :, mla_end:rw_end]
    segs = [rw[:, :, :3 * D]]
    mu_segs = [rwkv_mu[:, :3 * D]]
    o = 3 * D
    for width in (DECAY_LORA, DECAY_LORA, AAA_LORA, AAA_LORA):
        segs.append(jnp.pad(rw[:, :, o:o + width], ((0, 0), (0, 0), (0, LORA_PAD - width))))
        mu_segs.append(jnp.pad(rwkv_mu[:, o:o + width], ((0, 0), (0, LORA_PAD - width))))
        o += width
    segs.append(rw[:, :, o:])
    mu_segs.append(rwkv_mu[:, o:])
    w_rw = jnp.concatenate(segs, axis=-1).astype(BF16)
    mu_rw = jnp.concatenate(mu_segs, axis=-1)[:, None, :]
    w_gt = w_in[:, :, rw_end:].astype(BF16)
    w2p = _pad_rows(rwkv_w2, LORA_PAD).astype(BF16)
    a2p = _pad_rows(rwkv_a2, LORA_PAD).astype(BF16)
    g2b = rwkv_g2.astype(BF16)
    wq4 = w_uq.reshape(L, Q_LORA, HEADS, QK_DIM)
    wq_r = wq4[..., NOPE:]
    wq = jnp.concatenate([wq4[..., :NOPE], wq_r, _rot_cols(wq_r)], axis=-1)
    wq = wq.reshape(L, Q_LORA, HEADS * 2 * LANES).astype(BF16)
    wkv4 = w_ukv.reshape(L, KV_LORA, HEADS, NOPE + V_DIM)
    wk = wkv4[..., :NOPE].reshape(L, KV_LORA, D).astype(BF16)
    wv = wkv4[..., NOPE:].reshape(L, KV_LORA, D).astype(BF16)
    ws_b = gmlp_ws.astype(BF16)
    bs_full = jnp.repeat(jnp.swapaxes(gmlp_bs, 1, 2), GMLP_GW, axis=2)
    w_br = w_branch.astype(BF16)
    w_ob = w_o.astype(BF16)
    rw_pad = jnp.pad(router_w, ((0, 0), (0, 0), (0, LANES - N_EXPERTS)))
    rb_pad = jnp.pad(router_b, ((0, 0), (0, LANES - N_EXPERTS)), constant_values=-1e30)[:, None, :]
    bg4 = b_gate[:, :, None, :]
    bu4 = b_up[:, :, None, :]
    bd4 = b_down[:, :, None, :]

    cos, sin = _rope_tables(t_lat)
    cos_all = jnp.concatenate([jnp.ones((n_ctx, ROPE), F32), jnp.tile(cos, (b_lat, 1))], axis=0)
    sin_all = jnp.concatenate([jnp.zeros((n_ctx, ROPE), F32), jnp.tile(sin, (b_lat, 1))], axis=0)
    tabk = jnp.concatenate([cos_all, sin_all], axis=-1)
    tabq = jnp.concatenate([jnp.ones((n, NOPE), F32), tabk], axis=-1)
    tab_cache = jnp.concatenate([jnp.ones((b_lat * past, ROPE), F32), jnp.zeros((b_lat * past, ROPE), F32)], -1)

    zeros_state = jnp.zeros((b_ctx, RHEADS, RH, RH), F32)
    sf_lat = state_rwkv_fwd
    sb_lat = state_rwkv_bwd

    cap = (n * TOP_K // BM + N_EXPERTS) * BM
    x = jnp.concatenate([x_prompt.reshape(n_ctx, D), x_sample.reshape(n_lat, D)], axis=0)
    h = ln_modulate(x, mod_all[0], modrow)
    ckv_out, kr_out, sf_out, sb_out = [], [], [], []
    for l in range(L):
        mod = mod_all[l]
        p_gm = matmul(h, w_gm[l], 1024, "in_proj_gmlp")
        p_mla = matmul(h, w_mla[l], MLA_COLS, "in_proj_mla")
        p_rw = matmul(h, w_rw[l], 768, "in_proj_rwkv")
        p_gate = matmul(h, w_gt[l], 1024, "in_proj_gate")

        o_a = gmlp(p_gm, gmlp_ln_g[l][None], gmlp_ln_b[l][None], ws_b[l], bs_full[l])

        q = q_proj(p_mla, q_norm_g[l][None], wq[l], tabq)
        kcat, vv, ckvn = kv_proj(p_mla, 1, p_mla, (Q_LORA + KV_LORA) // LANES, kv_norm_g[l][None],
                                 wk[l], wv[l], tabk, True)
        cache_c = cache_ckv[:, l].reshape(b_lat * past, KV_LORA)
        cache_r = jnp.pad(cache_krope[:, l].reshape(b_lat * past, ROPE), ((0, 0), (0, ROPE)))
        kcat_c, vv_c, _ = kv_proj(cache_c, 0, cache_r, 0, kv_norm_g[l][None], wk[l], wv[l], tab_cache, False)
        hw2 = HEADS * 2 * LANES
        kcat_lat = jnp.concatenate([kcat_c.reshape(b_lat, past, hw2), kcat[n_ctx:].reshape(b_lat, t_lat, hw2)],
                                   axis=1).reshape(b_lat * (past + t_lat), hw2)
        vv_lat = jnp.concatenate([vv_c.reshape(b_lat, past, D), vv[n_ctx:].reshape(b_lat, t_lat, D)],
                                 axis=1).reshape(b_lat * (past + t_lat), D)
        o_b_ctx = attention(q, 0, kcat, vv, b_ctx, seq, seq, HEADS, "mla_attention_ctx")
        o_b_lat = attention(q, n_ctx, kcat_lat, vv_lat, b_lat, t_lat, past + t_lat, 1, "mla_attention_lat")
        o_b = jnp.concatenate([o_b_ctx, o_b_lat], axis=0)
        ckv_out.append(ckvn[:n_ctx].reshape(b_ctx, seq, KV_LORA))
        kr_out.append(p_mla[:n_ctx, Q_LORA + KV_LORA:Q_LORA + KV_LORA + ROPE].reshape(b_ctx, seq, ROPE))

        (r, v, kk, lwf, lwb, kdf, kdb, bqf, bqb, bonus, gg) = rwkv_prep(
            p_rw, mu_rw[l], rwkv_kk[l][None], rwkv_w0[l], w2p[l], rwkv_a0[l], a2p[l], rwkv_ka[l],
            rwkv_rk[l].reshape(1, D), g2b[l], n_ctx // RB, seq // RB, t_lat // RB)
        yf_c, yb_c, sf_c, sb_c = rwkv_scan(r, v, kk, lwf, lwb, kdf, kdb, bqf, bqb, zeros_state, zeros_state,
                                           0, b_ctx, seq, "rwkv_scan_ctx")
        yf_l, yb_l, _, _ = rwkv_scan(r, v, kk, lwf, lwb, kdf, kdb, bqf, bqb, sf_lat[:, l], sb_lat[:, l],
                                     n_ctx, b_lat, t_lat, "rwkv_scan_lat")
        sf_out.append(sf_c)
        sb_out.append(sb_c)
        o_c = rwkv_finish(jnp.concatenate([yf_c, yf_l], 0), jnp.concatenate([yb_c, yb_l], 0), bonus, gg,
                          rwkv_lnx_g[l][None], rwkv_lnx_b[l][None])

        z = branch_merge(o_a, o_b, o_c, w_br[l], p_gate)
        x1, h2 = out_proj(z, w_ob[l], x, mod, modrow, ln1_g[l][None], ln1_b[l][None])

        idx, wts, hist = router(h2, rw_pad[l], rb_pad[l])
        base, pad_end = moe_layout(hist)
        dest = moe_dest(idx, base)
        dest_flat = dest[:, :TOP_K].reshape(n // TB, 1, TB * TOP_K)
        xb = moe_dispatch(dest_flat, h2, jnp.zeros((cap, D), F32))
        n_valid = (pad_end[N_EXPERTS - 1] // BM).astype(jnp.int32)
        blk_start = jnp.minimum(jnp.arange(cap // BM, dtype=jnp.int32), n_valid - 1) * BM
        block_e = jnp.sum((pad_end[None, :N_EXPERTS] <= blk_start[:, None]).astype(jnp.int32), axis=1)
        block_e = jnp.minimum(block_e, N_EXPERTS - 1)
        nv = n_valid.reshape(1)
        act = expert_up(block_e, nv, xb, w_gate, bg4, w_up, bu4, l)
        yb = expert_down(block_e, nv, act, w_down, bd4, l)
        mod_next = mod_all[min(l + 1, L - 1)]
        x, h = moe_combine(dest_flat, wts, x1, mod, mod_next, modrow, ln2_g[l][None], ln2_b[l][None], yb)

    y_prompt = x[:n_ctx].reshape(b_ctx, seq, D)
    y_sample = x[n_ctx:].reshape(b_lat, t_lat, D)
    return (y_prompt, y_sample, jnp.stack(ckv_out, axis=1), jnp.stack(kr_out, axis=1),
            jnp.stack(sf_out, axis=1), jnp.stack(sb_out, axis=1))
```

```python
import functools
import math

import jax
import jax.numpy as jnp
from jax import lax
from jax.experimental import pallas as pl
from jax.experimental.pallas import tpu as pltpu

F32 = jnp.float32
BF16 = jnp.bfloat16
HI = lax.Precision.HIGHEST

DEPTH_NORM = 4
GRID_W = 64
CHUNK = 128
D = 2048
GMLP_GW = 128
GMLP_GROUPS = D // GMLP_GW
V_DIM = 128
NOPE = 128
ROPE = 64
QK_DIM = NOPE + ROPE
HEADS = D // V_DIM
Q_LORA = 512
KV_LORA = 512
ROPE_THETA = 10000.0
RH = 64
RHEADS = D // RH
DECAY_LORA = 96
AAA_LORA = 96
GATE_LORA = 256
GN_EPS = 64e-5
N_EXPERTS = 32
TOP_K = 4
SWIGLU_LIMIT = 7.0
SWIGLU_ALPHA = 1.702
LN_EPS = 1e-5
RMS_EPS = 1e-6
ALPHA = (2 * DEPTH_NORM) ** 0.25

LANES = 128
VMEM_LIMIT = 48 * 1024 * 1024

TB = 256
RB = 128
SC = 64
HG = 16
BM = 512
EXPERT_TN = 1024
EXPERT_UP_TN = 512
EXPERT_VMEM_LIMIT = 56 * 1024 * 1024
LORA_PAD = 128
RW_COLS = 3 * D + 4 * LORA_PAD + GATE_LORA
MLA_COLS = Q_LORA + KV_LORA + 2 * ROPE


def _cp(*sem):
    return pltpu.CompilerParams(dimension_semantics=sem, vmem_limit_bytes=VMEM_LIMIT)


def _ln(x):
    mu = jnp.mean(x, -1, keepdims=True)
    xc = x - mu
    var = jnp.mean(xc * xc, -1, keepdims=True)
    return xc * lax.rsqrt(var + LN_EPS)


def _sigmoid(x):
    return 1.0 / (1.0 + jnp.exp(-x))


def _gelu_tanh(x):
    return 0.5 * x * (1.0 + jnp.tanh(math.sqrt(2.0 / math.pi) * (x + 0.044715 * (x * x * x))))


def _mod_kernel(c_ref, w_ref, b_ref, o_ref):
    c = c_ref[...]
    s = (c * _sigmoid(c)).astype(BF16)
    o_ref[...] = jnp.dot(s, w_ref[...].astype(BF16), preferred_element_type=F32) + b_ref[...]


def modulation(cond8, w_mod, b_mod):
    L = w_mod.shape[0]
    tn = 1024
    return pl.pallas_call(
        _mod_kernel,
        grid=(L, 6 * D // tn),
        in_specs=[pl.BlockSpec((8, D), lambda l, j: (0, 0)),
                  pl.BlockSpec((None, D, tn), lambda l, j: (l, 0, j)),
                  pl.BlockSpec((None, 1, tn), lambda l, j: (l, 0, j))],
        out_specs=pl.BlockSpec((None, 8, tn), lambda l, j: (l, 0, j)),
        out_shape=jax.ShapeDtypeStruct((L, 8, 6 * D), F32),
        compiler_params=_cp("arbitrary", "arbitrary"),
        name="modulation",
    )(cond8, w_mod, b_mod.reshape(L, 1, 6 * D))


def _lnmod_kernel(x_ref, mod_ref, h_ref):
    m = mod_ref[...]
    h_ref[...] = (_ln(x_ref[...]) * (1.0 + m[1:2]) + m[0:1]).astype(BF16)


def ln_modulate(x, mod, modrow):
    n = x.shape[0]
    return pl.pallas_call(
        _lnmod_kernel,
        grid=(n // TB,),
        in_specs=[pl.BlockSpec((TB, D), lambda i: (i, 0)),
                  pl.BlockSpec((None, 6, D), lambda i: (modrow(i), 0, 0))],
        out_specs=pl.BlockSpec((TB, D), lambda i: (i, 0)),
        out_shape=jax.ShapeDtypeStruct((n, D), BF16),
        compiler_params=_cp("arbitrary"),
        name="ln_modulate",
    )(x, mod)


def _mm_kernel(x_ref, w_ref, o_ref):
    o_ref[...] = jnp.dot(x_ref[...], w_ref[...], preferred_element_type=F32).astype(o_ref.dtype)


def matmul(x, w, layer, tn, name, out_dtype=F32):
    m, k = x.shape
    nn = w.shape[2]
    tm = 512 if m % 512 == 0 else TB
    return pl.pallas_call(
        _mm_kernel,
        grid=(nn // tn, m // tm),
        in_specs=[pl.BlockSpec((tm, k), lambda j, i: (i, 0)),
                  pl.BlockSpec((None, k, tn), lambda j, i: (layer, 0, j))],
        out_specs=pl.BlockSpec((tm, tn), lambda j, i: (i, j)),
        out_shape=jax.ShapeDtypeStruct((m, nn), out_dtype),
        compiler_params=_cp("arbitrary", "arbitrary"),
        name=name,
    )(x, w)


def _gmlp_kernel(p_ref, g_ref, b_ref, ws_ref, bs_ref, o_ref):
    u = _gelu_tanh(p_ref[:, :D])
    v = _gelu_tanh(p_ref[:, D:])
    v = (_ln(v) * g_ref[...] + b_ref[...]).astype(BF16)
    for g in range(GMLP_GROUPS):
        sl = slice(g * GMLP_GW, (g + 1) * GMLP_GW)
        mixed = jnp.dot(ws_ref[g], v[:, sl], preferred_element_type=F32) + bs_ref[:, sl]
        o_ref[:, sl] = (u[:, sl] * mixed).astype(BF16)


def gmlp(p_gm, ln_g, ln_b, ws, bs_full):
    n = p_gm.shape[0]
    return pl.pallas_call(
        _gmlp_kernel,
        grid=(n // CHUNK,),
        in_specs=[pl.BlockSpec((CHUNK, 2 * D), lambda i: (i, 0)),
                  pl.BlockSpec((1, D), lambda i: (0, 0)),
                  pl.BlockSpec((1, D), lambda i: (0, 0)),
                  pl.BlockSpec((GMLP_GROUPS, CHUNK, CHUNK), lambda i: (0, 0, 0)),
                  pl.BlockSpec((CHUNK, D), lambda i: (0, 0))],
        out_specs=pl.BlockSpec((CHUNK, D), lambda i: (i, 0)),
        out_shape=jax.ShapeDtypeStruct((n, D), BF16),
        compiler_params=_cp("arbitrary"),
        name="gmlp",
    )(p_gm, ln_g, ln_b, ws, bs_full)


def _qproj_kernel(cq_ref, g_ref, w_ref, tab_ref, q_ref):
    cq = cq_ref[...]
    n = (cq * lax.rsqrt(jnp.mean(cq * cq, -1, keepdims=True) + RMS_EPS) * g_ref[...]).astype(BF16)
    tab = tab_ref[...]
    hw = 2 * LANES
    for h in range(HEADS):
        sl = slice(h * hw, (h + 1) * hw)
        q_ref[:, sl] = (jnp.dot(n, w_ref[:, sl], preferred_element_type=F32) * tab).astype(BF16)


def q_proj(p_mla, g, wq, tabq):
    n = p_mla.shape[0]
    hw = 2 * LANES
    return pl.pallas_call(
        _qproj_kernel,
        grid=(n // TB,),
        in_specs=[pl.BlockSpec((TB, Q_LORA), lambda i: (i, 0)),
                  pl.BlockSpec((1, Q_LORA), lambda i: (0, 0)),
                  pl.BlockSpec((Q_LORA, HEADS * hw), lambda i: (0, 0)),
                  pl.BlockSpec((TB, hw), lambda i: (i, 0))],
        out_specs=pl.BlockSpec((TB, HEADS * hw), lambda i: (i, 0)),
        out_shape=jax.ShapeDtypeStruct((n, HEADS * hw), BF16),
        compiler_params=_cp("arbitrary"),
        name="mla_q_proj",
    )(p_mla, g, wq, tabq)


def _kvproj_kernel(ckv_ref, kr_ref, g_ref, wk_ref, wv_ref, tab_ref, kcat_ref, v_ref, ckvn_ref, *, normalize):
    ckv = ckv_ref[...]
    if normalize:
        ckv = ckv * lax.rsqrt(jnp.mean(ckv * ckv, -1, keepdims=True) + RMS_EPS) * g_ref[...]
    ckvn_ref[...] = ckv
    nb = ckv.astype(BF16)
    t = kr_ref[...] * tab_ref[...]
    khi = (t + pltpu.roll(t, ROPE, axis=1)).astype(BF16)
    kn = jnp.dot(nb, wk_ref[...], preferred_element_type=F32).astype(BF16)
    v_ref[...] = jnp.dot(nb, wv_ref[...], preferred_element_type=F32).astype(BF16)
    hw = 2 * LANES
    for h in range(HEADS):
        kcat_ref[:, h * hw:h * hw + LANES] = kn[:, h * LANES:(h + 1) * LANES]
        kcat_ref[:, h * hw + LANES:(h + 1) * hw] = khi


def kv_proj(ckv_src, ckv_col, kr_src, kr_col, g, wk, wv, tabk, normalize):
    n = ckv_src.shape[0]
    hw = 2 * LANES
    return pl.pallas_call(
        functools.partial(_kvproj_kernel, normalize=normalize),
        grid=(n // TB,),
        in_specs=[pl.BlockSpec((TB, KV_LORA), lambda i: (i, ckv_col)),
                  pl.BlockSpec((TB, LANES), lambda i: (i, kr_col)),
                  pl.BlockSpec((1, KV_LORA), lambda i: (0, 0)),
                  pl.BlockSpec((KV_LORA, D), lambda i: (0, 0)),
                  pl.BlockSpec((KV_LORA, D), lambda i: (0, 0)),
                  pl.BlockSpec((TB, LANES), lambda i: (i, 0))],
        out_specs=[pl.BlockSpec((TB, HEADS * hw), lambda i: (i, 0)),
                   pl.BlockSpec((TB, D), lambda i: (i, 0)),
                   pl.BlockSpec((TB, KV_LORA), lambda i: (i, 0))],
        out_shape=[jax.ShapeDtypeStruct((n, HEADS * hw), BF16),
                   jax.ShapeDtypeStruct((n, D), BF16),
                   jax.ShapeDtypeStruct((n, KV_LORA), F32)],
        compiler_params=_cp("arbitrary"),
        name="mla_kv_proj",
    )(ckv_src, kr_src, g, wk, wv, tabk)


def _attn_kernel(q_ref, k_ref, v_ref, o_ref, *, nh):
    hw = 2 * LANES
    for h in range(nh):
        s = lax.dot_general(q_ref[:, h * hw:(h + 1) * hw], k_ref[:, h * hw:(h + 1) * hw],
                            (((1,), (1,)), ((), ())), preferred_element_type=F32)
        s = s * (QK_DIM ** -0.5)
        e = jnp.exp(s - jnp.max(s, -1, keepdims=True))
        p = (e * (1.0 / jnp.sum(e, -1, keepdims=True))).astype(BF16)
        o_ref[:, h * V_DIM:(h + 1) * V_DIM] = jnp.dot(
            p, v_ref[:, h * V_DIM:(h + 1) * V_DIM], preferred_element_type=F32).astype(BF16)


def attention(q, q_row0, kcat, v, n_seq, t_q, n_keys, nh, name):
    hw = 2 * LANES
    qb = t_q // TB
    qoff = q_row0 // TB
    return pl.pallas_call(
        functools.partial(_attn_kernel, nh=nh),
        grid=(n_seq, HEADS // nh, qb),
        in_specs=[pl.BlockSpec((TB, nh * hw), lambda s, h, i: (qoff + s * qb + i, h)),
                  pl.BlockSpec((n_keys, nh * hw), lambda s, h, i: (s, h)),
                  pl.BlockSpec((n_keys, nh * V_DIM), lambda s, h, i: (s, h))],
        out_specs=pl.BlockSpec((TB, nh * V_DIM), lambda s, h, i: (s * qb + i, h)),
        out_shape=jax.ShapeDtypeStruct((n_seq * t_q, D), BF16),
        compiler_params=_cp("arbitrary", "arbitrary", "arbitrary"),
        name=name,
    )(q, kcat, v)


def _head_ones():
    r = lax.broadcasted_iota(jnp.int32, (LANES, LANES), 0) >= RH
    c = lax.broadcasted_iota(jnp.int32, (LANES, LANES), 1) >= RH
    return (r == c).astype(BF16)


def _head_sum(x, ones_bd):
    hi, lo = _split_bf16(x)
    parts = [jnp.dot(hi[:, i * LANES:(i + 1) * LANES], ones_bd, preferred_element_type=F32)
             + jnp.dot(lo[:, i * LANES:(i + 1) * LANES], ones_bd, preferred_element_type=F32)
             for i in range(x.shape[1] // LANES)]
    return jnp.concatenate(parts, axis=1)


def _rwkv_prep_kernel(p_ref, prev_ref, next_ref, mu_ref, kkw_ref, w0_ref, w2_ref, a0_ref, a2_ref, ka_ref,
                      rk_ref, g2_ref,
                      r_o, v_o, kk_o, lwf_o, lwb_o, kdf_o, kdb_o, bqf_o, bqb_o, bonus_o, g_o,
                      *, nb_ctx, bps_ctx, bps_lat):
    i = pl.program_id(0)
    is_lat = i >= nb_ctx
    pos = jnp.where(is_lat, lax.rem(i - nb_ctx, bps_lat), lax.rem(i, bps_ctx))
    last = jnp.where(is_lat, bps_lat - 1, bps_ctx - 1)
    has_prev = (pos != 0).astype(F32)
    has_next = (pos != last).astype(F32)
    rows = lax.broadcasted_iota(jnp.int32, (RB, 1), 0)

    def shifted(lo, hi):
        p = p_ref[:, lo:hi]
        prev = jnp.where(rows == 0, prev_ref[7:8, lo:hi] * has_prev, pltpu.roll(p, 1, axis=0))
        nxt = jnp.where(rows == RB - 1, next_ref[0:1, lo:hi] * has_next, pltpu.roll(p, RB - 1, axis=0))
        return p + (0.5 * (prev + nxt) - p) * mu_ref[:, lo:hi]

    ones_bd = _head_ones()
    r = shifted(0, D)
    k = shifted(D, 2 * D)
    v = shifted(2 * D, 3 * D)
    r_o[...] = r
    v_o[...] = v
    kkf = k * kkw_ref[...]
    nrm = jnp.sqrt(_head_sum(kkf * kkf, ones_bd))
    kk = kkf / jnp.maximum(nrm, 1e-12)
    kk_o[...] = kk
    bonus_o[...] = _head_sum(r * k * rk_ref[...], ones_bd) * v
    base = 3 * D
    outs = ((lwf_o, kdf_o, bqf_o), (lwb_o, kdb_o, bqb_o))
    for d in range(2):
        wl = shifted(base + d * LORA_PAD, base + (d + 1) * LORA_PAD)
        z = w0_ref[d:d + 1, :] + jnp.dot(jnp.tanh(wl).astype(BF16), w2_ref[d], preferred_element_type=F32)
        y = -z
        softplus = jnp.maximum(y, 0.0) + jnp.log(1.0 + jnp.exp(-jnp.abs(y)))
        logw = -softplus - 0.5
        outs[d][0][...] = -jnp.exp(logw)
        al = shifted(base + (2 + d) * LORA_PAD, base + (3 + d) * LORA_PAD)
        a = _sigmoid(a0_ref[d:d + 1, :] + jnp.dot(al.astype(BF16), a2_ref[d], preferred_element_type=F32))
        outs[d][1][...] = k * (1.0 + (a - 1.0) * ka_ref[d:d + 1, :])
        outs[d][2][...] = kk * a
    gl = shifted(base + 4 * LORA_PAD, base + 4 * LORA_PAD + GATE_LORA)
    g_o[...] = jnp.dot(_sigmoid(gl).astype(BF16), g2_ref[...], preferred_element_type=F32)


def rwkv_prep(p_rw, mu, kkw, w0, w2, a0, a2, ka, rk, g2, nb_ctx, bps_ctx, bps_lat):
    n = p_rw.shape[0]
    nb = n // RB
    sub = RB // 8
    row = lambda i: (i, 0)
    const2 = lambda i: (0, 0)
    out = jax.ShapeDtypeStruct((n, D), F32)
    return pl.pallas_call(
        functools.partial(_rwkv_prep_kernel, nb_ctx=nb_ctx, bps_ctx=bps_ctx, bps_lat=bps_lat),
        grid=(nb,),
        in_specs=[pl.BlockSpec((RB, RW_COLS), row),
                  pl.BlockSpec((8, RW_COLS), lambda i: (jnp.maximum(i * sub - 1, 0), 0)),
                  pl.BlockSpec((8, RW_COLS), lambda i: (jnp.minimum((i + 1) * sub, nb * sub - 1), 0)),
                  pl.BlockSpec((1, RW_COLS), const2),
                  pl.BlockSpec((1, D), const2),
                  pl.BlockSpec((2, D), const2),
                  pl.BlockSpec((2, LORA_PAD, D), lambda i: (0, 0, 0)),
                  pl.BlockSpec((2, D), const2),
                  pl.BlockSpec((2, LORA_PAD, D), lambda i: (0, 0, 0)),
                  pl.BlockSpec((2, D), const2),
                  pl.BlockSpec((1, D), const2),
                  pl.BlockSpec((GATE_LORA, D), const2)],
        out_specs=[pl.BlockSpec((RB, D), row)] * 11,
        out_shape=[out] * 11,
        compiler_params=_cp("arbitrary"),
        name="rwkv_prep",
    )(p_rw, p_rw, p_rw, mu, kkw, w0, w2, a0, a2, ka, rk, g2)


def _mm(eq, a, b):
    return jnp.einsum(eq, a.astype(BF16), b.astype(BF16), preferred_element_type=F32)


def _split_bf16(x):
    hi = x.astype(BF16)
    return hi, (x - hi.astype(F32)).astype(BF16)


def _mm3(eq, a, b):
    ah, al = _split_bf16(a)
    bh, bl = _split_bf16(b)
    mm = lambda x, y: jnp.einsum(eq, x, y, preferred_element_type=F32)
    return mm(ah, bh) + (mm(ah, bl) + mm(al, bh))


_NN = "hmk,hkn->hmn"
_NT = "hmk,hnk->hmn"
_TN = "hkm,hkn->hmn"


def _scan_sum(x, reverse):
    rows = lax.broadcasted_iota(jnp.int32, (SC, 1), 0)
    shift = 1
    while shift < SC:
        if reverse:
            x = x + jnp.where(rows < SC - shift, pltpu.roll(x, SC - shift, axis=0), 0.0)
        else:
            x = x + jnp.where(rows >= shift, pltpu.roll(x, shift, axis=0), 0.0)
        shift *= 2
    return x


def _wkv_chunk(refs, s, reverse):
    r, v, kk, lw, kd, bq = (x[...] for x in refs)
    last = 0 if reverse else SC - 1
    L = _scan_sum(lw, reverse)
    ltot = L[last:last + 1, :]
    en = jnp.exp(-L)
    et = jnp.exp(ltot - L)

    def heads(x):
        return jnp.stack([x[:, h * RH:(h + 1) * RH] for h in range(HG)], axis=0)

    ax = heads(-kk * jnp.exp(L - lw))
    rt = heads(r * jnp.exp(L))
    bt = heads(bq * en)
    kt = heads(kd * en)
    bc = heads(bq * et)
    kc = heads(kd * et)
    ptot = heads(jnp.exp(ltot))
    v = heads(v)

    ti = lax.broadcasted_iota(jnp.int32, (SC, SC), 0)
    si = lax.broadcasted_iota(jnp.int32, (SC, SC), 1)
    strict = (si > ti) if reverse else (si < ti)
    incl = (si >= ti) if reverse else (si <= ti)
    a_ab = jnp.where(strict, _mm3(_NT, ax, bt), 0.0)
    a_ak = jnp.where(strict, _mm3(_NT, ax, kt), 0.0)
    a_rb = jnp.where(incl, _mm(_NT, rt, bt), 0.0)
    a_rk = jnp.where(incl, _mm(_NT, rt, kt), 0.0)
    npow = a_ab
    e = a_ab
    for _ in range(int(math.log2(SC)) - 1):
        npow = _mm(_NN, npow, npow)
        e = e + npow + _mm(_NN, e, npow)
    ws = _mm(_NT, jnp.concatenate([ax, rt], axis=1), s)
    w = ws[:, :SC] + _mm(_NN, a_ak, v)
    u = w + _mm(_NN, e, w)
    rho = (w - u) + _mm3(_NN, a_ab, u)
    u = u + rho + _mm(_NN, e, rho)
    y = ws[:, SC:] + _mm(_NN, a_rb, u) + _mm(_NN, a_rk, v)
    s_new = s * ptot + _mm3(_TN, jnp.concatenate([u, v], axis=1), jnp.concatenate([bc, kc], axis=1))
    return jnp.concatenate([y[h] for h in range(HG)], axis=1), s_new


def _scan_kernel(rf, vf, kkf, lwf, kdf, bqf, rb, vb, kkb, lwb, kdb, bqb, s0f, s0b,
                 yf_ref, yb_ref, sfo, sbo, stf, stb, *, nc):
    c = pl.program_id(2)

    @pl.when(c == 0)
    def _():
        stf[...] = s0f[...]
        stb[...] = s0b[...]

    for reverse, refs, st, yref in ((False, (rf, vf, kkf, lwf, kdf, bqf), stf, yf_ref),
                                    (True, (rb, vb, kkb, lwb, kdb, bqb), stb, yb_ref)):
        y, s_new = _wkv_chunk(refs, st[...], reverse)
        st[...] = s_new
        yref[...] = y

    @pl.when(c == nc - 1)
    def _():
        sfo[...] = stf[...]
        sbo[...] = stb[...]


def rwkv_scan(r, v, kk, lwf, lwb, kdf, kdb, bqf, bqb, s0f, s0b, row0, n_seq, t_len, name):
    nc = t_len // SC
    off = row0 // SC
    cw = HG * RH
    fwd = lambda s, h, c: (off + s * nc + c, h)
    bwd = lambda s, h, c: (off + s * nc + (nc - 1 - c), h)
    blk = lambda im: pl.BlockSpec((SC, cw), im)
    st_spec = pl.BlockSpec((None, HG, RH, RH), lambda s, h, c: (s, h, 0, 0))
    n_rows = n_seq * t_len
    return pl.pallas_call(
        functools.partial(_scan_kernel, nc=nc),
        grid=(n_seq, RHEADS // HG, nc),
        in_specs=[blk(fwd)] * 6 + [blk(bwd)] * 6 + [st_spec, st_spec],
        out_specs=[pl.BlockSpec((SC, cw), lambda s, h, c: (s * nc + c, h)),
                   pl.BlockSpec((SC, cw), lambda s, h, c: (s * nc + (nc - 1 - c), h)),
                   st_spec, st_spec],
        out_shape=[jax.ShapeDtypeStruct((n_rows, D), F32), jax.ShapeDtypeStruct((n_rows, D), F32),
                   jax.ShapeDtypeStruct(s0f.shape, F32), jax.ShapeDtypeStruct(s0b.shape, F32)],
        scratch_shapes=[pltpu.VMEM((HG, RH, RH), F32), pltpu.VMEM((HG, RH, RH), F32)],
        compiler_params=_cp("arbitrary", "arbitrary", "arbitrary"),
        name=name,
    )(r, v, kk, lwf, kdf, bqf, r, v, kk, lwb, kdb, bqb, s0f, s0b)


def _rwkv_fin_kernel(yf_ref, yb_ref, bonus_ref, g_ref, lg_ref, lb_ref, o_ref):
    ones_bd = _head_ones()
    y = yf_ref[...] + yb_ref[...]
    mu = _head_sum(y, ones_bd) * (1.0 / RH)
    yc = y - mu
    var = _head_sum(yc * yc, ones_bd) * (1.0 / RH)
    yn = yc * lax.rsqrt(var + GN_EPS) * lg_ref[...] + lb_ref[...]
    o_ref[...] = ((yn + bonus_ref[...]) * g_ref[...]).astype(BF16)


def rwkv_finish(yf, yb, bonus, g, lnx_g, lnx_b):
    n = yf.shape[0]
    row = lambda i: (i, 0)
    const2 = lambda i: (0, 0)
    return pl.pallas_call(
        _rwkv_fin_kernel,
        grid=(n // TB,),
        in_specs=[pl.BlockSpec((TB, D), row)] * 4 + [pl.BlockSpec((1, D), const2)] * 2,
        out_specs=pl.BlockSpec((TB, D), row),
        out_shape=jax.ShapeDtypeStruct((n, D), BF16),
        compiler_params=_cp("arbitrary"),
        name="rwkv_finish",
    )(yf, yb, bonus, g, lnx_g, lnx_b)


def _merge_kernel(oa_ref, ob_ref, oc_ref, w_ref, ga_ref, gb_ref, gc_ref, z_ref):
    acc = None
    for n, (o_ref, g_ref) in enumerate(((oa_ref, ga_ref), (ob_ref, gb_ref), (oc_ref, gc_ref))):
        y = _sigmoid(g_ref[...]) * jnp.dot(o_ref[...], w_ref[n], preferred_element_type=F32)
        acc = y if acc is None else acc + y
    z_ref[...] = acc.astype(BF16)


def branch_merge(o_a, o_b, o_c, w_branch, layer, p_gate):
    n = o_a.shape[0]
    tn = 512
    tm = 512 if n % 512 == 0 else TB
    nj = D // tn
    o_spec = pl.BlockSpec((tm, D), lambda j, i: (i, 0))
    gate = lambda b: pl.BlockSpec((tm, tn), lambda j, i: (i, b * nj + j))
    return pl.pallas_call(
        _merge_kernel,
        grid=(nj, n // tm),
        in_specs=[o_spec, o_spec, o_spec,
                  pl.BlockSpec((None, 3, D, tn), lambda j, i: (layer, 0, 0, j)),
                  gate(0), gate(1), gate(2)],
        out_specs=pl.BlockSpec((tm, tn), lambda j, i: (i, j)),
        out_shape=jax.ShapeDtypeStruct((n, D), BF16),
        compiler_params=_cp("arbitrary", "arbitrary"),
        name="branch_merge",
    )(o_a, o_b, o_c, w_branch, p_gate, p_gate, p_gate)


def _bf16_bits(x):
    return lax.bitcast_convert_type(x.astype(BF16).astype(F32), jnp.uint32)


def _pack_rows(x):
    return (_bf16_bits(x[:, :D // 2]) >> 16) | _bf16_bits(x[:, D // 2:])


def _unpack_rows(p):
    lo = lax.bitcast_convert_type(p << 16, F32)
    hi = lax.bitcast_convert_type(p & jnp.uint32(0xFFFF0000), F32)
    return jnp.concatenate([lo, hi], axis=1).astype(BF16)


def _oproj_kernel(z_ref, wo_ref, x_ref, mod_ref, g_ref, b_ref, x1_ref, h2_ref, h2p_ref):
    mix = jnp.dot(z_ref[...], wo_ref[...], preferred_element_type=F32)
    m = mod_ref[...]
    x1 = _ln(ALPHA * x_ref[...] + m[2:3] * mix) * g_ref[...] + b_ref[...]
    x1_ref[...] = x1
    h2 = _ln(x1) * (1.0 + m[4:5]) + m[3:4]
    h2_ref[...] = h2
    h2p_ref[...] = _pack_rows(h2)


def out_proj(z, w_o, layer, x, mod, modrow, ln_g, ln_b):
    n = z.shape[0]
    row = lambda i: (i, 0)
    const2 = lambda i: (0, 0)
    return pl.pallas_call(
        _oproj_kernel,
        grid=(n // TB,),
        in_specs=[pl.BlockSpec((TB, D), row),
                  pl.BlockSpec((None, D, D), lambda i: (layer, 0, 0)),
                  pl.BlockSpec((TB, D), row),
                  pl.BlockSpec((None, 6, D), lambda i: (modrow(i), 0, 0)),
                  pl.BlockSpec((1, D), const2),
                  pl.BlockSpec((1, D), const2)],
        out_specs=[pl.BlockSpec((TB, D), row), pl.BlockSpec((TB, D), row), pl.BlockSpec((TB, D // 2), row)],
        out_shape=[jax.ShapeDtypeStruct((n, D), F32), jax.ShapeDtypeStruct((n, D), F32),
                   jax.ShapeDtypeStruct((n, D // 2), jnp.uint32)],
        compiler_params=_cp("arbitrary"),
        name="out_proj_ln",
    )(z, w_o, x, mod, ln_g, ln_b)


def _router_kernel(h_ref, w_ref, b_ref, idx_ref, wts_ref, hist_ref):
    logits = jnp.dot(h_ref[...], w_ref[...], precision=HI, preferred_element_type=F32) + b_ref[...]
    lane = lax.broadcasted_iota(jnp.int32, (TB, LANES), 1).astype(F32)
    cur = logits
    idx_out = jnp.zeros((TB, LANES), F32)
    val_out = jnp.zeros((TB, LANES), F32)
    hist = jnp.zeros((TB, LANES), F32)
    top = None
    for k in range(TOP_K):
        m = jnp.max(cur, -1, keepdims=True)
        am = jnp.min(jnp.where(cur == m, lane, float(LANES)), -1, keepdims=True)
        sel = lane == am
        top = m if top is None else top
        idx_out = jnp.where(lane == k, am, idx_out)
        val_out = jnp.where(lane == k, jnp.exp(m - top), val_out)
        hist = hist + sel.astype(F32)
        cur = jnp.where(sel, -jnp.inf, cur)
    idx_ref[...] = idx_out.astype(jnp.int32)
    wts_ref[...] = val_out * (1.0 / jnp.sum(val_out, -1, keepdims=True))
    hist_ref[...] = jnp.sum(hist, 0, keepdims=True)


def router(h2, rw_pad, rb_pad):
    n = h2.shape[0]
    nb = n // TB
    return pl.pallas_call(
        _router_kernel,
        grid=(nb,),
        in_specs=[pl.BlockSpec((TB, D), lambda i: (i, 0)),
                  pl.BlockSpec((D, LANES), lambda i: (0, 0)),
                  pl.BlockSpec((1, LANES), lambda i: (0, 0))],
        out_specs=[pl.BlockSpec((TB, LANES), lambda i: (i, 0)),
                   pl.BlockSpec((TB, LANES), lambda i: (i, 0)),
                   pl.BlockSpec((None, 1, LANES), lambda i: (i, 0, 0))],
        out_shape=[jax.ShapeDtypeStruct((n, LANES), jnp.int32),
                   jax.ShapeDtypeStruct((n, LANES), F32),
                   jax.ShapeDtypeStruct((nb, 1, LANES), F32)],
        compiler_params=_cp("arbitrary"),
        name="moe_router",
    )(h2, rw_pad, rb_pad)


def _dest_kernel(idx_ref, base_ref, dest_ref):
    lane = lax.broadcasted_iota(jnp.int32, (TB, LANES), 1).astype(F32)
    ri = lax.broadcasted_iota(jnp.int32, (TB, TB), 0)
    ci = lax.broadcasted_iota(jnp.int32, (TB, TB), 1)
    lower = (ci < ri).astype(BF16)
    idx = idx_ref[...].astype(F32)
    run = base_ref[...]
    dest = jnp.zeros((TB, LANES), F32)
    for k in range(TOP_K):
        e_k = jnp.sum(jnp.where(lane == k, idx, 0), -1, keepdims=True)
        onehot = lane == e_k
        oh = onehot.astype(F32)
        before = jnp.dot(lower, oh.astype(BF16), preferred_element_type=F32)
        d_k = jnp.sum(jnp.where(onehot, run + before, 0.0), -1, keepdims=True)
        dest = jnp.where(lane == k, d_k, dest)
        run = run + jnp.sum(oh, 0, keepdims=True)
    dest_ref[...] = dest.astype(jnp.int32)


def moe_dest(idx, base):
    n = idx.shape[0]
    nb = n // TB
    return pl.pallas_call(
        _dest_kernel,
        grid=(nb,),
        in_specs=[pl.BlockSpec((TB, LANES), lambda i: (i, 0)),
                  pl.BlockSpec((None, 1, LANES), lambda i: (i, 0, 0))],
        out_specs=pl.BlockSpec((TB, LANES), lambda i: (i, 0)),
        out_shape=jax.ShapeDtypeStruct((n, LANES), jnp.int32),
        compiler_params=_cp("arbitrary"),
        name="moe_dest",
    )(idx, base)


def _row_copy(src, src_row, dst, dst_row, sem):
    return pltpu.make_async_copy(src.at[pl.ds(src_row, 1)], dst.at[pl.ds(dst_row, 1)], sem)


def _dispatch_kernel(dest_ref, h_ref, xb_in, xb_out, sem):
    del xb_in

    def issue(t, carry):
        for k in range(TOP_K):
            _row_copy(h_ref, t, xb_out, dest_ref[0, t * TOP_K + k], sem).start()
        return carry

    lax.fori_loop(0, TB, issue, 0)

    def drain(t, carry):
        for k in range(TOP_K):
            _row_copy(h_ref, t, xb_out, dest_ref[0, t * TOP_K + k], sem).wait()
        return carry

    lax.fori_loop(0, TB, drain, 0)


def moe_dispatch(dest_flat, h2, xb_init):
    n = h2.shape[0]
    nb = n // TB
    return pl.pallas_call(
        _dispatch_kernel,
        grid=(nb,),
        in_specs=[pl.BlockSpec((None, 1, TB * TOP_K), lambda i: (i, 0, 0), memory_space=pltpu.SMEM),
                  pl.BlockSpec((TB, D // 2), lambda i: (i, 0)),
                  pl.BlockSpec(memory_space=pl.ANY)],
        out_specs=pl.BlockSpec(memory_space=pl.ANY),
        out_shape=jax.ShapeDtypeStruct(xb_init.shape, xb_init.dtype),
        scratch_shapes=[pltpu.SemaphoreType.DMA(())],
        input_output_aliases={2: 0},
        compiler_params=_cp("arbitrary"),
        name="moe_dispatch",
    )(dest_flat, h2, xb_init)


def _expert_changed(be_ref):
    i = pl.program_id(1)
    return jnp.logical_or(i == 0, be_ref[i] != be_ref[jnp.maximum(i - 1, 0)])


def _expert_up_kernel(be_ref, nv_ref, x_ref, wg_ref, bg_ref, wu_ref, bu_ref, act_ref, wg_bf, wu_bf):
    used = pl.program_id(1) < nv_ref[0]

    @pl.when(jnp.logical_not(used))
    def _():
        act_ref[...] = jnp.zeros_like(act_ref)

    @pl.when(jnp.logical_and(used, _expert_changed(be_ref)))
    def _():
        wg_bf[...] = wg_ref[...].astype(BF16)
        wu_bf[...] = wu_ref[...].astype(BF16)

    @pl.when(used)
    def _():
        xb = _unpack_rows(x_ref[...])
        gl = jnp.minimum(jnp.dot(xb, wg_bf[...], preferred_element_type=F32) + bg_ref[...], SWIGLU_LIMIT)
        lin = jnp.clip(jnp.dot(xb, wu_bf[...], preferred_element_type=F32) + bu_ref[...],
                       -SWIGLU_LIMIT, SWIGLU_LIMIT)
        act_ref[...] = (gl * _sigmoid(SWIGLU_ALPHA * gl) * (lin + 1.0)).astype(BF16)


def expert_up(block_e, n_valid, xb, w_gate, b_gate, w_up, b_up, layer):
    cap = xb.shape[0]
    tn = EXPERT_UP_TN
    rowblk = lambda j, i, be, nv: (jnp.minimum(i, nv[0] - 1), 0)
    wspec = pl.BlockSpec((None, None, D, tn), lambda j, i, be, nv: (layer, be[i], 0, j))
    bspec = pl.BlockSpec((None, None, 1, tn), lambda j, i, be, nv: (layer, be[i], 0, j))
    return pl.pallas_call(
        _expert_up_kernel,
        grid_spec=pltpu.PrefetchScalarGridSpec(
            num_scalar_prefetch=2,
            grid=(D // tn, cap // BM),
            in_specs=[pl.BlockSpec((BM, D // 2), rowblk), wspec, bspec, wspec, bspec],
            out_specs=pl.BlockSpec((BM, tn), lambda j, i, be, nv: (i, j)),
            scratch_shapes=[pltpu.VMEM((D, tn), BF16), pltpu.VMEM((D, tn), BF16)]),
        out_shape=jax.ShapeDtypeStruct((cap, D), BF16),
        compiler_params=pltpu.CompilerParams(dimension_semantics=("arbitrary", "arbitrary"),
                                             vmem_limit_bytes=EXPERT_VMEM_LIMIT),
        name="moe_expert_up",
    )(block_e, n_valid, xb, w_gate, b_gate, w_up, b_up)


def _expert_down_kernel(be_ref, nv_ref, a_ref, wd_ref, bd_ref, y_ref, wd_bf):
    used = pl.program_id(1) < nv_ref[0]

    @pl.when(jnp.logical_not(used))
    def _():
        y_ref[...] = jnp.zeros_like(y_ref)

    @pl.when(jnp.logical_and(used, _expert_changed(be_ref)))
    def _():
        wd_bf[...] = wd_ref[...].astype(BF16)

    @pl.when(used)
    def _():
        y_ref[...] = jnp.dot(a_ref[...], wd_bf[...], preferred_element_type=F32) + bd_ref[...]


def expert_down(block_e, n_valid, act, w_down, b_down, layer):
    cap = act.shape[0]
    tn = EXPERT_TN
    return pl.pallas_call(
        _expert_down_kernel,
        grid_spec=pltpu.PrefetchScalarGridSpec(
            num_scalar_prefetch=2,
            grid=(D // tn, cap // BM),
            in_specs=[pl.BlockSpec((BM, D), lambda j, i, be, nv: (jnp.minimum(i, nv[0] - 1), 0)),
                      pl.BlockSpec((None, None, D, tn), lambda j, i, be, nv: (layer, be[i], 0, j)),
                      pl.BlockSpec((None, None, 1, tn), lambda j, i, be, nv: (layer, be[i], 0, j))],
            out_specs=pl.BlockSpec((BM, tn), lambda j, i, be, nv: (i, j)),
            scratch_shapes=[pltpu.VMEM((D, tn), BF16)]),
        out_shape=jax.ShapeDtypeStruct((cap, D), F32),
        compiler_params=pltpu.CompilerParams(dimension_semantics=("arbitrary", "arbitrary"),
                                             vmem_limit_bytes=EXPERT_VMEM_LIMIT),
        name="moe_expert_down",
    )(block_e, n_valid, act, w_down, b_down)


def _combine_kernel(dest_ref, wts_ref, x1_ref, mod_ref, g_ref, b_ref, modn_ref, yb_ref,
                    x2_ref, hn_ref, rows, sem):
    def issue(t, carry):
        for k in range(TOP_K):
            _row_copy(yb_ref, dest_ref[0, t * TOP_K + k], rows.at[k], t, sem).start()
        return carry

    lax.fori_loop(0, TB, issue, 0)

    def drain(t, carry):
        for k in range(TOP_K):
            _row_copy(yb_ref, dest_ref[0, t * TOP_K + k], rows.at[k], t, sem).wait()
        return carry

    lax.fori_loop(0, TB, drain, 0)
    w = wts_ref[...]
    ffn = rows[0] * w[:, 0:1]
    for k in range(1, TOP_K):
        ffn = ffn + rows[k] * w[:, k:k + 1]
    m = mod_ref[...]
    x2 = _ln(ALPHA * x1_ref[...] + m[5:6] * ffn) * g_ref[...] + b_ref[...]
    x2_ref[...] = x2
    mn = modn_ref[...]
    hn_ref[...] = (_ln(x2) * (1.0 + mn[1:2]) + mn[0:1]).astype(BF16)


def moe_combine(dest_flat, wts, x1, mod, mod_next, modrow, ln_g, ln_b, yb):
    n = x1.shape[0]
    row = lambda i: (i, 0)
    const2 = lambda i: (0, 0)
    modspec = pl.BlockSpec((None, 6, D), lambda i: (modrow(i), 0, 0))
    return pl.pallas_call(
        _combine_kernel,
        grid=(n // TB,),
        in_specs=[pl.BlockSpec((None, 1, TB * TOP_K), lambda i: (i, 0, 0), memory_space=pltpu.SMEM),
                  pl.BlockSpec((TB, LANES), row),
                  pl.BlockSpec((TB, D), row),
                  modspec,
                  pl.BlockSpec((1, D), const2),
                  pl.BlockSpec((1, D), const2),
                  modspec,
                  pl.BlockSpec(memory_space=pl.ANY)],
        out_specs=[pl.BlockSpec((TB, D), row), pl.BlockSpec((TB, D), row)],
        out_shape=[jax.ShapeDtypeStruct((n, D), F32), jax.ShapeDtypeStruct((n, D), BF16)],
        scratch_shapes=[pltpu.VMEM((TOP_K, TB, D), F32), pltpu.SemaphoreType.DMA(())],
        compiler_params=_cp("arbitrary"),
        name="moe_combine",
    )(dest_flat, wts, x1, mod, ln_g, ln_b, mod_next, yb)


def moe_layout(hist):
    h = hist[:, 0, :].astype(jnp.int32)
    counts = jnp.sum(h, 0)
    padded = (counts + BM - 1) // BM * BM
    pad_end = jnp.cumsum(padded)
    pad_start = pad_end - padded
    before = jnp.cumsum(h, 0) - h
    base = (pad_start[None, :] + before).astype(F32)[:, None, :]
    return base, pad_end


def _rot_cols(w):
    q = ROPE // 4
    return jnp.concatenate([-w[..., q:2 * q], w[..., 0:q], -w[..., 3 * q:4 * q], w[..., 2 * q:3 * q]], axis=-1)


def _rope_tables(n):
    rows = n // GRID_W
    row_id = jnp.repeat(jnp.arange(rows, dtype=F32), GRID_W)
    col_id = jnp.tile(jnp.arange(GRID_W, dtype=F32), rows)
    half = ROPE // 2
    inv_freq = ROPE_THETA ** (-jnp.arange(0, half, 2, dtype=F32) / half)
    ang_r = row_id[:, None] * inv_freq
    ang_c = col_id[:, None] * inv_freq
    cos = jnp.concatenate([jnp.cos(ang_r)] * 2 + [jnp.cos(ang_c)] * 2, axis=-1)
    sin = jnp.concatenate([jnp.sin(ang_r)] * 2 + [jnp.sin(ang_c)] * 2, axis=-1)
    return cos, sin


def _pad_rows(w, rows):
    return jnp.pad(w, [(0, 0)] * (w.ndim - 2) + [(0, rows - w.shape[-2]), (0, 0)])


def kernel(x_prompt, x_sample, cache_ckv, cache_krope, state_rwkv_fwd, state_rwkv_bwd, c, c_ctx, w_mod, b_mod, w_in, gmlp_ln_g, gmlp_ln_b, gmlp_ws, gmlp_bs, q_norm_g, w_uq, kv_norm_g, w_ukv, rwkv_mu, rwkv_w0, rwkv_w2, rwkv_a0, rwkv_a2, rwkv_g2, rwkv_kk, rwkv_ka, rwkv_rk, rwkv_lnx_g, rwkv_lnx_b, w_branch, w_o, ln1_g, ln1_b, router_w, router_b, w_gate, b_gate, w_up, b_up, w_down, b_down, ln2_g, ln2_b):
    L = w_mod.shape[0]
    b_ctx, seq, _ = x_prompt.shape
    b_lat, t_lat, _ = x_sample.shape
    past = cache_ckv.shape[2]
    n_ctx = b_ctx * seq
    n_lat = b_lat * t_lat
    n = n_ctx + n_lat
    assert seq % TB == 0 and t_lat % TB == 0 and past % TB == 0 and b_lat + 1 <= 8
    nb_ctx = n_ctx // TB
    bps_lat = t_lat // TB

    def modrow(i):
        return jnp.where(i < nb_ctx, 0, 1 + (i - nb_ctx) // bps_lat)

    cond8 = jnp.zeros((8, D), F32).at[0].set(c_ctx).at[1:1 + b_lat].set(c)
    mod_all = modulation(cond8, w_mod, b_mod).reshape(L, 8, 6, D)

    gm_end = 2 * D
    mla_end = gm_end + Q_LORA + KV_LORA + ROPE
    rw_end = mla_end + 3 * D + 2 * DECAY_LORA + 2 * AAA_LORA + GATE_LORA
    w_gm = w_in[:, :, :gm_end].astype(BF16)
    w_kr = w_in[:, :, mla_end - ROPE:mla_end]
    w_mla = jnp.concatenate([w_in[:, :, gm_end:mla_end], _rot_cols(w_kr)], axis=-1).astype(BF16)
    rw = w_in[:, :, mla_end:rw_end]
    segs = [rw[:, :, :3 * D]]
    mu_segs = [rwkv_mu[:, :3 * D]]
    o = 3 * D
    for width in (DECAY_LORA, DECAY_LORA, AAA_LORA, AAA_LORA):
        segs.append(jnp.pad(rw[:, :, o:o + width], ((0, 0), (0, 0), (0, LORA_PAD - width))))
        mu_segs.append(jnp.pad(rwkv_mu[:, o:o + width], ((0, 0), (0, LORA_PAD - width))))
        o += width
    segs.append(rw[:, :, o:])
    mu_segs.append(rwkv_mu[:, o:])
    w_rw = jnp.concatenate(segs, axis=-1).astype(BF16)
    mu_rw = jnp.concatenate(mu_segs, axis=-1)[:, None, :]
    w_gt = w_in[:, :, rw_end:].astype(BF16)
    w2p = _pad_rows(rwkv_w2, LORA_PAD).astype(BF16)
    a2p = _pad_rows(rwkv_a2, LORA_PAD).astype(BF16)
    g2b = rwkv_g2.astype(BF16)
    wq4 = w_uq.reshape(L, Q_LORA, HEADS, QK_DIM)
    wq_r = wq4[..., NOPE:]
    wq = jnp.concatenate([wq4[..., :NOPE], wq_r, _rot_cols(wq_r)], axis=-1)
    wq = wq.reshape(L, Q_LORA, HEADS * 2 * LANES).astype(BF16)
    wkv4 = w_ukv.reshape(L, KV_LORA, HEADS, NOPE + V_DIM)
    wk = wkv4[..., :NOPE].reshape(L, KV_LORA, D).astype(BF16)
    wv = wkv4[..., NOPE:].reshape(L, KV_LORA, D).astype(BF16)
    ws_b = gmlp_ws.astype(BF16)
    bs_full = jnp.repeat(jnp.swapaxes(gmlp_bs, 1, 2), GMLP_GW, axis=2)
    w_br = w_branch.astype(BF16)
    w_ob = w_o.astype(BF16)
    rw_pad = jnp.pad(router_w, ((0, 0), (0, 0), (0, LANES - N_EXPERTS)))
    rb_pad = jnp.pad(router_b, ((0, 0), (0, LANES - N_EXPERTS)), constant_values=-1e30)[:, None, :]
    bg4 = b_gate[:, :, None, :]
    bu4 = b_up[:, :, None, :]
    bd4 = b_down[:, :, None, :]

    cos, sin = _rope_tables(t_lat)
    cos_all = jnp.concatenate([jnp.ones((n_ctx, ROPE), F32), jnp.tile(cos, (b_lat, 1))], axis=0)
    sin_all = jnp.concatenate([jnp.zeros((n_ctx, ROPE), F32), jnp.tile(sin, (b_lat, 1))], axis=0)
    tabk = jnp.concatenate([cos_all, sin_all], axis=-1)
    tabq = jnp.concatenate([jnp.ones((n, NOPE), F32), tabk], axis=-1)
    tab_cache = jnp.concatenate([jnp.ones((b_lat * past, ROPE), F32), jnp.zeros((b_lat * past, ROPE), F32)], -1)

    zeros_state = jnp.zeros((b_ctx, RHEADS, RH, RH), F32)
    sf_lat = state_rwkv_fwd
    sb_lat = state_rwkv_bwd

    cap = (n * TOP_K // BM + N_EXPERTS) * BM
    x = jnp.concatenate([x_prompt.reshape(n_ctx, D), x_sample.reshape(n_lat, D)], axis=0)
    h = ln_modulate(x, mod_all[0], modrow)
    ckv_out, kr_out, sf_out, sb_out = [], [], [], []
    for l in range(L):
        mod = mod_all[l]
        p_gm = matmul(h, w_gm, l, 1024, "in_proj_gmlp")
        p_mla = matmul(h, w_mla, l, MLA_COLS, "in_proj_mla")
        p_rw = matmul(h, w_rw, l, 768, "in_proj_rwkv")
        p_gate = matmul(h, w_gt, l, 1024, "in_proj_gate")

        o_a = gmlp(p_gm, gmlp_ln_g[l][None], gmlp_ln_b[l][None], ws_b[l], bs_full[l])

        q = q_proj(p_mla, q_norm_g[l][None], wq[l], tabq)
        kcat, vv, ckvn = kv_proj(p_mla, 1, p_mla, (Q_LORA + KV_LORA) // LANES, kv_norm_g[l][None],
                                 wk[l], wv[l], tabk, True)
        cache_c = cache_ckv[:, l].reshape(b_lat * past, KV_LORA)
        cache_r = jnp.pad(cache_krope[:, l].reshape(b_lat * past, ROPE), ((0, 0), (0, ROPE)))
        kcat_c, vv_c, _ = kv_proj(cache_c, 0, cache_r, 0, kv_norm_g[l][None], wk[l], wv[l], tab_cache, False)
        hw2 = HEADS * 2 * LANES
        kcat_lat = jnp.concatenate([kcat_c.reshape(b_lat, past, hw2), kcat[n_ctx:].reshape(b_lat, t_lat, hw2)],
                                   axis=1).reshape(b_lat * (past + t_lat), hw2)
        vv_lat = jnp.concatenate([vv_c.reshape(b_lat, past, D), vv[n_ctx:].reshape(b_lat, t_lat, D)],
                                 axis=1).reshape(b_lat * (past + t_lat), D)
        o_b_ctx = attention(q, 0, kcat, vv, b_ctx, seq, seq, HEADS, "mla_attention_ctx")
        o_b_lat = attention(q, n_ctx, kcat_lat, vv_lat, b_lat, t_lat, past + t_lat, 1, "mla_attention_lat")
        o_b = jnp.concatenate([o_b_ctx, o_b_lat], axis=0)
        ckv_out.append(ckvn[:n_ctx].reshape(b_ctx, seq, KV_LORA))
        kr_out.append(p_mla[:n_ctx, Q_LORA + KV_LORA:Q_LORA + KV_LORA + ROPE].reshape(b_ctx, seq, ROPE))

        (r, v, kk, lwf, lwb, kdf, kdb, bqf, bqb, bonus, gg) = rwkv_prep(
            p_rw, mu_rw[l], rwkv_kk[l][None], rwkv_w0[l], w2p[l], rwkv_a0[l], a2p[l], rwkv_ka[l],
            rwkv_rk[l].reshape(1, D), g2b[l], n_ctx // RB, seq // RB, t_lat // RB)
        yf_c, yb_c, sf_c, sb_c = rwkv_scan(r, v, kk, lwf, lwb, kdf, kdb, bqf, bqb, zeros_state, zeros_state,
                                           0, b_ctx, seq, "rwkv_scan_ctx")
        yf_l, yb_l, _, _ = rwkv_scan(r, v, kk, lwf, lwb, kdf, kdb, bqf, bqb, sf_lat[:, l], sb_lat[:, l],
                                     n_ctx, b_lat, t_lat, "rwkv_scan_lat")
        sf_out.append(sf_c)
        sb_out.append(sb_c)
        o_c = rwkv_finish(jnp.concatenate([yf_c, yf_l], 0), jnp.concatenate([yb_c, yb_l], 0), bonus, gg,
                          rwkv_lnx_g[l][None], rwkv_lnx_b[l][None])

        z = branch_merge(o_a, o_b, o_c, w_br, l, p_gate)
        x1, h2, h2p = out_proj(z, w_ob, l, x, mod, modrow, ln1_g[l][None], ln1_b[l][None])

        idx, wts, hist = router(h2, rw_pad[l], rb_pad[l])
        base, pad_end = moe_layout(hist)
        dest = moe_dest(idx, base)
        dest_flat = dest[:, :TOP_K].reshape(n // TB, 1, TB * TOP_K)
        xb = moe_dispatch(dest_flat, h2p, jnp.zeros((cap, D // 2), jnp.uint32))
        n_valid = (pad_end[N_EXPERTS - 1] // BM).astype(jnp.int32)
        blk_start = jnp.minimum(jnp.arange(cap // BM, dtype=jnp.int32), n_valid - 1) * BM
        block_e = jnp.sum((pad_end[None, :N_EXPERTS] <= blk_start[:, None]).astype(jnp.int32), axis=1)
        block_e = jnp.minimum(block_e, N_EXPERTS - 1)
        nv = n_valid.reshape(1)
        act = expert_up(block_e, nv, xb, w_gate, bg4, w_up, bu4, l)
        yb = expert_down(block_e, nv, act, w_down, bd4, l)
        mod_next = mod_all[min(l + 1, L - 1)]
        x, h = moe_combine(dest_flat, wts, x1, mod, mod_next, modrow, ln2_g[l][None], ln2_b[l][None], yb)

    y_prompt = x[:n_ctx].reshape(b_ctx, seq, D)
    y_sample = x[n_ctx:].reshape(b_lat, t_lat, D)
    return (y_prompt, y_sample, jnp.stack(ckv_out, axis=1), jnp.stack(kr_out, axis=1),
            jnp.stack(sf_out, axis=1), jnp.stack(sb_out, axis=1))
```

```python
import functools
import math

import jax
import jax.numpy as jnp
from jax import lax
from jax.experimental import pallas as pl
from jax.experimental.pallas import tpu as pltpu

F32 = jnp.float32
BF16 = jnp.bfloat16
HI = lax.Precision.HIGHEST

DEPTH_NORM = 4
GRID_W = 64
CHUNK = 128
D = 2048
GMLP_GW = 128
GMLP_GROUPS = D // GMLP_GW
V_DIM = 128
NOPE = 128
ROPE = 64
QK_DIM = NOPE + ROPE
HEADS = D // V_DIM
Q_LORA = 512
KV_LORA = 512
ROPE_THETA = 10000.0
RH = 64
RHEADS = D // RH
DECAY_LORA = 96
AAA_LORA = 96
GATE_LORA = 256
GN_EPS = 64e-5
N_EXPERTS = 32
TOP_K = 4
SWIGLU_LIMIT = 7.0
SWIGLU_ALPHA = 1.702
LN_EPS = 1e-5
RMS_EPS = 1e-6
ALPHA = (2 * DEPTH_NORM) ** 0.25

LANES = 128
VMEM_LIMIT = 48 * 1024 * 1024

TB = 256
RB = 128
SC = 64
HG = 16
BM = 512
EXPERT_TN = 1024
EXPERT_UP_TN = 1024
EXPERT_VMEM_LIMIT = 56 * 1024 * 1024
LORA_PAD = 128
RW_COLS = 3 * D + 4 * LORA_PAD + GATE_LORA
MLA_COLS = Q_LORA + KV_LORA + 2 * ROPE


def _cp(*sem):
    return pltpu.CompilerParams(dimension_semantics=sem, vmem_limit_bytes=VMEM_LIMIT)


def _ln(x):
    mu = jnp.mean(x, -1, keepdims=True)
    xc = x - mu
    var = jnp.mean(xc * xc, -1, keepdims=True)
    return xc * lax.rsqrt(var + LN_EPS)


def _sigmoid(x):
    return 1.0 / (1.0 + jnp.exp(-x))


def _gelu_tanh(x):
    return 0.5 * x * (1.0 + jnp.tanh(math.sqrt(2.0 / math.pi) * (x + 0.044715 * (x * x * x))))


def _mod_kernel(c_ref, w_ref, b_ref, o_ref):
    c = c_ref[...]
    s = (c * _sigmoid(c)).astype(BF16)
    o_ref[...] = jnp.dot(s, w_ref[...].astype(BF16), preferred_element_type=F32) + b_ref[...]


def modulation(cond8, w_mod, b_mod):
    L = w_mod.shape[0]
    tn = 1024
    return pl.pallas_call(
        _mod_kernel,
        grid=(L, 6 * D // tn),
        in_specs=[pl.BlockSpec((8, D), lambda l, j: (0, 0)),
                  pl.BlockSpec((None, D, tn), lambda l, j: (l, 0, j)),
                  pl.BlockSpec((None, 1, tn), lambda l, j: (l, 0, j))],
        out_specs=pl.BlockSpec((None, 8, tn), lambda l, j: (l, 0, j)),
        out_shape=jax.ShapeDtypeStruct((L, 8, 6 * D), F32),
        compiler_params=_cp("arbitrary", "arbitrary"),
        name="modulation",
    )(cond8, w_mod, b_mod.reshape(L, 1, 6 * D))


def _lnmod_kernel(x_ref, mod_ref, h_ref):
    m = mod_ref[...]
    h_ref[...] = (_ln(x_ref[...]) * (1.0 + m[1:2]) + m[0:1]).astype(BF16)


def ln_modulate(x, mod, modrow):
    n = x.shape[0]
    return pl.pallas_call(
        _lnmod_kernel,
        grid=(n // TB,),
        in_specs=[pl.BlockSpec((TB, D), lambda i: (i, 0)),
                  pl.BlockSpec((None, 6, D), lambda i: (modrow(i), 0, 0))],
        out_specs=pl.BlockSpec((TB, D), lambda i: (i, 0)),
        out_shape=jax.ShapeDtypeStruct((n, D), BF16),
        compiler_params=_cp("arbitrary"),
        name="ln_modulate",
    )(x, mod)


def _mm_kernel(x_ref, w_ref, o_ref):
    o_ref[...] = jnp.dot(x_ref[...], w_ref[...], preferred_element_type=F32).astype(o_ref.dtype)


def matmul(x, w, layer, tn, name, out_dtype=F32):
    m, k = x.shape
    nn = w.shape[2]
    tm = 512 if m % 512 == 0 else TB
    return pl.pallas_call(
        _mm_kernel,
        grid=(nn // tn, m // tm),
        in_specs=[pl.BlockSpec((tm, k), lambda j, i: (i, 0)),
                  pl.BlockSpec((None, k, tn), lambda j, i: (layer, 0, j))],
        out_specs=pl.BlockSpec((tm, tn), lambda j, i: (i, j)),
        out_shape=jax.ShapeDtypeStruct((m, nn), out_dtype),
        compiler_params=_cp("arbitrary", "arbitrary"),
        name=name,
    )(x, w)


def _gmlp_kernel(p_ref, g_ref, b_ref, ws_ref, bs_ref, o_ref):
    u = _gelu_tanh(p_ref[:, :D])
    v = _gelu_tanh(p_ref[:, D:])
    v = (_ln(v) * g_ref[...] + b_ref[...]).astype(BF16)
    for g in range(GMLP_GROUPS):
        sl = slice(g * GMLP_GW, (g + 1) * GMLP_GW)
        mixed = jnp.dot(ws_ref[g], v[:, sl], preferred_element_type=F32) + bs_ref[:, sl]
        o_ref[:, sl] = (u[:, sl] * mixed).astype(BF16)


def gmlp(p_gm, ln_g, ln_b, ws, bs_full):
    n = p_gm.shape[0]
    return pl.pallas_call(
        _gmlp_kernel,
        grid=(n // CHUNK,),
        in_specs=[pl.BlockSpec((CHUNK, 2 * D), lambda i: (i, 0)),
                  pl.BlockSpec((1, D), lambda i: (0, 0)),
                  pl.BlockSpec((1, D), lambda i: (0, 0)),
                  pl.BlockSpec((GMLP_GROUPS, CHUNK, CHUNK), lambda i: (0, 0, 0)),
                  pl.BlockSpec((CHUNK, D), lambda i: (0, 0))],
        out_specs=pl.BlockSpec((CHUNK, D), lambda i: (i, 0)),
        out_shape=jax.ShapeDtypeStruct((n, D), BF16),
        compiler_params=_cp("arbitrary"),
        name="gmlp",
    )(p_gm, ln_g, ln_b, ws, bs_full)


def _qproj_kernel(cq_ref, g_ref, w_ref, tab_ref, q_ref):
    cq = cq_ref[...]
    n = (cq * lax.rsqrt(jnp.mean(cq * cq, -1, keepdims=True) + RMS_EPS) * g_ref[...]).astype(BF16)
    tab = tab_ref[...]
    hw = 2 * LANES
    for h in range(HEADS):
        sl = slice(h * hw, (h + 1) * hw)
        q_ref[:, sl] = (jnp.dot(n, w_ref[:, sl], preferred_element_type=F32) * tab).astype(BF16)


def q_proj(p_mla, g, wq, tabq):
    n = p_mla.shape[0]
    hw = 2 * LANES
    return pl.pallas_call(
        _qproj_kernel,
        grid=(n // TB,),
        in_specs=[pl.BlockSpec((TB, Q_LORA), lambda i: (i, 0)),
                  pl.BlockSpec((1, Q_LORA), lambda i: (0, 0)),
                  pl.BlockSpec((Q_LORA, HEADS * hw), lambda i: (0, 0)),
                  pl.BlockSpec((TB, hw), lambda i: (i, 0))],
        out_specs=pl.BlockSpec((TB, HEADS * hw), lambda i: (i, 0)),
        out_shape=jax.ShapeDtypeStruct((n, HEADS * hw), BF16),
        compiler_params=_cp("arbitrary"),
        name="mla_q_proj",
    )(p_mla, g, wq, tabq)


def _kvproj_kernel(ckv_ref, kr_ref, g_ref, wk_ref, wv_ref, tab_ref, kcat_ref, v_ref, ckvn_ref, *, normalize):
    ckv = ckv_ref[...]
    if normalize:
        ckv = ckv * lax.rsqrt(jnp.mean(ckv * ckv, -1, keepdims=True) + RMS_EPS) * g_ref[...]
    ckvn_ref[...] = ckv
    nb = ckv.astype(BF16)
    t = kr_ref[...] * tab_ref[...]
    khi = (t + pltpu.roll(t, ROPE, axis=1)).astype(BF16)
    kn = jnp.dot(nb, wk_ref[...], preferred_element_type=F32).astype(BF16)
    v_ref[...] = jnp.dot(nb, wv_ref[...], preferred_element_type=F32).astype(BF16)
    hw = 2 * LANES
    for h in range(HEADS):
        kcat_ref[:, h * hw:h * hw + LANES] = kn[:, h * LANES:(h + 1) * LANES]
        kcat_ref[:, h * hw + LANES:(h + 1) * hw] = khi


def kv_proj(ckv_src, ckv_col, kr_src, kr_col, g, wk, wv, tabk, normalize):
    n = ckv_src.shape[0]
    hw = 2 * LANES
    return pl.pallas_call(
        functools.partial(_kvproj_kernel, normalize=normalize),
        grid=(n // TB,),
        in_specs=[pl.BlockSpec((TB, KV_LORA), lambda i: (i, ckv_col)),
                  pl.BlockSpec((TB, LANES), lambda i: (i, kr_col)),
                  pl.BlockSpec((1, KV_LORA), lambda i: (0, 0)),
                  pl.BlockSpec((KV_LORA, D), lambda i: (0, 0)),
                  pl.BlockSpec((KV_LORA, D), lambda i: (0, 0)),
                  pl.BlockSpec((TB, LANES), lambda i: (i, 0))],
        out_specs=[pl.BlockSpec((TB, HEADS * hw), lambda i: (i, 0)),
                   pl.BlockSpec((TB, D), lambda i: (i, 0)),
                   pl.BlockSpec((TB, KV_LORA), lambda i: (i, 0))],
        out_shape=[jax.ShapeDtypeStruct((n, HEADS * hw), BF16),
                   jax.ShapeDtypeStruct((n, D), BF16),
                   jax.ShapeDtypeStruct((n, KV_LORA), F32)],
        compiler_params=_cp("arbitrary"),
        name="mla_kv_proj",
    )(ckv_src, kr_src, g, wk, wv, tabk)


def _attn_kernel(q_ref, k_ref, v_ref, o_ref, *, nh):
    hw = 2 * LANES
    for h in range(nh):
        s = lax.dot_general(q_ref[:, h * hw:(h + 1) * hw], k_ref[:, h * hw:(h + 1) * hw],
                            (((1,), (1,)), ((), ())), preferred_element_type=F32)
        s = s * (QK_DIM ** -0.5)
        e = jnp.exp(s - jnp.max(s, -1, keepdims=True))
        p = (e * (1.0 / jnp.sum(e, -1, keepdims=True))).astype(BF16)
        o_ref[:, h * V_DIM:(h + 1) * V_DIM] = jnp.dot(
            p, v_ref[:, h * V_DIM:(h + 1) * V_DIM], preferred_element_type=F32).astype(BF16)


def attention(q, q_row0, kcat, v, n_seq, t_q, n_keys, nh, name):
    hw = 2 * LANES
    qb = t_q // TB
    qoff = q_row0 // TB
    return pl.pallas_call(
        functools.partial(_attn_kernel, nh=nh),
        grid=(n_seq, HEADS // nh, qb),
        in_specs=[pl.BlockSpec((TB, nh * hw), lambda s, h, i: (qoff + s * qb + i, h)),
                  pl.BlockSpec((n_keys, nh * hw), lambda s, h, i: (s, h)),
                  pl.BlockSpec((n_keys, nh * V_DIM), lambda s, h, i: (s, h))],
        out_specs=pl.BlockSpec((TB, nh * V_DIM), lambda s, h, i: (s * qb + i, h)),
        out_shape=jax.ShapeDtypeStruct((n_seq * t_q, D), BF16),
        compiler_params=_cp("arbitrary", "arbitrary", "arbitrary"),
        name=name,
    )(q, kcat, v)


def _head_ones():
    r = lax.broadcasted_iota(jnp.int32, (LANES, LANES), 0) >= RH
    c = lax.broadcasted_iota(jnp.int32, (LANES, LANES), 1) >= RH
    return (r == c).astype(BF16)


def _head_sum(x, ones_bd):
    hi, lo = _split_bf16(x)
    parts = [jnp.dot(hi[:, i * LANES:(i + 1) * LANES], ones_bd, preferred_element_type=F32)
             + jnp.dot(lo[:, i * LANES:(i + 1) * LANES], ones_bd, preferred_element_type=F32)
             for i in range(x.shape[1] // LANES)]
    return jnp.concatenate(parts, axis=1)


def _rwkv_prep_kernel(p_ref, prev_ref, next_ref, mu_ref, kkw_ref, w0_ref, w2_ref, a0_ref, a2_ref, ka_ref,
                      rk_ref, g2_ref,
                      r_o, v_o, kk_o, lwf_o, lwb_o, kdf_o, kdb_o, bqf_o, bqb_o, bonus_o, g_o,
                      *, nb_ctx, bps_ctx, bps_lat):
    i = pl.program_id(0)
    is_lat = i >= nb_ctx
    pos = jnp.where(is_lat, lax.rem(i - nb_ctx, bps_lat), lax.rem(i, bps_ctx))
    last = jnp.where(is_lat, bps_lat - 1, bps_ctx - 1)
    has_prev = (pos != 0).astype(F32)
    has_next = (pos != last).astype(F32)
    rows = lax.broadcasted_iota(jnp.int32, (RB, 1), 0)

    def shifted(lo, hi):
        p = p_ref[:, lo:hi]
        prev = jnp.where(rows == 0, prev_ref[7:8, lo:hi] * has_prev, pltpu.roll(p, 1, axis=0))
        nxt = jnp.where(rows == RB - 1, next_ref[0:1, lo:hi] * has_next, pltpu.roll(p, RB - 1, axis=0))
        return p + (0.5 * (prev + nxt) - p) * mu_ref[:, lo:hi]

    ones_bd = _head_ones()
    r = shifted(0, D)
    k = shifted(D, 2 * D)
    v = shifted(2 * D, 3 * D)
    r_o[...] = r
    v_o[...] = v
    kkf = k * kkw_ref[...]
    nrm = jnp.sqrt(_head_sum(kkf * kkf, ones_bd))
    kk = kkf / jnp.maximum(nrm, 1e-12)
    kk_o[...] = kk
    bonus_o[...] = _head_sum(r * k * rk_ref[...], ones_bd) * v
    base = 3 * D
    outs = ((lwf_o, kdf_o, bqf_o), (lwb_o, kdb_o, bqb_o))
    for d in range(2):
        wl = shifted(base + d * LORA_PAD, base + (d + 1) * LORA_PAD)
        z = w0_ref[d:d + 1, :] + jnp.dot(jnp.tanh(wl).astype(BF16), w2_ref[d], preferred_element_type=F32)
        y = -z
        softplus = jnp.maximum(y, 0.0) + jnp.log(1.0 + jnp.exp(-jnp.abs(y)))
        logw = -softplus - 0.5
        outs[d][0][...] = -jnp.exp(logw)
        al = shifted(base + (2 + d) * LORA_PAD, base + (3 + d) * LORA_PAD)
        a = _sigmoid(a0_ref[d:d + 1, :] + jnp.dot(al.astype(BF16), a2_ref[d], preferred_element_type=F32))
        outs[d][1][...] = k * (1.0 + (a - 1.0) * ka_ref[d:d + 1, :])
        outs[d][2][...] = kk * a
    gl = shifted(base + 4 * LORA_PAD, base + 4 * LORA_PAD + GATE_LORA)
    g_o[...] = jnp.dot(_sigmoid(gl).astype(BF16), g2_ref[...], preferred_element_type=F32)


def rwkv_prep(p_rw, mu, kkw, w0, w2, a0, a2, ka, rk, g2, nb_ctx, bps_ctx, bps_lat):
    n = p_rw.shape[0]
    nb = n // RB
    sub = RB // 8
    row = lambda i: (i, 0)
    const2 = lambda i: (0, 0)
    out = jax.ShapeDtypeStruct((n, D), F32)
    return pl.pallas_call(
        functools.partial(_rwkv_prep_kernel, nb_ctx=nb_ctx, bps_ctx=bps_ctx, bps_lat=bps_lat),
        grid=(nb,),
        in_specs=[pl.BlockSpec((RB, RW_COLS), row),
                  pl.BlockSpec((8, RW_COLS), lambda i: (jnp.maximum(i * sub - 1, 0), 0)),
                  pl.BlockSpec((8, RW_COLS), lambda i: (jnp.minimum((i + 1) * sub, nb * sub - 1), 0)),
                  pl.BlockSpec((1, RW_COLS), const2),
                  pl.BlockSpec((1, D), const2),
                  pl.BlockSpec((2, D), const2),
                  pl.BlockSpec((2, LORA_PAD, D), lambda i: (0, 0, 0)),
                  pl.BlockSpec((2, D), const2),
                  pl.BlockSpec((2, LORA_PAD, D), lambda i: (0, 0, 0)),
                  pl.BlockSpec((2, D), const2),
                  pl.BlockSpec((1, D), const2),
                  pl.BlockSpec((GATE_LORA, D), const2)],
        out_specs=[pl.BlockSpec((RB, D), row)] * 11,
        out_shape=[out] * 11,
        compiler_params=_cp("arbitrary"),
        name="rwkv_prep",
    )(p_rw, p_rw, p_rw, mu, kkw, w0, w2, a0, a2, ka, rk, g2)


def _mm(eq, a, b):
    return jnp.einsum(eq, a.astype(BF16), b.astype(BF16), preferred_element_type=F32)


def _split_bf16(x):
    hi = x.astype(BF16)
    return hi, (x - hi.astype(F32)).astype(BF16)


def _mm3(eq, a, b):
    ah, al = _split_bf16(a)
    bh, bl = _split_bf16(b)
    mm = lambda x, y: jnp.einsum(eq, x, y, preferred_element_type=F32)
    return mm(ah, bh) + (mm(ah, bl) + mm(al, bh))


_NN = "hmk,hkn->hmn"
_NT = "hmk,hnk->hmn"
_TN = "hkm,hkn->hmn"


def _scan_sum(x, reverse):
    rows = lax.broadcasted_iota(jnp.int32, (SC, 1), 0)
    shift = 1
    while shift < SC:
        if reverse:
            x = x + jnp.where(rows < SC - shift, pltpu.roll(x, SC - shift, axis=0), 0.0)
        else:
            x = x + jnp.where(rows >= shift, pltpu.roll(x, shift, axis=0), 0.0)
        shift *= 2
    return x


def _wkv_chunk(refs, s, reverse):
    r, v, kk, lw, kd, bq = (x[...] for x in refs)
    last = 0 if reverse else SC - 1
    L = _scan_sum(lw, reverse)
    ltot = L[last:last + 1, :]
    en = jnp.exp(-L)
    et = jnp.exp(ltot - L)

    def heads(x):
        return jnp.stack([x[:, h * RH:(h + 1) * RH] for h in range(HG)], axis=0)

    ax = heads(-kk * jnp.exp(L - lw))
    rt = heads(r * jnp.exp(L))
    bt = heads(bq * en)
    kt = heads(kd * en)
    bc = heads(bq * et)
    kc = heads(kd * et)
    ptot = heads(jnp.exp(ltot))
    v = heads(v)

    ti = lax.broadcasted_iota(jnp.int32, (SC, SC), 0)
    si = lax.broadcasted_iota(jnp.int32, (SC, SC), 1)
    strict = (si > ti) if reverse else (si < ti)
    incl = (si >= ti) if reverse else (si <= ti)
    a_ab = jnp.where(strict, _mm3(_NT, ax, bt), 0.0)
    a_ak = jnp.where(strict, _mm3(_NT, ax, kt), 0.0)
    a_rb = jnp.where(incl, _mm(_NT, rt, bt), 0.0)
    a_rk = jnp.where(incl, _mm(_NT, rt, kt), 0.0)
    npow = a_ab
    e = a_ab
    for _ in range(int(math.log2(SC)) - 2):
        npow = _mm(_NN, npow, npow)
        e = e + npow + _mm(_NN, e, npow)
    ws = _mm(_NT, jnp.concatenate([ax, rt], axis=1), s)
    w = ws[:, :SC] + _mm(_NN, a_ak, v)
    u = w + _mm(_NN, e, w)
    rho = (w - u) + _mm3(_NN, a_ab, u)
    u = u + rho + _mm(_NN, e, rho)
    y = ws[:, SC:] + _mm(_NN, a_rb, u) + _mm(_NN, a_rk, v)
    s_new = s * ptot + _mm3(_TN, jnp.concatenate([u, v], axis=1), jnp.concatenate([bc, kc], axis=1))
    return jnp.concatenate([y[h] for h in range(HG)], axis=1), s_new


def _scan_kernel(rf, vf, kkf, lwf, kdf, bqf, rb, vb, kkb, lwb, kdb, bqb, s0f, s0b,
                 yf_ref, yb_ref, sfo, sbo, stf, stb, *, nc):
    c = pl.program_id(2)

    @pl.when(c == 0)
    def _():
        stf[...] = s0f[...]
        stb[...] = s0b[...]

    for reverse, refs, st, yref in ((False, (rf, vf, kkf, lwf, kdf, bqf), stf, yf_ref),
                                    (True, (rb, vb, kkb, lwb, kdb, bqb), stb, yb_ref)):
        y, s_new = _wkv_chunk(refs, st[...], reverse)
        st[...] = s_new
        yref[...] = y

    @pl.when(c == nc - 1)
    def _():
        sfo[...] = stf[...]
        sbo[...] = stb[...]


def rwkv_scan(r, v, kk, lwf, lwb, kdf, kdb, bqf, bqb, s0f, s0b, row0, n_seq, t_len, name):
    nc = t_len // SC
    off = row0 // SC
    cw = HG * RH
    fwd = lambda s, h, c: (off + s * nc + c, h)
    bwd = lambda s, h, c: (off + s * nc + (nc - 1 - c), h)
    blk = lambda im: pl.BlockSpec((SC, cw), im)
    st_spec = pl.BlockSpec((None, HG, RH, RH), lambda s, h, c: (s, h, 0, 0))
    n_rows = n_seq * t_len
    return pl.pallas_call(
        functools.partial(_scan_kernel, nc=nc),
        grid=(n_seq, RHEADS // HG, nc),
        in_specs=[blk(fwd)] * 6 + [blk(bwd)] * 6 + [st_spec, st_spec],
        out_specs=[pl.BlockSpec((SC, cw), lambda s, h, c: (s * nc + c, h)),
                   pl.BlockSpec((SC, cw), lambda s, h, c: (s * nc + (nc - 1 - c), h)),
                   st_spec, st_spec],
        out_shape=[jax.ShapeDtypeStruct((n_rows, D), F32), jax.ShapeDtypeStruct((n_rows, D), F32),
                   jax.ShapeDtypeStruct(s0f.shape, F32), jax.ShapeDtypeStruct(s0b.shape, F32)],
        scratch_shapes=[pltpu.VMEM((HG, RH, RH), F32), pltpu.VMEM((HG, RH, RH), F32)],
        compiler_params=_cp("arbitrary", "arbitrary", "arbitrary"),
        name=name,
    )(r, v, kk, lwf, kdf, bqf, r, v, kk, lwb, kdb, bqb, s0f, s0b)


def _rwkv_fin_kernel(yf_ref, yb_ref, bonus_ref, g_ref, lg_ref, lb_ref, o_ref):
    ones_bd = _head_ones()
    y = yf_ref[...] + yb_ref[...]
    mu = _head_sum(y, ones_bd) * (1.0 / RH)
    yc = y - mu
    var = _head_sum(yc * yc, ones_bd) * (1.0 / RH)
    yn = yc * lax.rsqrt(var + GN_EPS) * lg_ref[...] + lb_ref[...]
    o_ref[...] = ((yn + bonus_ref[...]) * g_ref[...]).astype(BF16)


def rwkv_finish(yf, yb, bonus, g, lnx_g, lnx_b):
    n = yf.shape[0]
    row = lambda i: (i, 0)
    const2 = lambda i: (0, 0)
    return pl.pallas_call(
        _rwkv_fin_kernel,
        grid=(n // TB,),
        in_specs=[pl.BlockSpec((TB, D), row)] * 4 + [pl.BlockSpec((1, D), const2)] * 2,
        out_specs=pl.BlockSpec((TB, D), row),
        out_shape=jax.ShapeDtypeStruct((n, D), BF16),
        compiler_params=_cp("arbitrary"),
        name="rwkv_finish",
    )(yf, yb, bonus, g, lnx_g, lnx_b)


def _merge_kernel(oa_ref, ob_ref, oc_ref, w_ref, ga_ref, gb_ref, gc_ref, z_ref):
    acc = None
    for n, (o_ref, g_ref) in enumerate(((oa_ref, ga_ref), (ob_ref, gb_ref), (oc_ref, gc_ref))):
        y = _sigmoid(g_ref[...]) * jnp.dot(o_ref[...], w_ref[n], preferred_element_type=F32)
        acc = y if acc is None else acc + y
    z_ref[...] = acc.astype(BF16)


def branch_merge(o_a, o_b, o_c, w_branch, layer, p_gate):
    n = o_a.shape[0]
    tn = 512
    tm = 512 if n % 512 == 0 else TB
    nj = D // tn
    o_spec = pl.BlockSpec((tm, D), lambda j, i: (i, 0))
    gate = lambda b: pl.BlockSpec((tm, tn), lambda j, i: (i, b * nj + j))
    return pl.pallas_call(
        _merge_kernel,
        grid=(nj, n // tm),
        in_specs=[o_spec, o_spec, o_spec,
                  pl.BlockSpec((None, 3, D, tn), lambda j, i: (layer, 0, 0, j)),
                  gate(0), gate(1), gate(2)],
        out_specs=pl.BlockSpec((tm, tn), lambda j, i: (i, j)),
        out_shape=jax.ShapeDtypeStruct((n, D), BF16),
        compiler_params=_cp("arbitrary", "arbitrary"),
        name="branch_merge",
    )(o_a, o_b, o_c, w_branch, p_gate, p_gate, p_gate)


def _bf16_bits(x):
    return lax.bitcast_convert_type(x.astype(BF16).astype(F32), jnp.uint32)


def _pack_rows(x):
    return (_bf16_bits(x[:, :D // 2]) >> 16) | _bf16_bits(x[:, D // 2:])


def _unpack_rows(p):
    lo = lax.bitcast_convert_type(p << 16, F32)
    hi = lax.bitcast_convert_type(p & jnp.uint32(0xFFFF0000), F32)
    return jnp.concatenate([lo, hi], axis=1).astype(BF16)


def _oproj_kernel(z_ref, wo_ref, x_ref, mod_ref, g_ref, b_ref, x1_ref, h2_ref, h2p_ref):
    mix = jnp.dot(z_ref[...], wo_ref[...], preferred_element_type=F32)
    m = mod_ref[...]
    x1 = _ln(ALPHA * x_ref[...] + m[2:3] * mix) * g_ref[...] + b_ref[...]
    x1_ref[...] = x1
    h2 = _ln(x1) * (1.0 + m[4:5]) + m[3:4]
    h2_ref[...] = h2
    h2p_ref[...] = _pack_rows(h2)


def out_proj(z, w_o, layer, x, mod, modrow, ln_g, ln_b):
    n = z.shape[0]
    row = lambda i: (i, 0)
    const2 = lambda i: (0, 0)
    return pl.pallas_call(
        _oproj_kernel,
        grid=(n // TB,),
        in_specs=[pl.BlockSpec((TB, D), row),
                  pl.BlockSpec((None, D, D), lambda i: (layer, 0, 0)),
                  pl.BlockSpec((TB, D), row),
                  pl.BlockSpec((None, 6, D), lambda i: (modrow(i), 0, 0)),
                  pl.BlockSpec((1, D), const2),
                  pl.BlockSpec((1, D), const2)],
        out_specs=[pl.BlockSpec((TB, D), row), pl.BlockSpec((TB, D), row), pl.BlockSpec((TB, D // 2), row)],
        out_shape=[jax.ShapeDtypeStruct((n, D), F32), jax.ShapeDtypeStruct((n, D), F32),
                   jax.ShapeDtypeStruct((n, D // 2), jnp.uint32)],
        compiler_params=_cp("arbitrary"),
        name="out_proj_ln",
    )(z, w_o, x, mod, ln_g, ln_b)


def _router_kernel(h_ref, w_ref, b_ref, idx_ref, wts_ref, hist_ref):
    logits = jnp.dot(h_ref[...], w_ref[...], precision=HI, preferred_element_type=F32) + b_ref[...]
    lane = lax.broadcasted_iota(jnp.int32, (TB, LANES), 1).astype(F32)
    cur = logits
    idx_out = jnp.zeros((TB, LANES), F32)
    val_out = jnp.zeros((TB, LANES), F32)
    hist = jnp.zeros((TB, LANES), F32)
    top = None
    for k in range(TOP_K):
        m = jnp.max(cur, -1, keepdims=True)
        am = jnp.min(jnp.where(cur == m, lane, float(LANES)), -1, keepdims=True)
        sel = lane == am
        top = m if top is None else top
        idx_out = jnp.where(lane == k, am, idx_out)
        val_out = jnp.where(lane == k, jnp.exp(m - top), val_out)
        hist = hist + sel.astype(F32)
        cur = jnp.where(sel, -jnp.inf, cur)
    idx_ref[...] = idx_out.astype(jnp.int32)
    wts_ref[...] = val_out * (1.0 / jnp.sum(val_out, -1, keepdims=True))
    hist_ref[...] = jnp.sum(hist, 0, keepdims=True)


def router(h2, rw_pad, rb_pad):
    n = h2.shape[0]
    nb = n // TB
    return pl.pallas_call(
        _router_kernel,
        grid=(nb,),
        in_specs=[pl.BlockSpec((TB, D), lambda i: (i, 0)),
                  pl.BlockSpec((D, LANES), lambda i: (0, 0)),
                  pl.BlockSpec((1, LANES), lambda i: (0, 0))],
        out_specs=[pl.BlockSpec((TB, LANES), lambda i: (i, 0)),
                   pl.BlockSpec((TB, LANES), lambda i: (i, 0)),
                   pl.BlockSpec((None, 1, LANES), lambda i: (i, 0, 0))],
        out_shape=[jax.ShapeDtypeStruct((n, LANES), jnp.int32),
                   jax.ShapeDtypeStruct((n, LANES), F32),
                   jax.ShapeDtypeStruct((nb, 1, LANES), F32)],
        compiler_params=_cp("arbitrary"),
        name="moe_router",
    )(h2, rw_pad, rb_pad)


def _dest_kernel(idx_ref, base_ref, dest_ref):
    lane = lax.broadcasted_iota(jnp.int32, (TB, LANES), 1).astype(F32)
    ri = lax.broadcasted_iota(jnp.int32, (TB, TB), 0)
    ci = lax.broadcasted_iota(jnp.int32, (TB, TB), 1)
    lower = (ci < ri).astype(BF16)
    idx = idx_ref[...].astype(F32)
    run = base_ref[...]
    dest = jnp.zeros((TB, LANES), F32)
    for k in range(TOP_K):
        e_k = jnp.sum(jnp.where(lane == k, idx, 0), -1, keepdims=True)
        onehot = lane == e_k
        oh = onehot.astype(F32)
        before = jnp.dot(lower, oh.astype(BF16), preferred_element_type=F32)
        d_k = jnp.sum(jnp.where(onehot, run + before, 0.0), -1, keepdims=True)
        dest = jnp.where(lane == k, d_k, dest)
        run = run + jnp.sum(oh, 0, keepdims=True)
    dest_ref[...] = dest.astype(jnp.int32)


def moe_dest(idx, base):
    n = idx.shape[0]
    nb = n // TB
    return pl.pallas_call(
        _dest_kernel,
        grid=(nb,),
        in_specs=[pl.BlockSpec((TB, LANES), lambda i: (i, 0)),
                  pl.BlockSpec((None, 1, LANES), lambda i: (i, 0, 0))],
        out_specs=pl.BlockSpec((TB, LANES), lambda i: (i, 0)),
        out_shape=jax.ShapeDtypeStruct((n, LANES), jnp.int32),
        compiler_params=_cp("arbitrary"),
        name="moe_dest",
    )(idx, base)


def _row_copy(src, src_row, dst, dst_row, sem):
    return pltpu.make_async_copy(src.at[pl.ds(src_row, 1)], dst.at[pl.ds(dst_row, 1)], sem)


def _dispatch_kernel(dest_ref, h_ref, xb_in, xb_out, sem):
    del xb_in

    def issue(t, carry):
        for k in range(TOP_K):
            _row_copy(h_ref, t, xb_out, dest_ref[0, t * TOP_K + k], sem).start()
        return carry

    lax.fori_loop(0, TB, issue, 0)

    def drain(t, carry):
        for k in range(TOP_K):
            _row_copy(h_ref, t, xb_out, dest_ref[0, t * TOP_K + k], sem).wait()
        return carry

    lax.fori_loop(0, TB, drain, 0)


def moe_dispatch(dest_flat, h2, xb_init):
    n = h2.shape[0]
    nb = n // TB
    return pl.pallas_call(
        _dispatch_kernel,
        grid=(nb,),
        in_specs=[pl.BlockSpec((None, 1, TB * TOP_K), lambda i: (i, 0, 0), memory_space=pltpu.SMEM),
                  pl.BlockSpec((TB, D // 2), lambda i: (i, 0)),
                  pl.BlockSpec(memory_space=pl.ANY)],
        out_specs=pl.BlockSpec(memory_space=pl.ANY),
        out_shape=jax.ShapeDtypeStruct(xb_init.shape, xb_init.dtype),
        scratch_shapes=[pltpu.SemaphoreType.DMA(())],
        input_output_aliases={2: 0},
        compiler_params=_cp("arbitrary"),
        name="moe_dispatch",
    )(dest_flat, h2, xb_init)


def _expert_changed(be_ref):
    i = pl.program_id(1)
    return jnp.logical_or(i == 0, be_ref[i] != be_ref[jnp.maximum(i - 1, 0)])


def _expert_up_kernel(be_ref, nv_ref, x_ref, wg_ref, bg_ref, wu_ref, bu_ref, act_ref, wg_bf, wu_bf):
    used = pl.program_id(1) < nv_ref[0]

    @pl.when(jnp.logical_not(used))
    def _():
        act_ref[...] = jnp.zeros_like(act_ref)

    @pl.when(jnp.logical_and(used, _expert_changed(be_ref)))
    def _():
        wg_bf[...] = wg_ref[...].astype(BF16)
        wu_bf[...] = wu_ref[...].astype(BF16)

    @pl.when(used)
    def _():
        half = BM // 2
        for r in range(2):
            rows = slice(r * half, (r + 1) * half)
            xb = _unpack_rows(x_ref[rows, :])
            gl = jnp.minimum(jnp.dot(xb, wg_bf[...], preferred_element_type=F32) + bg_ref[...], SWIGLU_LIMIT)
            lin = jnp.clip(jnp.dot(xb, wu_bf[...], preferred_element_type=F32) + bu_ref[...],
                           -SWIGLU_LIMIT, SWIGLU_LIMIT)
            act_ref[rows, :] = (gl * _sigmoid(SWIGLU_ALPHA * gl) * (lin + 1.0)).astype(BF16)


def expert_up(block_e, n_valid, xb, w_gate, b_gate, w_up, b_up, layer):
    cap = xb.shape[0]
    tn = EXPERT_UP_TN
    rowblk = lambda j, i, be, nv: (jnp.minimum(i, nv[0] - 1), 0)
    wspec = pl.BlockSpec((None, None, D, tn), lambda j, i, be, nv: (layer, be[i], 0, j))
    bspec = pl.BlockSpec((None, None, 1, tn), lambda j, i, be, nv: (layer, be[i], 0, j))
    return pl.pallas_call(
        _expert_up_kernel,
        grid_spec=pltpu.PrefetchScalarGridSpec(
            num_scalar_prefetch=2,
            grid=(D // tn, cap // BM),
            in_specs=[pl.BlockSpec((BM, D // 2), rowblk), wspec, bspec, wspec, bspec],
            out_specs=pl.BlockSpec((BM, tn), lambda j, i, be, nv: (i, j)),
            scratch_shapes=[pltpu.VMEM((D, tn), BF16), pltpu.VMEM((D, tn), BF16)]),
        out_shape=jax.ShapeDtypeStruct((cap, D), BF16),
        compiler_params=pltpu.CompilerParams(dimension_semantics=("arbitrary", "arbitrary"),
                                             vmem_limit_bytes=EXPERT_VMEM_LIMIT),
        name="moe_expert_up",
    )(block_e, n_valid, xb, w_gate, b_gate, w_up, b_up)


def _expert_down_kernel(be_ref, nv_ref, a_ref, wd_ref, bd_ref, y_ref, wd_bf):
    used = pl.program_id(1) < nv_ref[0]

    @pl.when(jnp.logical_not(used))
    def _():
        y_ref[...] = jnp.zeros_like(y_ref)

    @pl.when(jnp.logical_and(used, _expert_changed(be_ref)))
    def _():
        wd_bf[...] = wd_ref[...].astype(BF16)

    @pl.when(used)
    def _():
        y_ref[...] = jnp.dot(a_ref[...], wd_bf[...], preferred_element_type=F32) + bd_ref[...]


def expert_down(block_e, n_valid, act, w_down, b_down, layer):
    cap = act.shape[0]
    tn = EXPERT_TN
    return pl.pallas_call(
        _expert_down_kernel,
        grid_spec=pltpu.PrefetchScalarGridSpec(
            num_scalar_prefetch=2,
            grid=(D // tn, cap // BM),
            in_specs=[pl.BlockSpec((BM, D), lambda j, i, be, nv: (jnp.minimum(i, nv[0] - 1), 0)),
                      pl.BlockSpec((None, None, D, tn), lambda j, i, be, nv: (layer, be[i], 0, j)),
                      pl.BlockSpec((None, None, 1, tn), lambda j, i, be, nv: (layer, be[i], 0, j))],
            out_specs=pl.BlockSpec((BM, tn), lambda j, i, be, nv: (i, j)),
            scratch_shapes=[pltpu.VMEM((D, tn), BF16)]),
        out_shape=jax.ShapeDtypeStruct((cap, D), F32),
        compiler_params=pltpu.CompilerParams(dimension_semantics=("arbitrary", "arbitrary"),
                                             vmem_limit_bytes=EXPERT_VMEM_LIMIT),
        name="moe_expert_down",
    )(block_e, n_valid, act, w_down, b_down)


def _combine_kernel(dest_ref, wts_ref, x1_ref, mod_ref, g_ref, b_ref, modn_ref, yb_ref,
                    x2_ref, hn_ref, rows, sem):
    def issue(t, carry):
        for k in range(TOP_K):
            _row_copy(yb_ref, dest_ref[0, t * TOP_K + k], rows.at[k], t, sem).start()
        return carry

    lax.fori_loop(0, TB, issue, 0)

    def drain(t, carry):
        for k in range(TOP_K):
            _row_copy(yb_ref, dest_ref[0, t * TOP_K + k], rows.at[k], t, sem).wait()
        return carry

    lax.fori_loop(0, TB, drain, 0)
    w = wts_ref[...]
    ffn = rows[0] * w[:, 0:1]
    for k in range(1, TOP_K):
        ffn = ffn + rows[k] * w[:, k:k + 1]
    m = mod_ref[...]
    x2 = _ln(ALPHA * x1_ref[...] + m[5:6] * ffn) * g_ref[...] + b_ref[...]
    x2_ref[...] = x2
    mn = modn_ref[...]
    hn_ref[...] = (_ln(x2) * (1.0 + mn[1:2]) + mn[0:1]).astype(BF16)


def moe_combine(dest_flat, wts, x1, mod, mod_next, modrow, ln_g, ln_b, yb):
    n = x1.shape[0]
    row = lambda i: (i, 0)
    const2 = lambda i: (0, 0)
    modspec = pl.BlockSpec((None, 6, D), lambda i: (modrow(i), 0, 0))
    return pl.pallas_call(
        _combine_kernel,
        grid=(n // TB,),
        in_specs=[pl.BlockSpec((None, 1, TB * TOP_K), lambda i: (i, 0, 0), memory_space=pltpu.SMEM),
                  pl.BlockSpec((TB, LANES), row),
                  pl.BlockSpec((TB, D), row),
                  modspec,
                  pl.BlockSpec((1, D), const2),
                  pl.BlockSpec((1, D), const2),
                  modspec,
                  pl.BlockSpec(memory_space=pl.ANY)],
        out_specs=[pl.BlockSpec((TB, D), row), pl.BlockSpec((TB, D), row)],
        out_shape=[jax.ShapeDtypeStruct((n, D), F32), jax.ShapeDtypeStruct((n, D), BF16)],
        scratch_shapes=[pltpu.VMEM((TOP_K, TB, D), F32), pltpu.SemaphoreType.DMA(())],
        compiler_params=_cp("arbitrary"),
        name="moe_combine",
    )(dest_flat, wts, x1, mod, ln_g, ln_b, mod_next, yb)


def moe_layout(hist):
    h = hist[:, 0, :].astype(jnp.int32)
    counts = jnp.sum(h, 0)
    padded = (counts + BM - 1) // BM * BM
    pad_end = jnp.cumsum(padded)
    pad_start = pad_end - padded
    before = jnp.cumsum(h, 0) - h
    base = (pad_start[None, :] + before).astype(F32)[:, None, :]
    return base, pad_end


def _rot_cols(w):
    q = ROPE // 4
    return jnp.concatenate([-w[..., q:2 * q], w[..., 0:q], -w[..., 3 * q:4 * q], w[..., 2 * q:3 * q]], axis=-1)


def _rope_tables(n):
    rows = n // GRID_W
    row_id = jnp.repeat(jnp.arange(rows, dtype=F32), GRID_W)
    col_id = jnp.tile(jnp.arange(GRID_W, dtype=F32), rows)
    half = ROPE // 2
    inv_freq = ROPE_THETA ** (-jnp.arange(0, half, 2, dtype=F32) / half)
    ang_r = row_id[:, None] * inv_freq
    ang_c = col_id[:, None] * inv_freq
    cos = jnp.concatenate([jnp.cos(ang_r)] * 2 + [jnp.cos(ang_c)] * 2, axis=-1)
    sin = jnp.concatenate([jnp.sin(ang_r)] * 2 + [jnp.sin(ang_c)] * 2, axis=-1)
    return cos, sin


def _pad_rows(w, rows):
    return jnp.pad(w, [(0, 0)] * (w.ndim - 2) + [(0, rows - w.shape[-2]), (0, 0)])


def kernel(x_prompt, x_sample, cache_ckv, cache_krope, state_rwkv_fwd, state_rwkv_bwd, c, c_ctx, w_mod, b_mod, w_in, gmlp_ln_g, gmlp_ln_b, gmlp_ws, gmlp_bs, q_norm_g, w_uq, kv_norm_g, w_ukv, rwkv_mu, rwkv_w0, rwkv_w2, rwkv_a0, rwkv_a2, rwkv_g2, rwkv_kk, rwkv_ka, rwkv_rk, rwkv_lnx_g, rwkv_lnx_b, w_branch, w_o, ln1_g, ln1_b, router_w, router_b, w_gate, b_gate, w_up, b_up, w_down, b_down, ln2_g, ln2_b):
    L = w_mod.shape[0]
    b_ctx, seq, _ = x_prompt.shape
    b_lat, t_lat, _ = x_sample.shape
    past = cache_ckv.shape[2]
    n_ctx = b_ctx * seq
    n_lat = b_lat * t_lat
    n = n_ctx + n_lat
    assert seq % TB == 0 and t_lat % TB == 0 and past % TB == 0 and b_lat + 1 <= 8
    nb_ctx = n_ctx // TB
    bps_lat = t_lat // TB

    def modrow(i):
        return jnp.where(i < nb_ctx, 0, 1 + (i - nb_ctx) // bps_lat)

    cond8 = jnp.zeros((8, D), F32).at[0].set(c_ctx).at[1:1 + b_lat].set(c)
    mod_all = modulation(cond8, w_mod, b_mod).reshape(L, 8, 6, D)

    gm_end = 2 * D
    mla_end = gm_end + Q_LORA + KV_LORA + ROPE
    rw_end = mla_end + 3 * D + 2 * DECAY_LORA + 2 * AAA_LORA + GATE_LORA
    w_gm = w_in[:, :, :gm_end].astype(BF16)
    w_kr = w_in[:, :, mla_end - ROPE:mla_end]
    w_mla = jnp.concatenate([w_in[:, :, gm_end:mla_end], _rot_cols(w_kr)], axis=-1).astype(BF16)
    rw = w_in[:, :, mla_end:rw_end]
    segs = [rw[:, :, :3 * D]]
    mu_segs = [rwkv_mu[:, :3 * D]]
    o = 3 * D
    for width in (DECAY_LORA, DECAY_LORA, AAA_LORA, AAA_LORA):
        segs.append(jnp.pad(rw[:, :, o:o + width], ((0, 0), (0, 0), (0, LORA_PAD - width))))
        mu_segs.append(jnp.pad(rwkv_mu[:, o:o + width], ((0, 0), (0, LORA_PAD - width))))
        o += width
    segs.append(rw[:, :, o:])
    mu_segs.append(rwkv_mu[:, o:])
    w_rw = jnp.concatenate(segs, axis=-1).astype(BF16)
    mu_rw = jnp.concatenate(mu_segs, axis=-1)[:, None, :]
    w_gt = w_in[:, :, rw_end:].astype(BF16)
    w2p = _pad_rows(rwkv_w2, LORA_PAD).astype(BF16)
    a2p = _pad_rows(rwkv_a2, LORA_PAD).astype(BF16)
    g2b = rwkv_g2.astype(BF16)
    wq4 = w_uq.reshape(L, Q_LORA, HEADS, QK_DIM)
    wq_r = wq4[..., NOPE:]
    wq = jnp.concatenate([wq4[..., :NOPE], wq_r, _rot_cols(wq_r)], axis=-1)
    wq = wq.reshape(L, Q_LORA, HEADS * 2 * LANES).astype(BF16)
    wkv4 = w_ukv.reshape(L, KV_LORA, HEADS, NOPE + V_DIM)
    wk = wkv4[..., :NOPE].reshape(L, KV_LORA, D).astype(BF16)
    wv = wkv4[..., NOPE:].reshape(L, KV_LORA, D).astype(BF16)
    ws_b = gmlp_ws.astype(BF16)
    bs_full = jnp.repeat(jnp.swapaxes(gmlp_bs, 1, 2), GMLP_GW, axis=2)
    w_br = w_branch.astype(BF16)
    w_ob = w_o.astype(BF16)
    rw_pad = jnp.pad(router_w, ((0, 0), (0, 0), (0, LANES - N_EXPERTS)))
    rb_pad = jnp.pad(router_b, ((0, 0), (0, LANES - N_EXPERTS)), constant_values=-1e30)[:, None, :]
    bg4 = b_gate[:, :, None, :]
    bu4 = b_up[:, :, None, :]
    bd4 = b_down[:, :, None, :]

    cos, sin = _rope_tables(t_lat)
    cos_all = jnp.concatenate([jnp.ones((n_ctx, ROPE), F32), jnp.tile(cos, (b_lat, 1))], axis=0)
    sin_all = jnp.concatenate([jnp.zeros((n_ctx, ROPE), F32), jnp.tile(sin, (b_lat, 1))], axis=0)
    tabk = jnp.concatenate([cos_all, sin_all], axis=-1)
    tabq = jnp.concatenate([jnp.ones((n, NOPE), F32), tabk], axis=-1)
    tab_cache = jnp.concatenate([jnp.ones((b_lat * past, ROPE), F32), jnp.zeros((b_lat * past, ROPE), F32)], -1)

    zeros_state = jnp.zeros((b_ctx, RHEADS, RH, RH), F32)
    sf_lat = state_rwkv_fwd
    sb_lat = state_rwkv_bwd

    cap = (n * TOP_K // BM + N_EXPERTS) * BM
    x = jnp.concatenate([x_prompt.reshape(n_ctx, D), x_sample.reshape(n_lat, D)], axis=0)
    h = ln_modulate(x, mod_all[0], modrow)
    ckv_out, kr_out, sf_out, sb_out = [], [], [], []
    for l in range(L):
        mod = mod_all[l]
        p_gm = matmul(h, w_gm, l, 1024, "in_proj_gmlp")
        p_mla = matmul(h, w_mla, l, MLA_COLS, "in_proj_mla")
        p_rw = matmul(h, w_rw, l, 768, "in_proj_rwkv")
        p_gate = matmul(h, w_gt, l, 1024, "in_proj_gate")

        o_a = gmlp(p_gm, gmlp_ln_g[l][None], gmlp_ln_b[l][None], ws_b[l], bs_full[l])

        q = q_proj(p_mla, q_norm_g[l][None], wq[l], tabq)
        kcat, vv, ckvn = kv_proj(p_mla, 1, p_mla, (Q_LORA + KV_LORA) // LANES, kv_norm_g[l][None],
                                 wk[l], wv[l], tabk, True)
        cache_c = cache_ckv[:, l].reshape(b_lat * past, KV_LORA)
        cache_r = jnp.pad(cache_krope[:, l].reshape(b_lat * past, ROPE), ((0, 0), (0, ROPE)))
        kcat_c, vv_c, _ = kv_proj(cache_c, 0, cache_r, 0, kv_norm_g[l][None], wk[l], wv[l], tab_cache, False)
        hw2 = HEADS * 2 * LANES
        kcat_lat = jnp.concatenate([kcat_c.reshape(b_lat, past, hw2), kcat[n_ctx:].reshape(b_lat, t_lat, hw2)],
                                   axis=1).reshape(b_lat * (past + t_lat), hw2)
        vv_lat = jnp.concatenate([vv_c.reshape(b_lat, past, D), vv[n_ctx:].reshape(b_lat, t_lat, D)],
                                 axis=1).reshape(b_lat * (past + t_lat), D)
        o_b_ctx = attention(q, 0, kcat, vv, b_ctx, seq, seq, HEADS, "mla_attention_ctx")
        o_b_lat = attention(q, n_ctx, kcat_lat, vv_lat, b_lat, t_lat, past + t_lat, 2, "mla_attention_lat")
        o_b = jnp.concatenate([o_b_ctx, o_b_lat], axis=0)
        ckv_out.append(ckvn[:n_ctx].reshape(b_ctx, seq, KV_LORA))
        kr_out.append(p_mla[:n_ctx, Q_LORA + KV_LORA:Q_LORA + KV_LORA + ROPE].reshape(b_ctx, seq, ROPE))

        (r, v, kk, lwf, lwb, kdf, kdb, bqf, bqb, bonus, gg) = rwkv_prep(
            p_rw, mu_rw[l], rwkv_kk[l][None], rwkv_w0[l], w2p[l], rwkv_a0[l], a2p[l], rwkv_ka[l],
            rwkv_rk[l].reshape(1, D), g2b[l], n_ctx // RB, seq // RB, t_lat // RB)
        yf_c, yb_c, sf_c, sb_c = rwkv_scan(r, v, kk, lwf, lwb, kdf, kdb, bqf, bqb, zeros_state, zeros_state,
                                           0, b_ctx, seq, "rwkv_scan_ctx")
        yf_l, yb_l, _, _ = rwkv_scan(r, v, kk, lwf, lwb, kdf, kdb, bqf, bqb, sf_lat[:, l], sb_lat[:, l],
                                     n_ctx, b_lat, t_lat, "rwkv_scan_lat")
        sf_out.append(sf_c)
        sb_out.append(sb_c)
        o_c = rwkv_finish(jnp.concatenate([yf_c, yf_l], 0), jnp.concatenate([yb_c, yb_l], 0), bonus, gg,
                          rwkv_lnx_g[l][None], rwkv_lnx_b[l][None])

        z = branch_merge(o_a, o_b, o_c, w_br, l, p_gate)
        x1, h2, h2p = out_proj(z, w_ob, l, x, mod, modrow, ln1_g[l][None], ln1_b[l][None])

        idx, wts, hist = router(h2, rw_pad[l], rb_pad[l])
        base, pad_end = moe_layout(hist)
        dest = moe_dest(idx, base)
        dest_flat = dest[:, :TOP_K].reshape(n // TB, 1, TB * TOP_K)
        xb = moe_dispatch(dest_flat, h2p, jnp.zeros((cap, D // 2), jnp.uint32))
        n_valid = (pad_end[N_EXPERTS - 1] // BM).astype(jnp.int32)
        blk_start = jnp.minimum(jnp.arange(cap // BM, dtype=jnp.int32), n_valid - 1) * BM
        block_e = jnp.sum((pad_end[None, :N_EXPERTS] <= blk_start[:, None]).astype(jnp.int32), axis=1)
        block_e = jnp.minimum(block_e, N_EXPERTS - 1)
        nv = n_valid.reshape(1)
        act = expert_up(block_e, nv, xb, w_gate, bg4, w_up, bu4, l)
        yb = expert_down(block_e, nv, act, w_down, bd4, l)
        mod_next = mod_all[min(l + 1, L - 1)]
        x, h = moe_combine(dest_flat, wts, x1, mod, mod_next, modrow, ln2_g[l][None], ln2_b[l][None], yb)

    y_prompt = x[:n_ctx].reshape(b_ctx, seq, D)
    y_sample = x[n_ctx:].reshape(b_lat, t_lat, D)
    return (y_prompt, y_sample, jnp.stack(ckv_out, axis=1), jnp.stack(kr_out, axis=1),
            jnp.stack(sf_out, axis=1), jnp.stack(sb_out, axis=1))
```

```python
import functools
import math

import jax
import jax.numpy as jnp
from jax import lax
from jax.experimental import pallas as pl
from jax.experimental.pallas import tpu as pltpu

F32 = jnp.float32
BF16 = jnp.bfloat16
HI = lax.Precision.HIGHEST

DEPTH_NORM = 4
GRID_W = 64
CHUNK = 128
D = 2048
GMLP_GW = 128
GMLP_GROUPS = D // GMLP_GW
V_DIM = 128
NOPE = 128
ROPE = 64
QK_DIM = NOPE + ROPE
HEADS = D // V_DIM
Q_LORA = 512
KV_LORA = 512
ROPE_THETA = 10000.0
RH = 64
RHEADS = D // RH
DECAY_LORA = 96
AAA_LORA = 96
GATE_LORA = 256
GN_EPS = 64e-5
N_EXPERTS = 32
TOP_K = 4
SWIGLU_LIMIT = 7.0
SWIGLU_ALPHA = 1.702
LN_EPS = 1e-5
RMS_EPS = 1e-6
ALPHA = (2 * DEPTH_NORM) ** 0.25

LANES = 128
VMEM_LIMIT = 48 * 1024 * 1024

TB = 256
RB = 128
SC = 64
HG = 32
BM = 512
EXPERT_TN = 1024
EXPERT_UP_TN = 1024
EXPERT_VMEM_LIMIT = 56 * 1024 * 1024
DMA_UNROLL = 8
LORA_PAD = 128
RW_COLS = 3 * D + 4 * LORA_PAD + GATE_LORA
MLA_COLS = Q_LORA + KV_LORA + 2 * ROPE


def _cp(*sem):
    return pltpu.CompilerParams(dimension_semantics=sem, vmem_limit_bytes=VMEM_LIMIT)


def _ln(x):
    mu = jnp.mean(x, -1, keepdims=True)
    xc = x - mu
    var = jnp.mean(xc * xc, -1, keepdims=True)
    return xc * lax.rsqrt(var + LN_EPS)


def _sigmoid(x):
    return 1.0 / (1.0 + jnp.exp(-x))


def _gelu_tanh(x):
    return 0.5 * x * (1.0 + jnp.tanh(math.sqrt(2.0 / math.pi) * (x + 0.044715 * (x * x * x))))


def _mod_kernel(c_ref, w_ref, b_ref, o_ref):
    c = c_ref[...]
    s = (c * _sigmoid(c)).astype(BF16)
    o_ref[...] = jnp.dot(s, w_ref[...].astype(BF16), preferred_element_type=F32) + b_ref[...]


def modulation(cond8, w_mod, b_mod):
    L = w_mod.shape[0]
    tn = 1024
    return pl.pallas_call(
        _mod_kernel,
        grid=(L, 6 * D // tn),
        in_specs=[pl.BlockSpec((8, D), lambda l, j: (0, 0)),
                  pl.BlockSpec((None, D, tn), lambda l, j: (l, 0, j)),
                  pl.BlockSpec((None, 1, tn), lambda l, j: (l, 0, j))],
        out_specs=pl.BlockSpec((None, 8, tn), lambda l, j: (l, 0, j)),
        out_shape=jax.ShapeDtypeStruct((L, 8, 6 * D), F32),
        compiler_params=_cp("arbitrary", "arbitrary"),
        name="modulation",
    )(cond8, w_mod, b_mod.reshape(L, 1, 6 * D))


def _lnmod_kernel(x_ref, mod_ref, h_ref):
    m = mod_ref[...]
    h_ref[...] = (_ln(x_ref[...]) * (1.0 + m[1:2]) + m[0:1]).astype(BF16)


def ln_modulate(x, mod, modrow):
    n = x.shape[0]
    return pl.pallas_call(
        _lnmod_kernel,
        grid=(n // TB,),
        in_specs=[pl.BlockSpec((TB, D), lambda i: (i, 0)),
                  pl.BlockSpec((None, 6, D), lambda i: (modrow(i), 0, 0))],
        out_specs=pl.BlockSpec((TB, D), lambda i: (i, 0)),
        out_shape=jax.ShapeDtypeStruct((n, D), BF16),
        compiler_params=_cp("arbitrary"),
        name="ln_modulate",
    )(x, mod)


def _mm_kernel(x_ref, w_ref, o_ref):
    o_ref[...] = jnp.dot(x_ref[...], w_ref[...], preferred_element_type=F32).astype(o_ref.dtype)


def matmul(x, w, layer, tn, name, out_dtype=F32):
    m, k = x.shape
    nn = w.shape[2]
    tm = 512 if m % 512 == 0 else TB
    return pl.pallas_call(
        _mm_kernel,
        grid=(nn // tn, m // tm),
        in_specs=[pl.BlockSpec((tm, k), lambda j, i: (i, 0)),
                  pl.BlockSpec((None, k, tn), lambda j, i: (layer, 0, j))],
        out_specs=pl.BlockSpec((tm, tn), lambda j, i: (i, j)),
        out_shape=jax.ShapeDtypeStruct((m, nn), out_dtype),
        compiler_params=_cp("arbitrary", "arbitrary"),
        name=name,
    )(x, w)


def _gmlp_kernel(p_ref, g_ref, b_ref, ws_ref, bs_ref, o_ref):
    u = _gelu_tanh(p_ref[:, :D].astype(F32))
    v = _gelu_tanh(p_ref[:, D:].astype(F32))
    v = (_ln(v) * g_ref[...] + b_ref[...]).astype(BF16)
    for g in range(GMLP_GROUPS):
        sl = slice(g * GMLP_GW, (g + 1) * GMLP_GW)
        mixed = jnp.dot(ws_ref[g], v[:, sl], preferred_element_type=F32) + bs_ref[:, sl]
        o_ref[:, sl] = (u[:, sl] * mixed).astype(BF16)


def gmlp(p_gm, ln_g, ln_b, ws, bs_full):
    n = p_gm.shape[0]
    return pl.pallas_call(
        _gmlp_kernel,
        grid=(n // CHUNK,),
        in_specs=[pl.BlockSpec((CHUNK, 2 * D), lambda i: (i, 0)),
                  pl.BlockSpec((1, D), lambda i: (0, 0)),
                  pl.BlockSpec((1, D), lambda i: (0, 0)),
                  pl.BlockSpec((GMLP_GROUPS, CHUNK, CHUNK), lambda i: (0, 0, 0)),
                  pl.BlockSpec((CHUNK, D), lambda i: (0, 0))],
        out_specs=pl.BlockSpec((CHUNK, D), lambda i: (i, 0)),
        out_shape=jax.ShapeDtypeStruct((n, D), BF16),
        compiler_params=_cp("arbitrary"),
        name="gmlp",
    )(p_gm, ln_g, ln_b, ws, bs_full)


def _qproj_kernel(cq_ref, g_ref, w_ref, tab_ref, q_ref):
    cq = cq_ref[...]
    n = (cq * lax.rsqrt(jnp.mean(cq * cq, -1, keepdims=True) + RMS_EPS) * g_ref[...]).astype(BF16)
    tab = tab_ref[...]
    hw = 2 * LANES
    for h in range(HEADS):
        sl = slice(h * hw, (h + 1) * hw)
        q_ref[:, sl] = (jnp.dot(n, w_ref[:, sl], preferred_element_type=F32) * tab).astype(BF16)


def q_proj(p_mla, g, wq, tabq):
    n = p_mla.shape[0]
    hw = 2 * LANES
    return pl.pallas_call(
        _qproj_kernel,
        grid=(n // TB,),
        in_specs=[pl.BlockSpec((TB, Q_LORA), lambda i: (i, 0)),
                  pl.BlockSpec((1, Q_LORA), lambda i: (0, 0)),
                  pl.BlockSpec((Q_LORA, HEADS * hw), lambda i: (0, 0)),
                  pl.BlockSpec((TB, hw), lambda i: (i, 0))],
        out_specs=pl.BlockSpec((TB, HEADS * hw), lambda i: (i, 0)),
        out_shape=jax.ShapeDtypeStruct((n, HEADS * hw), BF16),
        compiler_params=_cp("arbitrary"),
        name="mla_q_proj",
    )(p_mla, g, wq, tabq)


def _kvproj_kernel(ckv_ref, kr_ref, g_ref, wk_ref, wv_ref, tab_ref, kcat_ref, v_ref, ckvn_ref, *, normalize):
    ckv = ckv_ref[...]
    if normalize:
        ckv = ckv * lax.rsqrt(jnp.mean(ckv * ckv, -1, keepdims=True) + RMS_EPS) * g_ref[...]
    ckvn_ref[...] = ckv
    nb = ckv.astype(BF16)
    t = kr_ref[...] * tab_ref[...]
    khi = (t + pltpu.roll(t, ROPE, axis=1)).astype(BF16)
    kn = jnp.dot(nb, wk_ref[...], preferred_element_type=F32).astype(BF16)
    v_ref[...] = jnp.dot(nb, wv_ref[...], preferred_element_type=F32).astype(BF16)
    hw = 2 * LANES
    for h in range(HEADS):
        kcat_ref[:, h * hw:h * hw + LANES] = kn[:, h * LANES:(h + 1) * LANES]
        kcat_ref[:, h * hw + LANES:(h + 1) * hw] = khi


def kv_proj(ckv_src, ckv_col, kr_src, kr_col, g, wk, wv, tabk, normalize):
    n = ckv_src.shape[0]
    hw = 2 * LANES
    return pl.pallas_call(
        functools.partial(_kvproj_kernel, normalize=normalize),
        grid=(n // TB,),
        in_specs=[pl.BlockSpec((TB, KV_LORA), lambda i: (i, ckv_col)),
                  pl.BlockSpec((TB, LANES), lambda i: (i, kr_col)),
                  pl.BlockSpec((1, KV_LORA), lambda i: (0, 0)),
                  pl.BlockSpec((KV_LORA, D), lambda i: (0, 0)),
                  pl.BlockSpec((KV_LORA, D), lambda i: (0, 0)),
                  pl.BlockSpec((TB, LANES), lambda i: (i, 0))],
        out_specs=[pl.BlockSpec((TB, HEADS * hw), lambda i: (i, 0)),
                   pl.BlockSpec((TB, D), lambda i: (i, 0)),
                   pl.BlockSpec((TB, KV_LORA), lambda i: (i, 0))],
        out_shape=[jax.ShapeDtypeStruct((n, HEADS * hw), BF16),
                   jax.ShapeDtypeStruct((n, D), BF16),
                   jax.ShapeDtypeStruct((n, KV_LORA), F32)],
        compiler_params=_cp("arbitrary"),
        name="mla_kv_proj",
    )(ckv_src, kr_src, g, wk, wv, tabk)


def _attn_kernel(q_ref, k_ref, v_ref, o_ref, *, nh):
    hw = 2 * LANES
    for h in range(nh):
        s = lax.dot_general(q_ref[:, h * hw:(h + 1) * hw], k_ref[:, h * hw:(h + 1) * hw],
                            (((1,), (1,)), ((), ())), preferred_element_type=F32)
        s = s * (QK_DIM ** -0.5)
        e = jnp.exp(s - jnp.max(s, -1, keepdims=True))
        p = (e * (1.0 / jnp.sum(e, -1, keepdims=True))).astype(BF16)
        o_ref[:, h * V_DIM:(h + 1) * V_DIM] = jnp.dot(
            p, v_ref[:, h * V_DIM:(h + 1) * V_DIM], preferred_element_type=F32).astype(BF16)


def attention(q, q_row0, kcat, v, n_seq, t_q, n_keys, nh, name):
    hw = 2 * LANES
    qb = t_q // TB
    qoff = q_row0 // TB
    return pl.pallas_call(
        functools.partial(_attn_kernel, nh=nh),
        grid=(n_seq, HEADS // nh, qb),
        in_specs=[pl.BlockSpec((TB, nh * hw), lambda s, h, i: (qoff + s * qb + i, h)),
                  pl.BlockSpec((n_keys, nh * hw), lambda s, h, i: (s, h)),
                  pl.BlockSpec((n_keys, nh * V_DIM), lambda s, h, i: (s, h))],
        out_specs=pl.BlockSpec((TB, nh * V_DIM), lambda s, h, i: (s * qb + i, h)),
        out_shape=jax.ShapeDtypeStruct((n_seq * t_q, D), BF16),
        compiler_params=_cp("arbitrary", "arbitrary", "arbitrary"),
        name=name,
    )(q, kcat, v)


def _head_ones():
    r = lax.broadcasted_iota(jnp.int32, (LANES, LANES), 0) >= RH
    c = lax.broadcasted_iota(jnp.int32, (LANES, LANES), 1) >= RH
    return (r == c).astype(BF16)


def _head_sum(x, ones_bd):
    hi, lo = _split_bf16(x)
    parts = [jnp.dot(hi[:, i * LANES:(i + 1) * LANES], ones_bd, preferred_element_type=F32)
             + jnp.dot(lo[:, i * LANES:(i + 1) * LANES], ones_bd, preferred_element_type=F32)
             for i in range(x.shape[1] // LANES)]
    return jnp.concatenate(parts, axis=1)


def _rwkv_prep_kernel(p_ref, prev_ref, next_ref, mu_ref, kkw_ref, w0_ref, w2_ref, a0_ref, a2_ref, ka_ref,
                      rk_ref, g2_ref,
                      r_o, v_o, kk_o, lwf_o, lwb_o, kdf_o, kdb_o, bqf_o, bqb_o, bonus_o, g_o,
                      *, nb_ctx, bps_ctx, bps_lat):
    i = pl.program_id(0)
    is_lat = i >= nb_ctx
    pos = jnp.where(is_lat, lax.rem(i - nb_ctx, bps_lat), lax.rem(i, bps_ctx))
    last = jnp.where(is_lat, bps_lat - 1, bps_ctx - 1)
    has_prev = (pos != 0).astype(F32)
    has_next = (pos != last).astype(F32)
    rows = lax.broadcasted_iota(jnp.int32, (RB, 1), 0)

    def shifted(lo, hi):
        p = p_ref[:, lo:hi]
        prev = jnp.where(rows == 0, prev_ref[7:8, lo:hi] * has_prev, pltpu.roll(p, 1, axis=0))
        nxt = jnp.where(rows == RB - 1, next_ref[0:1, lo:hi] * has_next, pltpu.roll(p, RB - 1, axis=0))
        return p + (0.5 * (prev + nxt) - p) * mu_ref[:, lo:hi]

    ones_bd = _head_ones()
    r = shifted(0, D)
    k = shifted(D, 2 * D)
    v = shifted(2 * D, 3 * D)
    r_o[...] = r
    v_o[...] = v
    kkf = k * kkw_ref[...]
    nrm = jnp.sqrt(_head_sum(kkf * kkf, ones_bd))
    kk = kkf / jnp.maximum(nrm, 1e-12)
    kk_o[...] = kk
    bonus_o[...] = _head_sum(r * k * rk_ref[...], ones_bd) * v
    base = 3 * D
    outs = ((lwf_o, kdf_o, bqf_o), (lwb_o, kdb_o, bqb_o))
    for d in range(2):
        wl = shifted(base + d * LORA_PAD, base + (d + 1) * LORA_PAD)
        z = w0_ref[d:d + 1, :] + jnp.dot(jnp.tanh(wl).astype(BF16), w2_ref[d], preferred_element_type=F32)
        y = -z
        softplus = jnp.maximum(y, 0.0) + jnp.log(1.0 + jnp.exp(-jnp.abs(y)))
        logw = -softplus - 0.5
        outs[d][0][...] = -jnp.exp(logw)
        al = shifted(base + (2 + d) * LORA_PAD, base + (3 + d) * LORA_PAD)
        a = _sigmoid(a0_ref[d:d + 1, :] + jnp.dot(al.astype(BF16), a2_ref[d], preferred_element_type=F32))
        outs[d][1][...] = k * (1.0 + (a - 1.0) * ka_ref[d:d + 1, :])
        outs[d][2][...] = kk * a
    gl = shifted(base + 4 * LORA_PAD, base + 4 * LORA_PAD + GATE_LORA)
    g_o[...] = jnp.dot(_sigmoid(gl).astype(BF16), g2_ref[...], preferred_element_type=F32)


def rwkv_prep(p_rw, mu, kkw, w0, w2, a0, a2, ka, rk, g2, nb_ctx, bps_ctx, bps_lat):
    n = p_rw.shape[0]
    nb = n // RB
    sub = RB // 8
    row = lambda i: (i, 0)
    const2 = lambda i: (0, 0)
    out = jax.ShapeDtypeStruct((n, D), F32)
    return pl.pallas_call(
        functools.partial(_rwkv_prep_kernel, nb_ctx=nb_ctx, bps_ctx=bps_ctx, bps_lat=bps_lat),
        grid=(nb,),
        in_specs=[pl.BlockSpec((RB, RW_COLS), row),
                  pl.BlockSpec((8, RW_COLS), lambda i: (jnp.maximum(i * sub - 1, 0), 0)),
                  pl.BlockSpec((8, RW_COLS), lambda i: (jnp.minimum((i + 1) * sub, nb * sub - 1), 0)),
                  pl.BlockSpec((1, RW_COLS), const2),
                  pl.BlockSpec((1, D), const2),
                  pl.BlockSpec((2, D), const2),
                  pl.BlockSpec((2, LORA_PAD, D), lambda i: (0, 0, 0)),
                  pl.BlockSpec((2, D), const2),
                  pl.BlockSpec((2, LORA_PAD, D), lambda i: (0, 0, 0)),
                  pl.BlockSpec((2, D), const2),
                  pl.BlockSpec((1, D), const2),
                  pl.BlockSpec((GATE_LORA, D), const2)],
        out_specs=[pl.BlockSpec((RB, D), row)] * 11,
        out_shape=[out] * 11,
        compiler_params=_cp("arbitrary"),
        name="rwkv_prep",
    )(p_rw, p_rw, p_rw, mu, kkw, w0, w2, a0, a2, ka, rk, g2)


def _mm(eq, a, b):
    return jnp.einsum(eq, a.astype(BF16), b.astype(BF16), preferred_element_type=F32)


def _split_bf16(x):
    hi = x.astype(BF16)
    return hi, (x - hi.astype(F32)).astype(BF16)


def _mm3(eq, a, b):
    ah, al = _split_bf16(a)
    bh, bl = _split_bf16(b)
    mm = lambda x, y: jnp.einsum(eq, x, y, preferred_element_type=F32)
    return mm(ah, bh) + (mm(ah, bl) + mm(al, bh))


_NN = "hmk,hkn->hmn"
_NT = "hmk,hnk->hmn"
_TN = "hkm,hkn->hmn"


def _scan_sum(x, reverse):
    rows = lax.broadcasted_iota(jnp.int32, (SC, 1), 0)
    shift = 1
    while shift < SC:
        if reverse:
            x = x + jnp.where(rows < SC - shift, pltpu.roll(x, SC - shift, axis=0), 0.0)
        else:
            x = x + jnp.where(rows >= shift, pltpu.roll(x, shift, axis=0), 0.0)
        shift *= 2
    return x


def _wkv_chunk(refs, s, reverse):
    r, v, kk, lw, kd, bq = (x[...] for x in refs)
    last = 0 if reverse else SC - 1
    L = _scan_sum(lw, reverse)
    ltot = L[last:last + 1, :]
    en = jnp.exp(-L)
    et = jnp.exp(ltot - L)

    def heads(x):
        return jnp.stack([x[:, h * RH:(h + 1) * RH] for h in range(HG)], axis=0)

    ax = heads(-kk * jnp.exp(L - lw))
    rt = heads(r * jnp.exp(L))
    bt = heads(bq * en)
    kt = heads(kd * en)
    bc = heads(bq * et)
    kc = heads(kd * et)
    ptot = heads(jnp.exp(ltot))
    v = heads(v)

    ti = lax.broadcasted_iota(jnp.int32, (SC, SC), 0)
    si = lax.broadcasted_iota(jnp.int32, (SC, SC), 1)
    strict = (si > ti) if reverse else (si < ti)
    incl = (si >= ti) if reverse else (si <= ti)
    a_ab = jnp.where(strict, _mm3(_NT, ax, bt), 0.0)
    a_ak = jnp.where(strict, _mm3(_NT, ax, kt), 0.0)
    a_rb = jnp.where(incl, _mm(_NT, rt, bt), 0.0)
    a_rk = jnp.where(incl, _mm(_NT, rt, kt), 0.0)
    npow = a_ab
    e = a_ab
    for _ in range(int(math.log2(SC)) - 2):
        npow = _mm(_NN, npow, npow)
        e = e + npow + _mm(_NN, e, npow)
    ws = _mm(_NT, jnp.concatenate([ax, rt], axis=1), s)
    w = ws[:, :SC] + _mm(_NN, a_ak, v)
    u = w + _mm(_NN, e, w)
    rho = (w - u) + _mm3(_NN, a_ab, u)
    u = u + rho + _mm(_NN, e, rho)
    y = ws[:, SC:] + _mm(_NN, a_rb, u) + _mm(_NN, a_rk, v)
    s_new = s * ptot + _mm3(_TN, jnp.concatenate([u, v], axis=1), jnp.concatenate([bc, kc], axis=1))
    return jnp.concatenate([y[h] for h in range(HG)], axis=1), s_new


def _scan_kernel(rf, vf, kkf, lwf, kdf, bqf, rb, vb, kkb, lwb, kdb, bqb, s0f, s0b,
                 yf_ref, yb_ref, sfo, sbo, stf, stb, *, nc):
    c = pl.program_id(2)

    @pl.when(c == 0)
    def _():
        stf[...] = s0f[...]
        stb[...] = s0b[...]

    for reverse, refs, st, yref in ((False, (rf, vf, kkf, lwf, kdf, bqf), stf, yf_ref),
                                    (True, (rb, vb, kkb, lwb, kdb, bqb), stb, yb_ref)):
        y, s_new = _wkv_chunk(refs, st[...], reverse)
        st[...] = s_new
        yref[...] = y

    @pl.when(c == nc - 1)
    def _():
        sfo[...] = stf[...]
        sbo[...] = stb[...]


def rwkv_scan(r, v, kk, lwf, lwb, kdf, kdb, bqf, bqb, s0f, s0b, row0, n_seq, t_len, name):
    nc = t_len // SC
    off = row0 // SC
    cw = HG * RH
    fwd = lambda s, h, c: (off + s * nc + c, h)
    bwd = lambda s, h, c: (off + s * nc + (nc - 1 - c), h)
    blk = lambda im: pl.BlockSpec((SC, cw), im)
    st_spec = pl.BlockSpec((None, HG, RH, RH), lambda s, h, c: (s, h, 0, 0))
    n_rows = n_seq * t_len
    return pl.pallas_call(
        functools.partial(_scan_kernel, nc=nc),
        grid=(n_seq, RHEADS // HG, nc),
        in_specs=[blk(fwd)] * 6 + [blk(bwd)] * 6 + [st_spec, st_spec],
        out_specs=[pl.BlockSpec((SC, cw), lambda s, h, c: (s * nc + c, h)),
                   pl.BlockSpec((SC, cw), lambda s, h, c: (s * nc + (nc - 1 - c), h)),
                   st_spec, st_spec],
        out_shape=[jax.ShapeDtypeStruct((n_rows, D), F32), jax.ShapeDtypeStruct((n_rows, D), F32),
                   jax.ShapeDtypeStruct(s0f.shape, F32), jax.ShapeDtypeStruct(s0b.shape, F32)],
        scratch_shapes=[pltpu.VMEM((HG, RH, RH), F32), pltpu.VMEM((HG, RH, RH), F32)],
        compiler_params=_cp("arbitrary", "arbitrary", "arbitrary"),
        name=name,
    )(r, v, kk, lwf, kdf, bqf, r, v, kk, lwb, kdb, bqb, s0f, s0b)


def _rwkv_fin_kernel(yf_ref, yb_ref, bonus_ref, g_ref, lg_ref, lb_ref, o_ref):
    ones_bd = _head_ones()
    y = yf_ref[...] + yb_ref[...]
    mu = _head_sum(y, ones_bd) * (1.0 / RH)
    yc = y - mu
    var = _head_sum(yc * yc, ones_bd) * (1.0 / RH)
    yn = yc * lax.rsqrt(var + GN_EPS) * lg_ref[...] + lb_ref[...]
    o_ref[...] = ((yn + bonus_ref[...]) * g_ref[...]).astype(BF16)


def rwkv_finish(yf, yb, bonus, g, lnx_g, lnx_b):
    n = yf.shape[0]
    row = lambda i: (i, 0)
    const2 = lambda i: (0, 0)
    return pl.pallas_call(
        _rwkv_fin_kernel,
        grid=(n // TB,),
        in_specs=[pl.BlockSpec((TB, D), row)] * 4 + [pl.BlockSpec((1, D), const2)] * 2,
        out_specs=pl.BlockSpec((TB, D), row),
        out_shape=jax.ShapeDtypeStruct((n, D), BF16),
        compiler_params=_cp("arbitrary"),
        name="rwkv_finish",
    )(yf, yb, bonus, g, lnx_g, lnx_b)


def _merge_kernel(oa_ref, ob_ref, oc_ref, w_ref, ga_ref, gb_ref, gc_ref, z_ref):
    acc = None
    for n, (o_ref, g_ref) in enumerate(((oa_ref, ga_ref), (ob_ref, gb_ref), (oc_ref, gc_ref))):
        y = _sigmoid(g_ref[...].astype(F32)) * jnp.dot(o_ref[...], w_ref[n], preferred_element_type=F32)
        acc = y if acc is None else acc + y
    z_ref[...] = acc.astype(BF16)


def branch_merge(o_a, o_b, o_c, w_branch, layer, p_gate):
    n = o_a.shape[0]
    tn = 512
    tm = 512 if n % 512 == 0 else TB
    nj = D // tn
    o_spec = pl.BlockSpec((tm, D), lambda j, i: (i, 0))
    gate = lambda b: pl.BlockSpec((tm, tn), lambda j, i: (i, b * nj + j))
    return pl.pallas_call(
        _merge_kernel,
        grid=(nj, n // tm),
        in_specs=[o_spec, o_spec, o_spec,
                  pl.BlockSpec((None, 3, D, tn), lambda j, i: (layer, 0, 0, j)),
                  gate(0), gate(1), gate(2)],
        out_specs=pl.BlockSpec((tm, tn), lambda j, i: (i, j)),
        out_shape=jax.ShapeDtypeStruct((n, D), BF16),
        compiler_params=_cp("arbitrary", "arbitrary"),
        name="branch_merge",
    )(o_a, o_b, o_c, w_branch, p_gate, p_gate, p_gate)


def _bf16_bits(x):
    return lax.bitcast_convert_type(x.astype(BF16).astype(F32), jnp.uint32)


def _pack_rows(x):
    return (_bf16_bits(x[:, :D // 2]) >> 16) | _bf16_bits(x[:, D // 2:])


def _unpack_rows(p):
    lo = lax.bitcast_convert_type(p << 16, F32)
    hi = lax.bitcast_convert_type(p & jnp.uint32(0xFFFF0000), F32)
    return jnp.concatenate([lo, hi], axis=1).astype(BF16)


def _oproj_kernel(z_ref, wo_ref, x_ref, mod_ref, g_ref, b_ref, x1_ref, h2_ref, h2p_ref):
    mix = jnp.dot(z_ref[...], wo_ref[...], preferred_element_type=F32)
    m = mod_ref[...]
    x1 = _ln(ALPHA * x_ref[...] + m[2:3] * mix) * g_ref[...] + b_ref[...]
    x1_ref[...] = x1
    h2 = _ln(x1) * (1.0 + m[4:5]) + m[3:4]
    h2_ref[...] = h2
    h2p_ref[...] = _pack_rows(h2)


def out_proj(z, w_o, layer, x, mod, modrow, ln_g, ln_b):
    n = z.shape[0]
    row = lambda i: (i, 0)
    const2 = lambda i: (0, 0)
    return pl.pallas_call(
        _oproj_kernel,
        grid=(n // TB,),
        in_specs=[pl.BlockSpec((TB, D), row),
                  pl.BlockSpec((None, D, D), lambda i: (layer, 0, 0)),
                  pl.BlockSpec((TB, D), row),
                  pl.BlockSpec((None, 6, D), lambda i: (modrow(i), 0, 0)),
                  pl.BlockSpec((1, D), const2),
                  pl.BlockSpec((1, D), const2)],
        out_specs=[pl.BlockSpec((TB, D), row), pl.BlockSpec((TB, D), row), pl.BlockSpec((TB, D // 2), row)],
        out_shape=[jax.ShapeDtypeStruct((n, D), F32), jax.ShapeDtypeStruct((n, D), F32),
                   jax.ShapeDtypeStruct((n, D // 2), jnp.uint32)],
        compiler_params=_cp("arbitrary"),
        name="out_proj_ln",
    )(z, w_o, x, mod, ln_g, ln_b)


def _router_kernel(h_ref, w_ref, b_ref, idx_ref, wts_ref, hist_ref):
    logits = jnp.dot(h_ref[...], w_ref[...], precision=HI, preferred_element_type=F32) + b_ref[...]
    lane = lax.broadcasted_iota(jnp.int32, (TB, LANES), 1).astype(F32)
    cur = logits
    idx_out = jnp.zeros((TB, LANES), F32)
    val_out = jnp.zeros((TB, LANES), F32)
    hist = jnp.zeros((TB, LANES), F32)
    top = None
    for k in range(TOP_K):
        m = jnp.max(cur, -1, keepdims=True)
        am = jnp.min(jnp.where(cur == m, lane, float(LANES)), -1, keepdims=True)
        sel = lane == am
        top = m if top is None else top
        idx_out = jnp.where(lane == k, am, idx_out)
        val_out = jnp.where(lane == k, jnp.exp(m - top), val_out)
        hist = hist + sel.astype(F32)
        cur = jnp.where(sel, -jnp.inf, cur)
    idx_ref[...] = idx_out.astype(jnp.int32)
    wts_ref[...] = val_out * (1.0 / jnp.sum(val_out, -1, keepdims=True))
    hist_ref[...] = jnp.sum(hist, 0, keepdims=True)


def router(h2, rw_pad, rb_pad):
    n = h2.shape[0]
    nb = n // TB
    return pl.pallas_call(
        _router_kernel,
        grid=(nb,),
        in_specs=[pl.BlockSpec((TB, D), lambda i: (i, 0)),
                  pl.BlockSpec((D, LANES), lambda i: (0, 0)),
                  pl.BlockSpec((1, LANES), lambda i: (0, 0))],
        out_specs=[pl.BlockSpec((TB, LANES), lambda i: (i, 0)),
                   pl.BlockSpec((TB, LANES), lambda i: (i, 0)),
                   pl.BlockSpec((None, 1, LANES), lambda i: (i, 0, 0))],
        out_shape=[jax.ShapeDtypeStruct((n, LANES), jnp.int32),
                   jax.ShapeDtypeStruct((n, LANES), F32),
                   jax.ShapeDtypeStruct((nb, 1, LANES), F32)],
        compiler_params=_cp("arbitrary"),
        name="moe_router",
    )(h2, rw_pad, rb_pad)


def _dest_kernel(idx_ref, base_ref, dest_ref):
    lane = lax.broadcasted_iota(jnp.int32, (TB, LANES), 1).astype(F32)
    ri = lax.broadcasted_iota(jnp.int32, (TB, TB), 0)
    ci = lax.broadcasted_iota(jnp.int32, (TB, TB), 1)
    lower = (ci < ri).astype(BF16)
    idx = idx_ref[...].astype(F32)
    run = base_ref[...]
    dest = jnp.zeros((TB, LANES), F32)
    for k in range(TOP_K):
        e_k = jnp.sum(jnp.where(lane == k, idx, 0), -1, keepdims=True)
        onehot = lane == e_k
        oh = onehot.astype(F32)
        before = jnp.dot(lower, oh.astype(BF16), preferred_element_type=F32)
        d_k = jnp.sum(jnp.where(onehot, run + before, 0.0), -1, keepdims=True)
        dest = jnp.where(lane == k, d_k, dest)
        run = run + jnp.sum(oh, 0, keepdims=True)
    dest_ref[...] = dest.astype(jnp.int32)


def moe_dest(idx, base):
    n = idx.shape[0]
    nb = n // TB
    return pl.pallas_call(
        _dest_kernel,
        grid=(nb,),
        in_specs=[pl.BlockSpec((TB, LANES), lambda i: (i, 0)),
                  pl.BlockSpec((None, 1, LANES), lambda i: (i, 0, 0))],
        out_specs=pl.BlockSpec((TB, LANES), lambda i: (i, 0)),
        out_shape=jax.ShapeDtypeStruct((n, LANES), jnp.int32),
        compiler_params=_cp("arbitrary"),
        name="moe_dest",
    )(idx, base)


def _row_copy(src, src_row, dst, dst_row, sem):
    return pltpu.make_async_copy(src.at[pl.ds(src_row, 1)], dst.at[pl.ds(dst_row, 1)], sem)


def _dispatch_kernel(dest_ref, h_ref, xb_in, xb_out, sem):
    del xb_in

    def issue(t, carry):
        for k in range(TOP_K):
            _row_copy(h_ref, t, xb_out, dest_ref[0, t * TOP_K + k], sem).start()
        return carry

    lax.fori_loop(0, TB, issue, 0, unroll=DMA_UNROLL)

    def drain(t, carry):
        for k in range(TOP_K):
            _row_copy(h_ref, t, xb_out, dest_ref[0, t * TOP_K + k], sem).wait()
        return carry

    lax.fori_loop(0, TB, drain, 0, unroll=DMA_UNROLL)


def moe_dispatch(dest_flat, h2, xb_init):
    n = h2.shape[0]
    nb = n // TB
    return pl.pallas_call(
        _dispatch_kernel,
        grid=(nb,),
        in_specs=[pl.BlockSpec((None, 1, TB * TOP_K), lambda i: (i, 0, 0), memory_space=pltpu.SMEM),
                  pl.BlockSpec((TB, D // 2), lambda i: (i, 0)),
                  pl.BlockSpec(memory_space=pl.ANY)],
        out_specs=pl.BlockSpec(memory_space=pl.ANY),
        out_shape=jax.ShapeDtypeStruct(xb_init.shape, xb_init.dtype),
        scratch_shapes=[pltpu.SemaphoreType.DMA(())],
        input_output_aliases={2: 0},
        compiler_params=_cp("arbitrary"),
        name="moe_dispatch",
    )(dest_flat, h2, xb_init)


def _expert_changed(be_ref):
    i = pl.program_id(1)
    return jnp.logical_or(i == 0, be_ref[i] != be_ref[jnp.maximum(i - 1, 0)])


def _expert_up_kernel(be_ref, nv_ref, x_ref, wg_ref, bg_ref, wu_ref, bu_ref, act_ref, wg_bf, wu_bf):
    used = pl.program_id(1) < nv_ref[0]

    @pl.when(jnp.logical_not(used))
    def _():
        act_ref[...] = jnp.zeros_like(act_ref)

    @pl.when(jnp.logical_and(used, _expert_changed(be_ref)))
    def _():
        wg_bf[...] = wg_ref[...].astype(BF16)
        wu_bf[...] = wu_ref[...].astype(BF16)

    @pl.when(used)
    def _():
        half = BM // 2
        for r in range(2):
            rows = slice(r * half, (r + 1) * half)
            xb = _unpack_rows(x_ref[rows, :])
            gl = jnp.minimum(jnp.dot(xb, wg_bf[...], preferred_element_type=F32) + bg_ref[...], SWIGLU_LIMIT)
            lin = jnp.clip(jnp.dot(xb, wu_bf[...], preferred_element_type=F32) + bu_ref[...],
                           -SWIGLU_LIMIT, SWIGLU_LIMIT)
            act_ref[rows, :] = (gl * _sigmoid(SWIGLU_ALPHA * gl) * (lin + 1.0)).astype(BF16)


def expert_up(block_e, n_valid, xb, w_gate, b_gate, w_up, b_up, layer):
    cap = xb.shape[0]
    tn = EXPERT_UP_TN
    rowblk = lambda j, i, be, nv: (jnp.minimum(i, nv[0] - 1), 0)
    wspec = pl.BlockSpec((None, None, D, tn), lambda j, i, be, nv: (layer, be[i], 0, j))
    bspec = pl.BlockSpec((None, None, 1, tn), lambda j, i, be, nv: (layer, be[i], 0, j))
    return pl.pallas_call(
        _expert_up_kernel,
        grid_spec=pltpu.PrefetchScalarGridSpec(
            num_scalar_prefetch=2,
            grid=(D // tn, cap // BM),
            in_specs=[pl.BlockSpec((BM, D // 2), rowblk), wspec, bspec, wspec, bspec],
            out_specs=pl.BlockSpec((BM, tn), lambda j, i, be, nv: (i, j)),
            scratch_shapes=[pltpu.VMEM((D, tn), BF16), pltpu.VMEM((D, tn), BF16)]),
        out_shape=jax.ShapeDtypeStruct((cap, D), BF16),
        compiler_params=pltpu.CompilerParams(dimension_semantics=("arbitrary", "arbitrary"),
                                             vmem_limit_bytes=EXPERT_VMEM_LIMIT),
        name="moe_expert_up",
    )(block_e, n_valid, xb, w_gate, b_gate, w_up, b_up)


def _expert_down_kernel(be_ref, nv_ref, a_ref, wd_ref, bd_ref, y_ref, wd_bf):
    used = pl.program_id(1) < nv_ref[0]

    @pl.when(jnp.logical_not(used))
    def _():
        y_ref[...] = jnp.zeros_like(y_ref)

    @pl.when(jnp.logical_and(used, _expert_changed(be_ref)))
    def _():
        wd_bf[...] = wd_ref[...].astype(BF16)

    @pl.when(used)
    def _():
        y_ref[...] = jnp.dot(a_ref[...], wd_bf[...], preferred_element_type=F32) + bd_ref[...]


def expert_down(block_e, n_valid, act, w_down, b_down, layer):
    cap = act.shape[0]
    tn = EXPERT_TN
    return pl.pallas_call(
        _expert_down_kernel,
        grid_spec=pltpu.PrefetchScalarGridSpec(
            num_scalar_prefetch=2,
            grid=(D // tn, cap // BM),
            in_specs=[pl.BlockSpec((BM, D), lambda j, i, be, nv: (jnp.minimum(i, nv[0] - 1), 0)),
                      pl.BlockSpec((None, None, D, tn), lambda j, i, be, nv: (layer, be[i], 0, j)),
                      pl.BlockSpec((None, None, 1, tn), lambda j, i, be, nv: (layer, be[i], 0, j))],
            out_specs=pl.BlockSpec((BM, tn), lambda j, i, be, nv: (i, j)),
            scratch_shapes=[pltpu.VMEM((D, tn), BF16)]),
        out_shape=jax.ShapeDtypeStruct((cap, D), F32),
        compiler_params=pltpu.CompilerParams(dimension_semantics=("arbitrary", "arbitrary"),
                                             vmem_limit_bytes=EXPERT_VMEM_LIMIT),
        name="moe_expert_down",
    )(block_e, n_valid, act, w_down, b_down)


def _combine_kernel(dest_ref, wts_ref, x1_ref, mod_ref, g_ref, b_ref, modn_ref, yb_ref,
                    x2_ref, hn_ref, rows, sem):
    def issue(t, carry):
        for k in range(TOP_K):
            _row_copy(yb_ref, dest_ref[0, t * TOP_K + k], rows.at[k], t, sem).start()
        return carry

    lax.fori_loop(0, TB, issue, 0, unroll=DMA_UNROLL)

    def drain(t, carry):
        for k in range(TOP_K):
            _row_copy(yb_ref, dest_ref[0, t * TOP_K + k], rows.at[k], t, sem).wait()
        return carry

    lax.fori_loop(0, TB, drain, 0, unroll=DMA_UNROLL)
    w = wts_ref[...]
    ffn = rows[0] * w[:, 0:1]
    for k in range(1, TOP_K):
        ffn = ffn + rows[k] * w[:, k:k + 1]
    m = mod_ref[...]
    x2 = _ln(ALPHA * x1_ref[...] + m[5:6] * ffn) * g_ref[...] + b_ref[...]
    x2_ref[...] = x2
    mn = modn_ref[...]
    hn_ref[...] = (_ln(x2) * (1.0 + mn[1:2]) + mn[0:1]).astype(BF16)


def moe_combine(dest_flat, wts, x1, mod, mod_next, modrow, ln_g, ln_b, yb):
    n = x1.shape[0]
    row = lambda i: (i, 0)
    const2 = lambda i: (0, 0)
    modspec = pl.BlockSpec((None, 6, D), lambda i: (modrow(i), 0, 0))
    return pl.pallas_call(
        _combine_kernel,
        grid=(n // TB,),
        in_specs=[pl.BlockSpec((None, 1, TB * TOP_K), lambda i: (i, 0, 0), memory_space=pltpu.SMEM),
                  pl.BlockSpec((TB, LANES), row),
                  pl.BlockSpec((TB, D), row),
                  modspec,
                  pl.BlockSpec((1, D), const2),
                  pl.BlockSpec((1, D), const2),
                  modspec,
                  pl.BlockSpec(memory_space=pl.ANY)],
        out_specs=[pl.BlockSpec((TB, D), row), pl.BlockSpec((TB, D), row)],
        out_shape=[jax.ShapeDtypeStruct((n, D), F32), jax.ShapeDtypeStruct((n, D), BF16)],
        scratch_shapes=[pltpu.VMEM((TOP_K, TB, D), F32), pltpu.SemaphoreType.DMA(())],
        compiler_params=_cp("arbitrary"),
        name="moe_combine",
    )(dest_flat, wts, x1, mod, ln_g, ln_b, mod_next, yb)


def moe_layout(hist):
    h = hist[:, 0, :].astype(jnp.int32)
    counts = jnp.sum(h, 0)
    padded = (counts + BM - 1) // BM * BM
    pad_end = jnp.cumsum(padded)
    pad_start = pad_end - padded
    before = jnp.cumsum(h, 0) - h
    base = (pad_start[None, :] + before).astype(F32)[:, None, :]
    return base, pad_end


def _rot_cols(w):
    q = ROPE // 4
    return jnp.concatenate([-w[..., q:2 * q], w[..., 0:q], -w[..., 3 * q:4 * q], w[..., 2 * q:3 * q]], axis=-1)


def _rope_tables(n):
    rows = n // GRID_W
    row_id = jnp.repeat(jnp.arange(rows, dtype=F32), GRID_W)
    col_id = jnp.tile(jnp.arange(GRID_W, dtype=F32), rows)
    half = ROPE // 2
    inv_freq = ROPE_THETA ** (-jnp.arange(0, half, 2, dtype=F32) / half)
    ang_r = row_id[:, None] * inv_freq
    ang_c = col_id[:, None] * inv_freq
    cos = jnp.concatenate([jnp.cos(ang_r)] * 2 + [jnp.cos(ang_c)] * 2, axis=-1)
    sin = jnp.concatenate([jnp.sin(ang_r)] * 2 + [jnp.sin(ang_c)] * 2, axis=-1)
    return cos, sin


def _pad_rows(w, rows):
    return jnp.pad(w, [(0, 0)] * (w.ndim - 2) + [(0, rows - w.shape[-2]), (0, 0)])


def kernel(x_prompt, x_sample, cache_ckv, cache_krope, state_rwkv_fwd, state_rwkv_bwd, c, c_ctx, w_mod, b_mod, w_in, gmlp_ln_g, gmlp_ln_b, gmlp_ws, gmlp_bs, q_norm_g, w_uq, kv_norm_g, w_ukv, rwkv_mu, rwkv_w0, rwkv_w2, rwkv_a0, rwkv_a2, rwkv_g2, rwkv_kk, rwkv_ka, rwkv_rk, rwkv_lnx_g, rwkv_lnx_b, w_branch, w_o, ln1_g, ln1_b, router_w, router_b, w_gate, b_gate, w_up, b_up, w_down, b_down, ln2_g, ln2_b):
    L = w_mod.shape[0]
    b_ctx, seq, _ = x_prompt.shape
    b_lat, t_lat, _ = x_sample.shape
    past = cache_ckv.shape[2]
    n_ctx = b_ctx * seq
    n_lat = b_lat * t_lat
    n = n_ctx + n_lat
    assert seq % TB == 0 and t_lat % TB == 0 and past % TB == 0 and b_lat + 1 <= 8
    nb_ctx = n_ctx // TB
    bps_lat = t_lat // TB

    def modrow(i):
        return jnp.where(i < nb_ctx, 0, 1 + (i - nb_ctx) // bps_lat)

    cond8 = jnp.zeros((8, D), F32).at[0].set(c_ctx).at[1:1 + b_lat].set(c)
    mod_all = modulation(cond8, w_mod, b_mod).reshape(L, 8, 6, D)

    gm_end = 2 * D
    mla_end = gm_end + Q_LORA + KV_LORA + ROPE
    rw_end = mla_end + 3 * D + 2 * DECAY_LORA + 2 * AAA_LORA + GATE_LORA
    w_gm = w_in[:, :, :gm_end].astype(BF16)
    w_kr = w_in[:, :, mla_end - ROPE:mla_end]
    w_mla = jnp.concatenate([w_in[:, :, gm_end:mla_end], _rot_cols(w_kr)], axis=-1).astype(BF16)
    rw = w_in[:, :, mla_end:rw_end]
    segs = [rw[:, :, :3 * D]]
    mu_segs = [rwkv_mu[:, :3 * D]]
    o = 3 * D
    for width in (DECAY_LORA, DECAY_LORA, AAA_LORA, AAA_LORA):
        segs.append(jnp.pad(rw[:, :, o:o + width], ((0, 0), (0, 0), (0, LORA_PAD - width))))
        mu_segs.append(jnp.pad(rwkv_mu[:, o:o + width], ((0, 0), (0, LORA_PAD - width))))
        o += width
    segs.append(rw[:, :, o:])
    mu_segs.append(rwkv_mu[:, o:])
    w_rw = jnp.concatenate(segs, axis=-1).astype(BF16)
    mu_rw = jnp.concatenate(mu_segs, axis=-1)[:, None, :]
    w_gt = w_in[:, :, rw_end:].astype(BF16)
    w2p = _pad_rows(rwkv_w2, LORA_PAD).astype(BF16)
    a2p = _pad_rows(rwkv_a2, LORA_PAD).astype(BF16)
    g2b = rwkv_g2.astype(BF16)
    wq4 = w_uq.reshape(L, Q_LORA, HEADS, QK_DIM)
    wq_r = wq4[..., NOPE:]
    wq = jnp.concatenate([wq4[..., :NOPE], wq_r, _rot_cols(wq_r)], axis=-1)
    wq = wq.reshape(L, Q_LORA, HEADS * 2 * LANES).astype(BF16)
    wkv4 = w_ukv.reshape(L, KV_LORA, HEADS, NOPE + V_DIM)
    wk = wkv4[..., :NOPE].reshape(L, KV_LORA, D).astype(BF16)
    wv = wkv4[..., NOPE:].reshape(L, KV_LORA, D).astype(BF16)
    ws_b = gmlp_ws.astype(BF16)
    bs_full = jnp.repeat(jnp.swapaxes(gmlp_bs, 1, 2), GMLP_GW, axis=2)
    w_br = w_branch.astype(BF16)
    w_ob = w_o.astype(BF16)
    rw_pad = jnp.pad(router_w, ((0, 0), (0, 0), (0, LANES - N_EXPERTS)))
    rb_pad = jnp.pad(router_b, ((0, 0), (0, LANES - N_EXPERTS)), constant_values=-1e30)[:, None, :]
    bg4 = b_gate[:, :, None, :]
    bu4 = b_up[:, :, None, :]
    bd4 = b_down[:, :, None, :]

    cos, sin = _rope_tables(t_lat)
    cos_all = jnp.concatenate([jnp.ones((n_ctx, ROPE), F32), jnp.tile(cos, (b_lat, 1))], axis=0)
    sin_all = jnp.concatenate([jnp.zeros((n_ctx, ROPE), F32), jnp.tile(sin, (b_lat, 1))], axis=0)
    tabk = jnp.concatenate([cos_all, sin_all], axis=-1)
    tabq = jnp.concatenate([jnp.ones((n, NOPE), F32), tabk], axis=-1)
    tab_cache = jnp.concatenate([jnp.ones((b_lat * past, ROPE), F32), jnp.zeros((b_lat * past, ROPE), F32)], -1)

    zeros_state = jnp.zeros((b_ctx, RHEADS, RH, RH), F32)
    sf_lat = state_rwkv_fwd
    sb_lat = state_rwkv_bwd

    cap = (n * TOP_K // BM + N_EXPERTS) * BM
    x = jnp.concatenate([x_prompt.reshape(n_ctx, D), x_sample.reshape(n_lat, D)], axis=0)
    h = ln_modulate(x, mod_all[0], modrow)
    ckv_out, kr_out, sf_out, sb_out = [], [], [], []
    for l in range(L):
        mod = mod_all[l]
        p_gm = matmul(h, w_gm, l, 1024, "in_proj_gmlp", BF16)
        p_mla = matmul(h, w_mla, l, MLA_COLS, "in_proj_mla")
        p_rw = matmul(h, w_rw, l, 768, "in_proj_rwkv")
        p_gate = matmul(h, w_gt, l, 1024, "in_proj_gate", BF16)

        o_a = gmlp(p_gm, gmlp_ln_g[l][None], gmlp_ln_b[l][None], ws_b[l], bs_full[l])

        q = q_proj(p_mla, q_norm_g[l][None], wq[l], tabq)
        kcat, vv, ckvn = kv_proj(p_mla, 1, p_mla, (Q_LORA + KV_LORA) // LANES, kv_norm_g[l][None],
                                 wk[l], wv[l], tabk, True)
        cache_c = cache_ckv[:, l].reshape(b_lat * past, KV_LORA)
        cache_r = jnp.pad(cache_krope[:, l].reshape(b_lat * past, ROPE), ((0, 0), (0, ROPE)))
        kcat_c, vv_c, _ = kv_proj(cache_c, 0, cache_r, 0, kv_norm_g[l][None], wk[l], wv[l], tab_cache, False)
        hw2 = HEADS * 2 * LANES
        kcat_lat = jnp.concatenate([kcat_c.reshape(b_lat, past, hw2), kcat[n_ctx:].reshape(b_lat, t_lat, hw2)],
                                   axis=1).reshape(b_lat * (past + t_lat), hw2)
        vv_lat = jnp.concatenate([vv_c.reshape(b_lat, past, D), vv[n_ctx:].reshape(b_lat, t_lat, D)],
                                 axis=1).reshape(b_lat * (past + t_lat), D)
        o_b_ctx = attention(q, 0, kcat, vv, b_ctx, seq, seq, HEADS, "mla_attention_ctx")
        o_b_lat = attention(q, n_ctx, kcat_lat, vv_lat, b_lat, t_lat, past + t_lat, 2, "mla_attention_lat")
        o_b = jnp.concatenate([o_b_ctx, o_b_lat], axis=0)
        ckv_out.append(ckvn[:n_ctx].reshape(b_ctx, seq, KV_LORA))
        kr_out.append(p_mla[:n_ctx, Q_LORA + KV_LORA:Q_LORA + KV_LORA + ROPE].reshape(b_ctx, seq, ROPE))

        (r, v, kk, lwf, lwb, kdf, kdb, bqf, bqb, bonus, gg) = rwkv_prep(
            p_rw, mu_rw[l], rwkv_kk[l][None], rwkv_w0[l], w2p[l], rwkv_a0[l], a2p[l], rwkv_ka[l],
            rwkv_rk[l].reshape(1, D), g2b[l], n_ctx // RB, seq // RB, t_lat // RB)
        yf_c, yb_c, sf_c, sb_c = rwkv_scan(r, v, kk, lwf, lwb, kdf, kdb, bqf, bqb, zeros_state, zeros_state,
                                           0, b_ctx, seq, "rwkv_scan_ctx")
        yf_l, yb_l, _, _ = rwkv_scan(r, v, kk, lwf, lwb, kdf, kdb, bqf, bqb, sf_lat[:, l], sb_lat[:, l],
                                     n_ctx, b_lat, t_lat, "rwkv_scan_lat")
        sf_out.append(sf_c)
        sb_out.append(sb_c)
        o_c = rwkv_finish(jnp.concatenate([yf_c, yf_l], 0), jnp.concatenate([yb_c, yb_l], 0), bonus, gg,
                          rwkv_lnx_g[l][None], rwkv_lnx_b[l][None])

        z = branch_merge(o_a, o_b, o_c, w_br, l, p_gate)
        x1, h2, h2p = out_proj(z, w_ob, l, x, mod, modrow, ln1_g[l][None], ln1_b[l][None])

        idx, wts, hist = router(h2, rw_pad[l], rb_pad[l])
        base, pad_end = moe_layout(hist)
        dest = moe_dest(idx, base)
        dest_flat = dest[:, :TOP_K].reshape(n // TB, 1, TB * TOP_K)
        xb = moe_dispatch(dest_flat, h2p, jnp.zeros((cap, D // 2), jnp.uint32))
        n_valid = (pad_end[N_EXPERTS - 1] // BM).astype(jnp.int32)
        blk_start = jnp.minimum(jnp.arange(cap // BM, dtype=jnp.int32), n_valid - 1) * BM
        block_e = jnp.sum((pad_end[None, :N_EXPERTS] <= blk_start[:, None]).astype(jnp.int32), axis=1)
        block_e = jnp.minimum(block_e, N_EXPERTS - 1)
        nv = n_valid.reshape(1)
        act = expert_up(block_e, nv, xb, w_gate, bg4, w_up, bu4, l)
        yb = expert_down(block_e, nv, act, w_down, bd4, l)
        mod_next = mod_all[min(l + 1, L - 1)]
        x, h = moe_combine(dest_flat, wts, x1, mod, mod_next, modrow, ln2_g[l][None], ln2_b[l][None], yb)

    y_prompt = x[:n_ctx].reshape(b_ctx, seq, D)
    y_sample = x[n_ctx:].reshape(b_lat, t_lat, D)
    return (y_prompt, y_sample, jnp.stack(ckv_out, axis=1), jnp.stack(kr_out, axis=1),
            jnp.stack(sf_out, axis=1), jnp.stack(sb_out, axis=1))
```

```python
import functools
import math

import jax
import jax.numpy as jnp
from jax import lax
from jax.experimental import pallas as pl
from jax.experimental.pallas import tpu as pltpu

F32 = jnp.float32
BF16 = jnp.bfloat16
HI = lax.Precision.HIGHEST

DEPTH_NORM = 4
GRID_W = 64
CHUNK = 128
D = 2048
GMLP_GW = 128
GMLP_GROUPS = D // GMLP_GW
V_DIM = 128
NOPE = 128
ROPE = 64
QK_DIM = NOPE + ROPE
HEADS = D // V_DIM
Q_LORA = 512
KV_LORA = 512
ROPE_THETA = 10000.0
RH = 64
RHEADS = D // RH
DECAY_LORA = 96
AAA_LORA = 96
GATE_LORA = 256
GN_EPS = 64e-5
N_EXPERTS = 32
TOP_K = 4
SWIGLU_LIMIT = 7.0
SWIGLU_ALPHA = 1.702
LN_EPS = 1e-5
RMS_EPS = 1e-6
ALPHA = (2 * DEPTH_NORM) ** 0.25

LANES = 128
VMEM_LIMIT = 48 * 1024 * 1024

TB = 256
RB = 128
SC = 64
HG = 32
BM = 512
EXPERT_TN = 1024
EXPERT_UP_TN = 1024
EXPERT_VMEM_LIMIT = 56 * 1024 * 1024
DMA_UNROLL = 8
LORA_PAD = 128
RW_COLS = 3 * D + 4 * LORA_PAD + GATE_LORA
MLA_COLS = Q_LORA + KV_LORA + 2 * ROPE


def _cp(*sem):
    return pltpu.CompilerParams(dimension_semantics=sem, vmem_limit_bytes=VMEM_LIMIT)


def _ln(x):
    mu = jnp.mean(x, -1, keepdims=True)
    xc = x - mu
    var = jnp.mean(xc * xc, -1, keepdims=True)
    return xc * lax.rsqrt(var + LN_EPS)


def _sigmoid(x):
    return 1.0 / (1.0 + jnp.exp(-x))


def _gelu_tanh(x):
    return 0.5 * x * (1.0 + jnp.tanh(math.sqrt(2.0 / math.pi) * (x + 0.044715 * (x * x * x))))


def _mod_kernel(c_ref, w_ref, b_ref, o_ref):
    c = c_ref[...]
    s = (c * _sigmoid(c)).astype(BF16)
    o_ref[...] = jnp.dot(s, w_ref[...].astype(BF16), preferred_element_type=F32) + b_ref[...]


def modulation(cond8, w_mod, b_mod):
    L = w_mod.shape[0]
    tn = 1024
    return pl.pallas_call(
        _mod_kernel,
        grid=(L, 6 * D // tn),
        in_specs=[pl.BlockSpec((8, D), lambda l, j: (0, 0)),
                  pl.BlockSpec((None, D, tn), lambda l, j: (l, 0, j)),
                  pl.BlockSpec((None, 1, tn), lambda l, j: (l, 0, j))],
        out_specs=pl.BlockSpec((None, 8, tn), lambda l, j: (l, 0, j)),
        out_shape=jax.ShapeDtypeStruct((L, 8, 6 * D), F32),
        compiler_params=_cp("arbitrary", "arbitrary"),
        name="modulation",
    )(cond8, w_mod, b_mod.reshape(L, 1, 6 * D))


def _lnmod_kernel(x_ref, mod_ref, h_ref):
    m = mod_ref[...]
    h_ref[...] = (_ln(x_ref[...]) * (1.0 + m[1:2]) + m[0:1]).astype(BF16)


def ln_modulate(x, mod, modrow):
    n = x.shape[0]
    return pl.pallas_call(
        _lnmod_kernel,
        grid=(n // TB,),
        in_specs=[pl.BlockSpec((TB, D), lambda i: (i, 0)),
                  pl.BlockSpec((None, 6, D), lambda i: (modrow(i), 0, 0))],
        out_specs=pl.BlockSpec((TB, D), lambda i: (i, 0)),
        out_shape=jax.ShapeDtypeStruct((n, D), BF16),
        compiler_params=_cp("arbitrary"),
        name="ln_modulate",
    )(x, mod)


def _mm_kernel(x_ref, w_ref, o_ref):
    o_ref[...] = jnp.dot(x_ref[...], w_ref[...], preferred_element_type=F32).astype(o_ref.dtype)


def matmul(x, w, layer, tn, name, out_dtype=F32):
    m, k = x.shape
    nn = w.shape[2]
    tm = 512 if m % 512 == 0 else TB
    return pl.pallas_call(
        _mm_kernel,
        grid=(nn // tn, m // tm),
        in_specs=[pl.BlockSpec((tm, k), lambda j, i: (i, 0)),
                  pl.BlockSpec((None, k, tn), lambda j, i: (layer, 0, j))],
        out_specs=pl.BlockSpec((tm, tn), lambda j, i: (i, j)),
        out_shape=jax.ShapeDtypeStruct((m, nn), out_dtype),
        compiler_params=_cp("arbitrary", "arbitrary"),
        name=name,
    )(x, w)


def _gmlp_kernel(p_ref, g_ref, b_ref, ws_ref, bs_ref, o_ref):
    u = _gelu_tanh(p_ref[:, :D].astype(F32))
    v = _gelu_tanh(p_ref[:, D:].astype(F32))
    v = (_ln(v) * g_ref[...] + b_ref[...]).astype(BF16)
    for g in range(GMLP_GROUPS):
        sl = slice(g * GMLP_GW, (g + 1) * GMLP_GW)
        mixed = jnp.dot(ws_ref[g], v[:, sl], preferred_element_type=F32) + bs_ref[:, sl]
        o_ref[:, sl] = (u[:, sl] * mixed).astype(BF16)


def gmlp(p_gm, ln_g, ln_b, ws, bs_full):
    n = p_gm.shape[0]
    return pl.pallas_call(
        _gmlp_kernel,
        grid=(n // CHUNK,),
        in_specs=[pl.BlockSpec((CHUNK, 2 * D), lambda i: (i, 0)),
                  pl.BlockSpec((1, D), lambda i: (0, 0)),
                  pl.BlockSpec((1, D), lambda i: (0, 0)),
                  pl.BlockSpec((GMLP_GROUPS, CHUNK, CHUNK), lambda i: (0, 0, 0)),
                  pl.BlockSpec((CHUNK, D), lambda i: (0, 0))],
        out_specs=pl.BlockSpec((CHUNK, D), lambda i: (i, 0)),
        out_shape=jax.ShapeDtypeStruct((n, D), BF16),
        compiler_params=_cp("arbitrary"),
        name="gmlp",
    )(p_gm, ln_g, ln_b, ws, bs_full)


def _qproj_kernel(cq_ref, g_ref, w_ref, tab_ref, q_ref):
    cq = cq_ref[...]
    n = (cq * lax.rsqrt(jnp.mean(cq * cq, -1, keepdims=True) + RMS_EPS) * g_ref[...]).astype(BF16)
    tab = tab_ref[...]
    hw = 2 * LANES
    for h in range(HEADS):
        sl = slice(h * hw, (h + 1) * hw)
        q_ref[:, sl] = (jnp.dot(n, w_ref[:, sl], preferred_element_type=F32) * tab).astype(BF16)


def q_proj(p_mla, g, wq, tabq):
    n = p_mla.shape[0]
    hw = 2 * LANES
    return pl.pallas_call(
        _qproj_kernel,
        grid=(n // TB,),
        in_specs=[pl.BlockSpec((TB, Q_LORA), lambda i: (i, 0)),
                  pl.BlockSpec((1, Q_LORA), lambda i: (0, 0)),
                  pl.BlockSpec((Q_LORA, HEADS * hw), lambda i: (0, 0)),
                  pl.BlockSpec((TB, hw), lambda i: (i, 0))],
        out_specs=pl.BlockSpec((TB, HEADS * hw), lambda i: (i, 0)),
        out_shape=jax.ShapeDtypeStruct((n, HEADS * hw), BF16),
        compiler_params=_cp("arbitrary"),
        name="mla_q_proj",
    )(p_mla, g, wq, tabq)


def _kvproj_kernel(ckv_ref, kr_ref, g_ref, wk_ref, wv_ref, tab_ref, kcat_ref, v_ref, ckvn_ref, *, normalize):
    ckv = ckv_ref[...]
    if normalize:
        ckv = ckv * lax.rsqrt(jnp.mean(ckv * ckv, -1, keepdims=True) + RMS_EPS) * g_ref[...]
    ckvn_ref[...] = ckv
    nb = ckv.astype(BF16)
    t = kr_ref[...] * tab_ref[...]
    khi = (t + pltpu.roll(t, ROPE, axis=1)).astype(BF16)
    kn = jnp.dot(nb, wk_ref[...], preferred_element_type=F32).astype(BF16)
    v_ref[...] = jnp.dot(nb, wv_ref[...], preferred_element_type=F32).astype(BF16)
    hw = 2 * LANES
    for h in range(HEADS):
        kcat_ref[:, h * hw:h * hw + LANES] = kn[:, h * LANES:(h + 1) * LANES]
        kcat_ref[:, h * hw + LANES:(h + 1) * hw] = khi


def kv_proj(ckv_src, ckv_col, kr_src, kr_col, g, wk, wv, tabk, normalize):
    n = ckv_src.shape[0]
    hw = 2 * LANES
    return pl.pallas_call(
        functools.partial(_kvproj_kernel, normalize=normalize),
        grid=(n // TB,),
        in_specs=[pl.BlockSpec((TB, KV_LORA), lambda i: (i, ckv_col)),
                  pl.BlockSpec((TB, LANES), lambda i: (i, kr_col)),
                  pl.BlockSpec((1, KV_LORA), lambda i: (0, 0)),
                  pl.BlockSpec((KV_LORA, D), lambda i: (0, 0)),
                  pl.BlockSpec((KV_LORA, D), lambda i: (0, 0)),
                  pl.BlockSpec((TB, LANES), lambda i: (i, 0))],
        out_specs=[pl.BlockSpec((TB, HEADS * hw), lambda i: (i, 0)),
                   pl.BlockSpec((TB, D), lambda i: (i, 0)),
                   pl.BlockSpec((TB, KV_LORA), lambda i: (i, 0))],
        out_shape=[jax.ShapeDtypeStruct((n, HEADS * hw), BF16),
                   jax.ShapeDtypeStruct((n, D), BF16),
                   jax.ShapeDtypeStruct((n, KV_LORA), F32)],
        compiler_params=_cp("arbitrary"),
        name="mla_kv_proj",
    )(ckv_src, kr_src, g, wk, wv, tabk)


def _attn_kernel(q_ref, k_ref, v_ref, o_ref, *, nh):
    hw = 2 * LANES
    for h in range(nh):
        s = lax.dot_general(q_ref[:, h * hw:(h + 1) * hw], k_ref[:, h * hw:(h + 1) * hw],
                            (((1,), (1,)), ((), ())), preferred_element_type=F32)
        s = s * (QK_DIM ** -0.5)
        e = jnp.exp(s - jnp.max(s, -1, keepdims=True))
        p = (e * (1.0 / jnp.sum(e, -1, keepdims=True))).astype(BF16)
        o_ref[:, h * V_DIM:(h + 1) * V_DIM] = jnp.dot(
            p, v_ref[:, h * V_DIM:(h + 1) * V_DIM], preferred_element_type=F32).astype(BF16)


def _attn_lat_kernel(q_ref, kc_ref, vc_ref, kn_ref, vn_ref, o_ref, *, nh, past):
    hw = 2 * LANES
    nt = (((1,), (1,)), ((), ()))
    for h in range(nh):
        q = q_ref[:, h * hw:(h + 1) * hw]
        s = jnp.concatenate(
            [lax.dot_general(q, kc_ref[:, h * hw:(h + 1) * hw], nt, preferred_element_type=F32),
             lax.dot_general(q, kn_ref[:, h * hw:(h + 1) * hw], nt, preferred_element_type=F32)], axis=1)
        s = s * (QK_DIM ** -0.5)
        e = jnp.exp(s - jnp.max(s, -1, keepdims=True))
        p = (e * (1.0 / jnp.sum(e, -1, keepdims=True))).astype(BF16)
        vs = slice(h * V_DIM, (h + 1) * V_DIM)
        o_ref[:, vs] = (jnp.dot(p[:, :past], vc_ref[:, vs], preferred_element_type=F32)
                        + jnp.dot(p[:, past:], vn_ref[:, vs], preferred_element_type=F32)).astype(BF16)


def attention_lat(q, row0, kcat_c, v_c, kcat, v, n_seq, t_q, past, nh):
    hw = 2 * LANES
    qb = t_q // TB
    qoff = row0 // TB
    koff = row0 // t_q
    return pl.pallas_call(
        functools.partial(_attn_lat_kernel, nh=nh, past=past),
        grid=(n_seq, HEADS // nh, qb),
        in_specs=[pl.BlockSpec((TB, nh * hw), lambda s, h, i: (qoff + s * qb + i, h)),
                  pl.BlockSpec((past, nh * hw), lambda s, h, i: (s, h)),
                  pl.BlockSpec((past, nh * V_DIM), lambda s, h, i: (s, h)),
                  pl.BlockSpec((t_q, nh * hw), lambda s, h, i: (koff + s, h)),
                  pl.BlockSpec((t_q, nh * V_DIM), lambda s, h, i: (koff + s, h))],
        out_specs=pl.BlockSpec((TB, nh * V_DIM), lambda s, h, i: (s * qb + i, h)),
        out_shape=jax.ShapeDtypeStruct((n_seq * t_q, D), BF16),
        compiler_params=_cp("arbitrary", "arbitrary", "arbitrary"),
        name="mla_attention_lat",
    )(q, kcat_c, v_c, kcat, v)


def attention(q, q_row0, kcat, v, n_seq, t_q, n_keys, nh, name):
    hw = 2 * LANES
    qb = t_q // TB
    qoff = q_row0 // TB
    return pl.pallas_call(
        functools.partial(_attn_kernel, nh=nh),
        grid=(n_seq, HEADS // nh, qb),
        in_specs=[pl.BlockSpec((TB, nh * hw), lambda s, h, i: (qoff + s * qb + i, h)),
                  pl.BlockSpec((n_keys, nh * hw), lambda s, h, i: (s, h)),
                  pl.BlockSpec((n_keys, nh * V_DIM), lambda s, h, i: (s, h))],
        out_specs=pl.BlockSpec((TB, nh * V_DIM), lambda s, h, i: (s * qb + i, h)),
        out_shape=jax.ShapeDtypeStruct((n_seq * t_q, D), BF16),
        compiler_params=_cp("arbitrary", "arbitrary", "arbitrary"),
        name=name,
    )(q, kcat, v)


def _head_ones():
    r = lax.broadcasted_iota(jnp.int32, (LANES, LANES), 0) >= RH
    c = lax.broadcasted_iota(jnp.int32, (LANES, LANES), 1) >= RH
    return (r == c).astype(BF16)


def _head_sum(x, ones_bd):
    hi, lo = _split_bf16(x)
    parts = [jnp.dot(hi[:, i * LANES:(i + 1) * LANES], ones_bd, preferred_element_type=F32)
             + jnp.dot(lo[:, i * LANES:(i + 1) * LANES], ones_bd, preferred_element_type=F32)
             for i in range(x.shape[1] // LANES)]
    return jnp.concatenate(parts, axis=1)


def _rwkv_prep_kernel(p_ref, prev_ref, next_ref, mu_ref, kkw_ref, w0_ref, w2_ref, a0_ref, a2_ref, ka_ref,
                      rk_ref, g2_ref,
                      r_o, v_o, kk_o, lwf_o, lwb_o, kdf_o, kdb_o, bqf_o, bqb_o, bonus_o, g_o,
                      *, nb_ctx, bps_ctx, bps_lat):
    i = pl.program_id(0)
    is_lat = i >= nb_ctx
    pos = jnp.where(is_lat, lax.rem(i - nb_ctx, bps_lat), lax.rem(i, bps_ctx))
    last = jnp.where(is_lat, bps_lat - 1, bps_ctx - 1)
    has_prev = (pos != 0).astype(F32)
    has_next = (pos != last).astype(F32)
    rows = lax.broadcasted_iota(jnp.int32, (RB, 1), 0)

    def shifted(lo, hi):
        p = p_ref[:, lo:hi]
        prev = jnp.where(rows == 0, prev_ref[7:8, lo:hi] * has_prev, pltpu.roll(p, 1, axis=0))
        nxt = jnp.where(rows == RB - 1, next_ref[0:1, lo:hi] * has_next, pltpu.roll(p, RB - 1, axis=0))
        return p + (0.5 * (prev + nxt) - p) * mu_ref[:, lo:hi]

    ones_bd = _head_ones()
    r = shifted(0, D)
    k = shifted(D, 2 * D)
    v = shifted(2 * D, 3 * D)
    r_o[...] = r
    v_o[...] = v
    kkf = k * kkw_ref[...]
    nrm = jnp.sqrt(_head_sum(kkf * kkf, ones_bd))
    kk = kkf / jnp.maximum(nrm, 1e-12)
    kk_o[...] = kk
    bonus_o[...] = _head_sum(r * k * rk_ref[...], ones_bd) * v
    base = 3 * D
    outs = ((lwf_o, kdf_o, bqf_o), (lwb_o, kdb_o, bqb_o))
    for d in range(2):
        wl = shifted(base + d * LORA_PAD, base + (d + 1) * LORA_PAD)
        z = w0_ref[d:d + 1, :] + jnp.dot(jnp.tanh(wl).astype(BF16), w2_ref[d], preferred_element_type=F32)
        y = -z
        softplus = jnp.maximum(y, 0.0) + jnp.log(1.0 + jnp.exp(-jnp.abs(y)))
        logw = -softplus - 0.5
        outs[d][0][...] = -jnp.exp(logw)
        al = shifted(base + (2 + d) * LORA_PAD, base + (3 + d) * LORA_PAD)
        a = _sigmoid(a0_ref[d:d + 1, :] + jnp.dot(al.astype(BF16), a2_ref[d], preferred_element_type=F32))
        outs[d][1][...] = k * (1.0 + (a - 1.0) * ka_ref[d:d + 1, :])
        outs[d][2][...] = kk * a
    gl = shifted(base + 4 * LORA_PAD, base + 4 * LORA_PAD + GATE_LORA)
    g_o[...] = jnp.dot(_sigmoid(gl).astype(BF16), g2_ref[...], preferred_element_type=F32)


def rwkv_prep(p_rw, mu, kkw, w0, w2, a0, a2, ka, rk, g2, nb_ctx, bps_ctx, bps_lat):
    n = p_rw.shape[0]
    nb = n // RB
    sub = RB // 8
    row = lambda i: (i, 0)
    const2 = lambda i: (0, 0)
    out = jax.ShapeDtypeStruct((n, D), F32)
    return pl.pallas_call(
        functools.partial(_rwkv_prep_kernel, nb_ctx=nb_ctx, bps_ctx=bps_ctx, bps_lat=bps_lat),
        grid=(nb,),
        in_specs=[pl.BlockSpec((RB, RW_COLS), row),
                  pl.BlockSpec((8, RW_COLS), lambda i: (jnp.maximum(i * sub - 1, 0), 0)),
                  pl.BlockSpec((8, RW_COLS), lambda i: (jnp.minimum((i + 1) * sub, nb * sub - 1), 0)),
                  pl.BlockSpec((1, RW_COLS), const2),
                  pl.BlockSpec((1, D), const2),
                  pl.BlockSpec((2, D), const2),
                  pl.BlockSpec((2, LORA_PAD, D), lambda i: (0, 0, 0)),
                  pl.BlockSpec((2, D), const2),
                  pl.BlockSpec((2, LORA_PAD, D), lambda i: (0, 0, 0)),
                  pl.BlockSpec((2, D), const2),
                  pl.BlockSpec((1, D), const2),
                  pl.BlockSpec((GATE_LORA, D), const2)],
        out_specs=[pl.BlockSpec((RB, D), row)] * 11,
        out_shape=[out] * 11,
        compiler_params=_cp("arbitrary"),
        name="rwkv_prep",
    )(p_rw, p_rw, p_rw, mu, kkw, w0, w2, a0, a2, ka, rk, g2)


def _mm(eq, a, b):
    return jnp.einsum(eq, a.astype(BF16), b.astype(BF16), preferred_element_type=F32)


def _split_bf16(x):
    hi = x.astype(BF16)
    return hi, (x - hi.astype(F32)).astype(BF16)


def _mm3(eq, a, b):
    ah, al = _split_bf16(a)
    bh, bl = _split_bf16(b)
    mm = lambda x, y: jnp.einsum(eq, x, y, preferred_element_type=F32)
    return mm(ah, bh) + (mm(ah, bl) + mm(al, bh))


_NN = "hmk,hkn->hmn"
_NT = "hmk,hnk->hmn"
_TN = "hkm,hkn->hmn"


def _scan_sum(x, reverse):
    rows = lax.broadcasted_iota(jnp.int32, (SC, 1), 0)
    shift = 1
    while shift < SC:
        if reverse:
            x = x + jnp.where(rows < SC - shift, pltpu.roll(x, SC - shift, axis=0), 0.0)
        else:
            x = x + jnp.where(rows >= shift, pltpu.roll(x, shift, axis=0), 0.0)
        shift *= 2
    return x


def _wkv_chunk(refs, s, reverse):
    r, v, kk, lw, kd, bq = (x[...] for x in refs)
    last = 0 if reverse else SC - 1
    L = _scan_sum(lw, reverse)
    ltot = L[last:last + 1, :]
    en = jnp.exp(-L)
    et = jnp.exp(ltot - L)

    def heads(x):
        return jnp.stack([x[:, h * RH:(h + 1) * RH] for h in range(HG)], axis=0)

    ax = heads(-kk * jnp.exp(L - lw))
    rt = heads(r * jnp.exp(L))
    bt = heads(bq * en)
    kt = heads(kd * en)
    bc = heads(bq * et)
    kc = heads(kd * et)
    ptot = heads(jnp.exp(ltot))
    v = heads(v)

    ti = lax.broadcasted_iota(jnp.int32, (SC, SC), 0)
    si = lax.broadcasted_iota(jnp.int32, (SC, SC), 1)
    strict = (si > ti) if reverse else (si < ti)
    incl = (si >= ti) if reverse else (si <= ti)
    bk = jnp.concatenate([bt, kt], axis=1)
    sa = _mm3(_NT, ax, bk)
    sr = _mm(_NT, rt, bk)
    a_ab = jnp.where(strict, sa[:, :, :SC], 0.0)
    a_ak = jnp.where(strict, sa[:, :, SC:], 0.0)
    a_rb = jnp.where(incl, sr[:, :, :SC], 0.0)
    a_rk = jnp.where(incl, sr[:, :, SC:], 0.0)
    npow = a_ab
    e = a_ab
    for _ in range(int(math.log2(SC)) - 2):
        npow = _mm(_NN, npow, npow)
        e = e + npow + _mm(_NN, e, npow)
    ws = _mm(_NT, jnp.concatenate([ax, rt], axis=1), s)
    w = ws[:, :SC] + _mm(_NN, a_ak, v)
    u = w + _mm(_NN, e, w)
    rho = (w - u) + _mm3(_NN, a_ab, u)
    u = u + rho + _mm(_NN, e, rho)
    y = ws[:, SC:] + _mm(_NN, a_rb, u) + _mm(_NN, a_rk, v)
    s_new = s * ptot + _mm3(_TN, jnp.concatenate([u, v], axis=1), jnp.concatenate([bc, kc], axis=1))
    return jnp.concatenate([y[h] for h in range(HG)], axis=1), s_new


def _scan_kernel(rf, vf, kkf, lwf, kdf, bqf, rb, vb, kkb, lwb, kdb, bqb, s0f, s0b,
                 yf_ref, yb_ref, sfo, sbo, stf, stb, *, nc):
    c = pl.program_id(2)

    @pl.when(c == 0)
    def _():
        stf[...] = s0f[...]
        stb[...] = s0b[...]

    for reverse, refs, st, yref in ((False, (rf, vf, kkf, lwf, kdf, bqf), stf, yf_ref),
                                    (True, (rb, vb, kkb, lwb, kdb, bqb), stb, yb_ref)):
        y, s_new = _wkv_chunk(refs, st[...], reverse)
        st[...] = s_new
        yref[...] = y

    @pl.when(c == nc - 1)
    def _():
        sfo[...] = stf[...]
        sbo[...] = stb[...]


def rwkv_scan(r, v, kk, lwf, lwb, kdf, kdb, bqf, bqb, s0f, s0b, row0, n_seq, t_len, name):
    nc = t_len // SC
    off = row0 // SC
    cw = HG * RH
    fwd = lambda s, h, c: (off + s * nc + c, h)
    bwd = lambda s, h, c: (off + s * nc + (nc - 1 - c), h)
    blk = lambda im: pl.BlockSpec((SC, cw), im)
    st_spec = pl.BlockSpec((None, HG, RH, RH), lambda s, h, c: (s, h, 0, 0))
    n_rows = n_seq * t_len
    return pl.pallas_call(
        functools.partial(_scan_kernel, nc=nc),
        grid=(n_seq, RHEADS // HG, nc),
        in_specs=[blk(fwd)] * 6 + [blk(bwd)] * 6 + [st_spec, st_spec],
        out_specs=[pl.BlockSpec((SC, cw), lambda s, h, c: (s * nc + c, h)),
                   pl.BlockSpec((SC, cw), lambda s, h, c: (s * nc + (nc - 1 - c), h)),
                   st_spec, st_spec],
        out_shape=[jax.ShapeDtypeStruct((n_rows, D), F32), jax.ShapeDtypeStruct((n_rows, D), F32),
                   jax.ShapeDtypeStruct(s0f.shape, F32), jax.ShapeDtypeStruct(s0b.shape, F32)],
        scratch_shapes=[pltpu.VMEM((HG, RH, RH), F32), pltpu.VMEM((HG, RH, RH), F32)],
        compiler_params=_cp("arbitrary", "arbitrary", "arbitrary"),
        name=name,
    )(r, v, kk, lwf, kdf, bqf, r, v, kk, lwb, kdb, bqb, s0f, s0b)


def _rwkv_fin_kernel(yfc_ref, ybc_ref, yfl_ref, ybl_ref, bonus_ref, g_ref, lg_ref, lb_ref, o_ref, *, nb_ctx):
    ones_bd = _head_ones()
    is_lat = pl.program_id(0) >= nb_ctx
    y = jnp.where(is_lat, yfl_ref[...] + ybl_ref[...], yfc_ref[...] + ybc_ref[...])
    mu = _head_sum(y, ones_bd) * (1.0 / RH)
    yc = y - mu
    var = _head_sum(yc * yc, ones_bd) * (1.0 / RH)
    yn = yc * lax.rsqrt(var + GN_EPS) * lg_ref[...] + lb_ref[...]
    o_ref[...] = ((yn + bonus_ref[...]) * g_ref[...]).astype(BF16)


def rwkv_finish(yf_c, yb_c, yf_l, yb_l, bonus, g, lnx_g, lnx_b):
    n = bonus.shape[0]
    nb_ctx = yf_c.shape[0] // TB
    row = lambda i: (i, 0)
    const2 = lambda i: (0, 0)
    ctx = pl.BlockSpec((TB, D), lambda i: (jnp.minimum(i, nb_ctx - 1), 0))
    lat = pl.BlockSpec((TB, D), lambda i: (jnp.maximum(i - nb_ctx, 0), 0))
    return pl.pallas_call(
        functools.partial(_rwkv_fin_kernel, nb_ctx=nb_ctx),
        grid=(n // TB,),
        in_specs=[ctx, ctx, lat, lat] + [pl.BlockSpec((TB, D), row)] * 2 + [pl.BlockSpec((1, D), const2)] * 2,
        out_specs=pl.BlockSpec((TB, D), row),
        out_shape=jax.ShapeDtypeStruct((n, D), BF16),
        compiler_params=_cp("arbitrary"),
        name="rwkv_finish",
    )(yf_c, yb_c, yf_l, yb_l, bonus, g, lnx_g, lnx_b)


def _merge_kernel(oa_ref, ob_ref, oc_ref, w_ref, ga_ref, gb_ref, gc_ref, z_ref):
    acc = None
    for n, (o_ref, g_ref) in enumerate(((oa_ref, ga_ref), (ob_ref, gb_ref), (oc_ref, gc_ref))):
        y = _sigmoid(g_ref[...].astype(F32)) * jnp.dot(o_ref[...], w_ref[n], preferred_element_type=F32)
        acc = y if acc is None else acc + y
    z_ref[...] = acc.astype(BF16)


def branch_merge(o_a, o_b, o_c, w_branch, layer, p_gate):
    n = o_a.shape[0]
    tn = 512
    tm = 512 if n % 512 == 0 else TB
    nj = D // tn
    o_spec = pl.BlockSpec((tm, D), lambda j, i: (i, 0))
    gate = lambda b: pl.BlockSpec((tm, tn), lambda j, i: (i, b * nj + j))
    return pl.pallas_call(
        _merge_kernel,
        grid=(nj, n // tm),
        in_specs=[o_spec, o_spec, o_spec,
                  pl.BlockSpec((None, 3, D, tn), lambda j, i: (layer, 0, 0, j)),
                  gate(0), gate(1), gate(2)],
        out_specs=pl.BlockSpec((tm, tn), lambda j, i: (i, j)),
        out_shape=jax.ShapeDtypeStruct((n, D), BF16),
        compiler_params=_cp("arbitrary", "arbitrary"),
        name="branch_merge",
    )(o_a, o_b, o_c, w_branch, p_gate, p_gate, p_gate)


def _bf16_bits(x):
    return lax.bitcast_convert_type(x.astype(BF16).astype(F32), jnp.uint32)


def _pack_rows(x):
    return (_bf16_bits(x[:, :D // 2]) >> 16) | _bf16_bits(x[:, D // 2:])


def _unpack_rows(p):
    lo = lax.bitcast_convert_type(p << 16, F32)
    hi = lax.bitcast_convert_type(p & jnp.uint32(0xFFFF0000), F32)
    return jnp.concatenate([lo, hi], axis=1).astype(BF16)


def _oproj_kernel(z_ref, wo_ref, x_ref, mod_ref, g_ref, b_ref, x1_ref, h2_ref, h2p_ref):
    mix = jnp.dot(z_ref[...], wo_ref[...], preferred_element_type=F32)
    m = mod_ref[...]
    x1 = _ln(ALPHA * x_ref[...] + m[2:3] * mix) * g_ref[...] + b_ref[...]
    x1_ref[...] = x1
    h2 = _ln(x1) * (1.0 + m[4:5]) + m[3:4]
    h2_ref[...] = h2
    h2p_ref[...] = _pack_rows(h2)


def out_proj(z, w_o, layer, x, mod, modrow, ln_g, ln_b):
    n = z.shape[0]
    row = lambda i: (i, 0)
    const2 = lambda i: (0, 0)
    return pl.pallas_call(
        _oproj_kernel,
        grid=(n // TB,),
        in_specs=[pl.BlockSpec((TB, D), row),
                  pl.BlockSpec((None, D, D), lambda i: (layer, 0, 0)),
                  pl.BlockSpec((TB, D), row),
                  pl.BlockSpec((None, 6, D), lambda i: (modrow(i), 0, 0)),
                  pl.BlockSpec((1, D), const2),
                  pl.BlockSpec((1, D), const2)],
        out_specs=[pl.BlockSpec((TB, D), row), pl.BlockSpec((TB, D), row), pl.BlockSpec((TB, D // 2), row)],
        out_shape=[jax.ShapeDtypeStruct((n, D), F32), jax.ShapeDtypeStruct((n, D), F32),
                   jax.ShapeDtypeStruct((n, D // 2), jnp.uint32)],
        compiler_params=_cp("arbitrary"),
        name="out_proj_ln",
    )(z, w_o, x, mod, ln_g, ln_b)


def _router_kernel(h_ref, w_ref, b_ref, idx_ref, wts_ref, hist_ref):
    logits = jnp.dot(h_ref[...], w_ref[...], precision=HI, preferred_element_type=F32) + b_ref[...]
    lane = lax.broadcasted_iota(jnp.int32, (TB, LANES), 1).astype(F32)
    cur = logits
    idx_out = jnp.zeros((TB, LANES), F32)
    val_out = jnp.zeros((TB, LANES), F32)
    hist = jnp.zeros((TB, LANES), F32)
    top = None
    for k in range(TOP_K):
        m = jnp.max(cur, -1, keepdims=True)
        am = jnp.min(jnp.where(cur == m, lane, float(LANES)), -1, keepdims=True)
        sel = lane == am
        top = m if top is None else top
        idx_out = jnp.where(lane == k, am, idx_out)
        val_out = jnp.where(lane == k, jnp.exp(m - top), val_out)
        hist = hist + sel.astype(F32)
        cur = jnp.where(sel, -jnp.inf, cur)
    idx_ref[...] = idx_out.astype(jnp.int32)
    wts_ref[...] = val_out * (1.0 / jnp.sum(val_out, -1, keepdims=True))
    hist_ref[...] = jnp.sum(hist, 0, keepdims=True)


def router(h2, rw_pad, rb_pad):
    n = h2.shape[0]
    nb = n // TB
    return pl.pallas_call(
        _router_kernel,
        grid=(nb,),
        in_specs=[pl.BlockSpec((TB, D), lambda i: (i, 0)),
                  pl.BlockSpec((D, LANES), lambda i: (0, 0)),
                  pl.BlockSpec((1, LANES), lambda i: (0, 0))],
        out_specs=[pl.BlockSpec((TB, LANES), lambda i: (i, 0)),
                   pl.BlockSpec((TB, LANES), lambda i: (i, 0)),
                   pl.BlockSpec((None, 1, LANES), lambda i: (i, 0, 0))],
        out_shape=[jax.ShapeDtypeStruct((n, LANES), jnp.int32),
                   jax.ShapeDtypeStruct((n, LANES), F32),
                   jax.ShapeDtypeStruct((nb, 1, LANES), F32)],
        compiler_params=_cp("arbitrary"),
        name="moe_router",
    )(h2, rw_pad, rb_pad)


def _dest_kernel(idx_ref, base_ref, dest_ref):
    lane = lax.broadcasted_iota(jnp.int32, (TB, LANES), 1).astype(F32)
    ri = lax.broadcasted_iota(jnp.int32, (TB, TB), 0)
    ci = lax.broadcasted_iota(jnp.int32, (TB, TB), 1)
    lower = (ci < ri).astype(BF16)
    idx = idx_ref[...].astype(F32)
    run = base_ref[...]
    dest = jnp.zeros((TB, LANES), F32)
    for k in range(TOP_K):
        e_k = jnp.sum(jnp.where(lane == k, idx, 0), -1, keepdims=True)
        onehot = lane == e_k
        oh = onehot.astype(F32)
        before = jnp.dot(lower, oh.astype(BF16), preferred_element_type=F32)
        d_k = jnp.sum(jnp.where(onehot, run + before, 0.0), -1, keepdims=True)
        dest = jnp.where(lane == k, d_k, dest)
        run = run + jnp.sum(oh, 0, keepdims=True)
    dest_ref[...] = dest.astype(jnp.int32)


def moe_dest(idx, base):
    n = idx.shape[0]
    nb = n // TB
    return pl.pallas_call(
        _dest_kernel,
        grid=(nb,),
        in_specs=[pl.BlockSpec((TB, LANES), lambda i: (i, 0)),
                  pl.BlockSpec((None, 1, LANES), lambda i: (i, 0, 0))],
        out_specs=pl.BlockSpec((TB, LANES), lambda i: (i, 0)),
        out_shape=jax.ShapeDtypeStruct((n, LANES), jnp.int32),
        compiler_params=_cp("arbitrary"),
        name="moe_dest",
    )(idx, base)


def _row_copy(src, src_row, dst, dst_row, sem):
    return pltpu.make_async_copy(src.at[pl.ds(src_row, 1)], dst.at[pl.ds(dst_row, 1)], sem)


def _dispatch_kernel(dest_ref, h_ref, xb_in, xb_out, sem):
    del xb_in

    def issue(t, carry):
        for k in range(TOP_K):
            _row_copy(h_ref, t, xb_out, dest_ref[0, t * TOP_K + k], sem).start()
        return carry

    lax.fori_loop(0, TB, issue, 0, unroll=DMA_UNROLL)

    def drain(t, carry):
        for k in range(TOP_K):
            _row_copy(h_ref, t, xb_out, dest_ref[0, t * TOP_K + k], sem).wait()
        return carry

    lax.fori_loop(0, TB, drain, 0, unroll=DMA_UNROLL)


def moe_dispatch(dest_flat, h2, xb_init):
    n = h2.shape[0]
    nb = n // TB
    return pl.pallas_call(
        _dispatch_kernel,
        grid=(nb,),
        in_specs=[pl.BlockSpec((None, 1, TB * TOP_K), lambda i: (i, 0, 0), memory_space=pltpu.SMEM),
                  pl.BlockSpec((TB, D // 2), lambda i: (i, 0)),
                  pl.BlockSpec(memory_space=pl.ANY)],
        out_specs=pl.BlockSpec(memory_space=pl.ANY),
        out_shape=jax.ShapeDtypeStruct(xb_init.shape, xb_init.dtype),
        scratch_shapes=[pltpu.SemaphoreType.DMA(())],
        input_output_aliases={2: 0},
        compiler_params=_cp("arbitrary"),
        name="moe_dispatch",
    )(dest_flat, h2, xb_init)


def _expert_changed(be_ref):
    i = pl.program_id(1)
    return jnp.logical_or(i == 0, be_ref[i] != be_ref[jnp.maximum(i - 1, 0)])


def _expert_up_kernel(be_ref, nv_ref, x_ref, wg_ref, bg_ref, wu_ref, bu_ref, act_ref, wg_bf, wu_bf):
    used = pl.program_id(1) < nv_ref[0]

    @pl.when(jnp.logical_not(used))
    def _():
        act_ref[...] = jnp.zeros_like(act_ref)

    @pl.when(jnp.logical_and(used, _expert_changed(be_ref)))
    def _():
        wg_bf[...] = wg_ref[...].astype(BF16)
        wu_bf[...] = wu_ref[...].astype(BF16)

    @pl.when(used)
    def _():
        half = BM // 2
        for r in range(2):
            rows = slice(r * half, (r + 1) * half)
            xb = _unpack_rows(x_ref[rows, :])
            gl = jnp.minimum(jnp.dot(xb, wg_bf[...], preferred_element_type=F32) + bg_ref[...], SWIGLU_LIMIT)
            lin = jnp.clip(jnp.dot(xb, wu_bf[...], preferred_element_type=F32) + bu_ref[...],
                           -SWIGLU_LIMIT, SWIGLU_LIMIT)
            act_ref[rows, :] = (gl * _sigmoid(SWIGLU_ALPHA * gl) * (lin + 1.0)).astype(BF16)


def expert_up(block_e, n_valid, xb, w_gate, b_gate, w_up, b_up, layer):
    cap = xb.shape[0]
    tn = EXPERT_UP_TN
    rowblk = lambda j, i, be, nv: (jnp.minimum(i, nv[0] - 1), 0)
    wspec = pl.BlockSpec((None, None, D, tn), lambda j, i, be, nv: (layer, be[i], 0, j))
    bspec = pl.BlockSpec((None, None, 1, tn), lambda j, i, be, nv: (layer, be[i], 0, j))
    return pl.pallas_call(
        _expert_up_kernel,
        grid_spec=pltpu.PrefetchScalarGridSpec(
            num_scalar_prefetch=2,
            grid=(D // tn, cap // BM),
            in_specs=[pl.BlockSpec((BM, D // 2), rowblk), wspec, bspec, wspec, bspec],
            out_specs=pl.BlockSpec((BM, tn), lambda j, i, be, nv: (i, j)),
            scratch_shapes=[pltpu.VMEM((D, tn), BF16), pltpu.VMEM((D, tn), BF16)]),
        out_shape=jax.ShapeDtypeStruct((cap, D), BF16),
        compiler_params=pltpu.CompilerParams(dimension_semantics=("arbitrary", "arbitrary"),
                                             vmem_limit_bytes=EXPERT_VMEM_LIMIT),
        name="moe_expert_up",
    )(block_e, n_valid, xb, w_gate, b_gate, w_up, b_up)


def _expert_down_kernel(be_ref, nv_ref, a_ref, wd_ref, bd_ref, y_ref, wd_bf):
    used = pl.program_id(1) < nv_ref[0]

    @pl.when(jnp.logical_not(used))
    def _():
        y_ref[...] = jnp.zeros_like(y_ref)

    @pl.when(jnp.logical_and(used, _expert_changed(be_ref)))
    def _():
        wd_bf[...] = wd_ref[...].astype(BF16)

    @pl.when(used)
    def _():
        y_ref[...] = jnp.dot(a_ref[...], wd_bf[...], preferred_element_type=F32) + bd_ref[...]


def expert_down(block_e, n_valid, act, w_down, b_down, layer):
    cap = act.shape[0]
    tn = EXPERT_TN
    return pl.pallas_call(
        _expert_down_kernel,
        grid_spec=pltpu.PrefetchScalarGridSpec(
            num_scalar_prefetch=2,
            grid=(D // tn, cap // BM),
            in_specs=[pl.BlockSpec((BM, D), lambda j, i, be, nv: (jnp.minimum(i, nv[0] - 1), 0)),
                      pl.BlockSpec((None, None, D, tn), lambda j, i, be, nv: (layer, be[i], 0, j)),
                      pl.BlockSpec((None, None, 1, tn), lambda j, i, be, nv: (layer, be[i], 0, j))],
            out_specs=pl.BlockSpec((BM, tn), lambda j, i, be, nv: (i, j)),
            scratch_shapes=[pltpu.VMEM((D, tn), BF16)]),
        out_shape=jax.ShapeDtypeStruct((cap, D), F32),
        compiler_params=pltpu.CompilerParams(dimension_semantics=("arbitrary", "arbitrary"),
                                             vmem_limit_bytes=EXPERT_VMEM_LIMIT),
        name="moe_expert_down",
    )(block_e, n_valid, act, w_down, b_down)


def _combine_kernel(dest_ref, wts_ref, x1_ref, mod_ref, g_ref, b_ref, modn_ref, yb_ref,
                    x2_ref, hn_ref, rows, sem):
    def issue(t, carry):
        for k in range(TOP_K):
            _row_copy(yb_ref, dest_ref[0, t * TOP_K + k], rows.at[k], t, sem).start()
        return carry

    lax.fori_loop(0, TB, issue, 0, unroll=DMA_UNROLL)

    def drain(t, carry):
        for k in range(TOP_K):
            _row_copy(yb_ref, dest_ref[0, t * TOP_K + k], rows.at[k], t, sem).wait()
        return carry

    lax.fori_loop(0, TB, drain, 0, unroll=DMA_UNROLL)
    w = wts_ref[...]
    ffn = rows[0] * w[:, 0:1]
    for k in range(1, TOP_K):
        ffn = ffn + rows[k] * w[:, k:k + 1]
    m = mod_ref[...]
    x2 = _ln(ALPHA * x1_ref[...] + m[5:6] * ffn) * g_ref[...] + b_ref[...]
    x2_ref[...] = x2
    mn = modn_ref[...]
    hn_ref[...] = (_ln(x2) * (1.0 + mn[1:2]) + mn[0:1]).astype(BF16)


def moe_combine(dest_flat, wts, x1, mod, mod_next, modrow, ln_g, ln_b, yb):
    n = x1.shape[0]
    row = lambda i: (i, 0)
    const2 = lambda i: (0, 0)
    modspec = pl.BlockSpec((None, 6, D), lambda i: (modrow(i), 0, 0))
    return pl.pallas_call(
        _combine_kernel,
        grid=(n // TB,),
        in_specs=[pl.BlockSpec((None, 1, TB * TOP_K), lambda i: (i, 0, 0), memory_space=pltpu.SMEM),
                  pl.BlockSpec((TB, LANES), row),
                  pl.BlockSpec((TB, D), row),
                  modspec,
                  pl.BlockSpec((1, D), const2),
                  pl.BlockSpec((1, D), const2),
                  modspec,
                  pl.BlockSpec(memory_space=pl.ANY)],
        out_specs=[pl.BlockSpec((TB, D), row), pl.BlockSpec((TB, D), row)],
        out_shape=[jax.ShapeDtypeStruct((n, D), F32), jax.ShapeDtypeStruct((n, D), BF16)],
        scratch_shapes=[pltpu.VMEM((TOP_K, TB, D), F32), pltpu.SemaphoreType.DMA(())],
        compiler_params=_cp("arbitrary"),
        name="moe_combine",
    )(dest_flat, wts, x1, mod, ln_g, ln_b, mod_next, yb)


def moe_layout(hist):
    h = hist[:, 0, :].astype(jnp.int32)
    counts = jnp.sum(h, 0)
    padded = (counts + BM - 1) // BM * BM
    pad_end = jnp.cumsum(padded)
    pad_start = pad_end - padded
    before = jnp.cumsum(h, 0) - h
    base = (pad_start[None, :] + before).astype(F32)[:, None, :]
    return base, pad_end


def _rot_cols(w):
    q = ROPE // 4
    return jnp.concatenate([-w[..., q:2 * q], w[..., 0:q], -w[..., 3 * q:4 * q], w[..., 2 * q:3 * q]], axis=-1)


def _rope_tables(n):
    rows = n // GRID_W
    row_id = jnp.repeat(jnp.arange(rows, dtype=F32), GRID_W)
    col_id = jnp.tile(jnp.arange(GRID_W, dtype=F32), rows)
    half = ROPE // 2
    inv_freq = ROPE_THETA ** (-jnp.arange(0, half, 2, dtype=F32) / half)
    ang_r = row_id[:, None] * inv_freq
    ang_c = col_id[:, None] * inv_freq
    cos = jnp.concatenate([jnp.cos(ang_r)] * 2 + [jnp.cos(ang_c)] * 2, axis=-1)
    sin = jnp.concatenate([jnp.sin(ang_r)] * 2 + [jnp.sin(ang_c)] * 2, axis=-1)
    return cos, sin


def _pad_rows(w, rows):
    return jnp.pad(w, [(0, 0)] * (w.ndim - 2) + [(0, rows - w.shape[-2]), (0, 0)])


def kernel(x_prompt, x_sample, cache_ckv, cache_krope, state_rwkv_fwd, state_rwkv_bwd, c, c_ctx, w_mod, b_mod, w_in, gmlp_ln_g, gmlp_ln_b, gmlp_ws, gmlp_bs, q_norm_g, w_uq, kv_norm_g, w_ukv, rwkv_mu, rwkv_w0, rwkv_w2, rwkv_a0, rwkv_a2, rwkv_g2, rwkv_kk, rwkv_ka, rwkv_rk, rwkv_lnx_g, rwkv_lnx_b, w_branch, w_o, ln1_g, ln1_b, router_w, router_b, w_gate, b_gate, w_up, b_up, w_down, b_down, ln2_g, ln2_b):
    L = w_mod.shape[0]
    b_ctx, seq, _ = x_prompt.shape
    b_lat, t_lat, _ = x_sample.shape
    past = cache_ckv.shape[2]
    n_ctx = b_ctx * seq
    n_lat = b_lat * t_lat
    n = n_ctx + n_lat
    assert seq % TB == 0 and t_lat % TB == 0 and past % TB == 0 and b_lat + 1 <= 8 and n_ctx % t_lat == 0
    nb_ctx = n_ctx // TB
    bps_lat = t_lat // TB

    def modrow(i):
        return jnp.where(i < nb_ctx, 0, 1 + (i - nb_ctx) // bps_lat)

    cond8 = jnp.zeros((8, D), F32).at[0].set(c_ctx).at[1:1 + b_lat].set(c)
    mod_all = modulation(cond8, w_mod, b_mod).reshape(L, 8, 6, D)

    gm_end = 2 * D
    mla_end = gm_end + Q_LORA + KV_LORA + ROPE
    rw_end = mla_end + 3 * D + 2 * DECAY_LORA + 2 * AAA_LORA + GATE_LORA
    w_gm = w_in[:, :, :gm_end].astype(BF16)
    w_kr = w_in[:, :, mla_end - ROPE:mla_end]
    w_mla = jnp.concatenate([w_in[:, :, gm_end:mla_end], _rot_cols(w_kr)], axis=-1).astype(BF16)
    rw = w_in[:, :, mla_end:rw_end]
    segs = [rw[:, :, :3 * D]]
    mu_segs = [rwkv_mu[:, :3 * D]]
    o = 3 * D
    for width in (DECAY_LORA, DECAY_LORA, AAA_LORA, AAA_LORA):
        segs.append(jnp.pad(rw[:, :, o:o + width], ((0, 0), (0, 0), (0, LORA_PAD - width))))
        mu_segs.append(jnp.pad(rwkv_mu[:, o:o + width], ((0, 0), (0, LORA_PAD - width))))
        o += width
    segs.append(rw[:, :, o:])
    mu_segs.append(rwkv_mu[:, o:])
    w_rw = jnp.concatenate(segs, axis=-1).astype(BF16)
    mu_rw = jnp.concatenate(mu_segs, axis=-1)[:, None, :]
    w_gt = w_in[:, :, rw_end:].astype(BF16)
    w2p = _pad_rows(rwkv_w2, LORA_PAD).astype(BF16)
    a2p = _pad_rows(rwkv_a2, LORA_PAD).astype(BF16)
    g2b = rwkv_g2.astype(BF16)
    wq4 = w_uq.reshape(L, Q_LORA, HEADS, QK_DIM)
    wq_r = wq4[..., NOPE:]
    wq = jnp.concatenate([wq4[..., :NOPE], wq_r, _rot_cols(wq_r)], axis=-1)
    wq = wq.reshape(L, Q_LORA, HEADS * 2 * LANES).astype(BF16)
    wkv4 = w_ukv.reshape(L, KV_LORA, HEADS, NOPE + V_DIM)
    wk = wkv4[..., :NOPE].reshape(L, KV_LORA, D).astype(BF16)
    wv = wkv4[..., NOPE:].reshape(L, KV_LORA, D).astype(BF16)
    ws_b = gmlp_ws.astype(BF16)
    bs_full = jnp.repeat(jnp.swapaxes(gmlp_bs, 1, 2), GMLP_GW, axis=2)
    w_br = w_branch.astype(BF16)
    w_ob = w_o.astype(BF16)
    rw_pad = jnp.pad(router_w, ((0, 0), (0, 0), (0, LANES - N_EXPERTS)))
    rb_pad = jnp.pad(router_b, ((0, 0), (0, LANES - N_EXPERTS)), constant_values=-1e30)[:, None, :]
    bg4 = b_gate[:, :, None, :]
    bu4 = b_up[:, :, None, :]
    bd4 = b_down[:, :, None, :]

    cos, sin = _rope_tables(t_lat)
    cos_all = jnp.concatenate([jnp.ones((n_ctx, ROPE), F32), jnp.tile(cos, (b_lat, 1))], axis=0)
    sin_all = jnp.concatenate([jnp.zeros((n_ctx, ROPE), F32), jnp.tile(sin, (b_lat, 1))], axis=0)
    tabk = jnp.concatenate([cos_all, sin_all], axis=-1)
    tabq = jnp.concatenate([jnp.ones((n, NOPE), F32), tabk], axis=-1)
    tab_cache = jnp.concatenate([jnp.ones((b_lat * past, ROPE), F32), jnp.zeros((b_lat * past, ROPE), F32)], -1)

    zeros_state = jnp.zeros((b_ctx, RHEADS, RH, RH), F32)
    sf_lat = state_rwkv_fwd
    sb_lat = state_rwkv_bwd

    cap = (n * TOP_K // BM + N_EXPERTS) * BM
    x = jnp.concatenate([x_prompt.reshape(n_ctx, D), x_sample.reshape(n_lat, D)], axis=0)
    h = ln_modulate(x, mod_all[0], modrow)
    ckv_out, kr_out, sf_out, sb_out = [], [], [], []
    for l in range(L):
        mod = mod_all[l]
        p_gm = matmul(h, w_gm, l, 1024, "in_proj_gmlp", BF16)
        p_mla = matmul(h, w_mla, l, MLA_COLS, "in_proj_mla")
        p_rw = matmul(h, w_rw, l, 768, "in_proj_rwkv")
        p_gate = matmul(h, w_gt, l, 1024, "in_proj_gate", BF16)

        o_a = gmlp(p_gm, gmlp_ln_g[l][None], gmlp_ln_b[l][None], ws_b[l], bs_full[l])

        q = q_proj(p_mla, q_norm_g[l][None], wq[l], tabq)
        kcat, vv, ckvn = kv_proj(p_mla, 1, p_mla, (Q_LORA + KV_LORA) // LANES, kv_norm_g[l][None],
                                 wk[l], wv[l], tabk, True)
        cache_c = cache_ckv[:, l].reshape(b_lat * past, KV_LORA)
        cache_r = jnp.pad(cache_krope[:, l].reshape(b_lat * past, ROPE), ((0, 0), (0, ROPE)))
        kcat_c, vv_c, _ = kv_proj(cache_c, 0, cache_r, 0, kv_norm_g[l][None], wk[l], wv[l], tab_cache, False)
        o_b_ctx = attention(q, 0, kcat, vv, b_ctx, seq, seq, HEADS, "mla_attention_ctx")
        o_b_lat = attention_lat(q, n_ctx, kcat_c, vv_c, kcat, vv, b_lat, t_lat, past, 2)
        o_b = jnp.concatenate([o_b_ctx, o_b_lat], axis=0)
        ckv_out.append(ckvn[:n_ctx].reshape(b_ctx, seq, KV_LORA))
        kr_out.append(p_mla[:n_ctx, Q_LORA + KV_LORA:Q_LORA + KV_LORA + ROPE].reshape(b_ctx, seq, ROPE))

        (r, v, kk, lwf, lwb, kdf, kdb, bqf, bqb, bonus, gg) = rwkv_prep(
            p_rw, mu_rw[l], rwkv_kk[l][None], rwkv_w0[l], w2p[l], rwkv_a0[l], a2p[l], rwkv_ka[l],
            rwkv_rk[l].reshape(1, D), g2b[l], n_ctx // RB, seq // RB, t_lat // RB)
        yf_c, yb_c, sf_c, sb_c = rwkv_scan(r, v, kk, lwf, lwb, kdf, kdb, bqf, bqb, zeros_state, zeros_state,
                                           0, b_ctx, seq, "rwkv_scan_ctx")
        yf_l, yb_l, _, _ = rwkv_scan(r, v, kk, lwf, lwb, kdf, kdb, bqf, bqb, sf_lat[:, l], sb_lat[:, l],
                                     n_ctx, b_lat, t_lat, "rwkv_scan_lat")
        sf_out.append(sf_c)
        sb_out.append(sb_c)
        o_c = rwkv_finish(yf_c, yb_c, yf_l, yb_l, bonus, gg, rwkv_lnx_g[l][None], rwkv_lnx_b[l][None])

        z = branch_merge(o_a, o_b, o_c, w_br, l, p_gate)
        x1, h2, h2p = out_proj(z, w_ob, l, x, mod, modrow, ln1_g[l][None], ln1_b[l][None])

        idx, wts, hist = router(h2, rw_pad[l], rb_pad[l])
        base, pad_end = moe_layout(hist)
        dest = moe_dest(idx, base)
        dest_flat = dest[:, :TOP_K].reshape(n // TB, 1, TB * TOP_K)
        xb = moe_dispatch(dest_flat, h2p, jnp.zeros((cap, D // 2), jnp.uint32))
        n_valid = (pad_end[N_EXPERTS - 1] // BM).astype(jnp.int32)
        blk_start = jnp.minimum(jnp.arange(cap // BM, dtype=jnp.int32), n_valid - 1) * BM
        block_e = jnp.sum((pad_end[None, :N_EXPERTS] <= blk_start[:, None]).astype(jnp.int32), axis=1)
        block_e = jnp.minimum(block_e, N_EXPERTS - 1)
        nv = n_valid.reshape(1)
        act = expert_up(block_e, nv, xb, w_gate, bg4, w_up, bu4, l)
        yb = expert_down(block_e, nv, act, w_down, bd4, l)
        mod_next = mod_all[min(l + 1, L - 1)]
        x, h = moe_combine(dest_flat, wts, x1, mod, mod_next, modrow, ln2_g[l][None], ln2_b[l][None], yb)

    y_prompt = x[:n_ctx].reshape(b_ctx, seq, D)
    y_sample = x[n_ctx:].reshape(b_lat, t_lat, D)
    return (y_prompt, y_sample, jnp.stack(ckv_out, axis=1), jnp.stack(kr_out, axis=1),
            jnp.stack(sf_out, axis=1), jnp.stack(sb_out, axis=1))
```

```python
import functools
import math

import jax
import jax.numpy as jnp
from jax import lax
from jax.experimental import pallas as pl
from jax.experimental.pallas import tpu as pltpu

F32 = jnp.float32
BF16 = jnp.bfloat16
HI = lax.Precision.HIGHEST

DEPTH_NORM = 4
GRID_W = 64
CHUNK = 128
D = 2048
GMLP_GW = 128
GMLP_GROUPS = D // GMLP_GW
V_DIM = 128
NOPE = 128
ROPE = 64
QK_DIM = NOPE + ROPE
HEADS = D // V_DIM
Q_LORA = 512
KV_LORA = 512
ROPE_THETA = 10000.0
RH = 64
RHEADS = D // RH
DECAY_LORA = 96
AAA_LORA = 96
GATE_LORA = 256
GN_EPS = 64e-5
N_EXPERTS = 32
TOP_K = 4
SWIGLU_LIMIT = 7.0
SWIGLU_ALPHA = 1.702
LN_EPS = 1e-5
RMS_EPS = 1e-6
ALPHA = (2 * DEPTH_NORM) ** 0.25

LANES = 128
VMEM_LIMIT = 48 * 1024 * 1024

TB = 256
RB = 128
SC = 64
HG = 32
BM = 512
EXPERT_TN = 1024
EXPERT_UP_TN = 1024
EXPERT_VMEM_LIMIT = 56 * 1024 * 1024
DMA_UNROLL = 8
LORA_PAD = 128
RW_COLS = 3 * D + 4 * LORA_PAD + GATE_LORA
MLA_COLS = Q_LORA + KV_LORA + 2 * ROPE


def _cp(*sem):
    return pltpu.CompilerParams(dimension_semantics=sem, vmem_limit_bytes=VMEM_LIMIT)


def _ln(x):
    mu = jnp.mean(x, -1, keepdims=True)
    xc = x - mu
    var = jnp.mean(xc * xc, -1, keepdims=True)
    return xc * lax.rsqrt(var + LN_EPS)


def _sigmoid(x):
    return 1.0 / (1.0 + jnp.exp(-x))


def _gelu_tanh(x):
    return 0.5 * x * (1.0 + jnp.tanh(math.sqrt(2.0 / math.pi) * (x + 0.044715 * (x * x * x))))


def _mod_kernel(c_ref, w_ref, b_ref, o_ref):
    c = c_ref[...]
    s = (c * _sigmoid(c)).astype(BF16)
    o_ref[...] = jnp.dot(s, w_ref[...].astype(BF16), preferred_element_type=F32) + b_ref[...]


def modulation(cond8, w_mod, b_mod):
    L = w_mod.shape[0]
    tn = 1024
    return pl.pallas_call(
        _mod_kernel,
        grid=(L, 6 * D // tn),
        in_specs=[pl.BlockSpec((8, D), lambda l, j: (0, 0)),
                  pl.BlockSpec((None, D, tn), lambda l, j: (l, 0, j)),
                  pl.BlockSpec((None, 1, tn), lambda l, j: (l, 0, j))],
        out_specs=pl.BlockSpec((None, 8, tn), lambda l, j: (l, 0, j)),
        out_shape=jax.ShapeDtypeStruct((L, 8, 6 * D), F32),
        compiler_params=_cp("arbitrary", "arbitrary"),
        name="modulation",
    )(cond8, w_mod, b_mod.reshape(L, 1, 6 * D))


def _lnmod_kernel(x_ref, mod_ref, h_ref):
    m = mod_ref[...]
    h_ref[...] = (_ln(x_ref[...]) * (1.0 + m[1:2]) + m[0:1]).astype(BF16)


def ln_modulate(x, mod, modrow):
    n = x.shape[0]
    return pl.pallas_call(
        _lnmod_kernel,
        grid=(n // TB,),
        in_specs=[pl.BlockSpec((TB, D), lambda i: (i, 0)),
                  pl.BlockSpec((None, 6, D), lambda i: (modrow(i), 0, 0))],
        out_specs=pl.BlockSpec((TB, D), lambda i: (i, 0)),
        out_shape=jax.ShapeDtypeStruct((n, D), BF16),
        compiler_params=_cp("arbitrary"),
        name="ln_modulate",
    )(x, mod)


def _mm_kernel(x_ref, w_ref, o_ref):
    o_ref[...] = jnp.dot(x_ref[...], w_ref[...], preferred_element_type=F32).astype(o_ref.dtype)


def matmul(x, w, layer, tn, name, out_dtype=F32):
    m, k = x.shape
    nn = w.shape[2]
    tm = 512 if m % 512 == 0 else TB
    return pl.pallas_call(
        _mm_kernel,
        grid=(nn // tn, m // tm),
        in_specs=[pl.BlockSpec((tm, k), lambda j, i: (i, 0)),
                  pl.BlockSpec((None, k, tn), lambda j, i: (layer, 0, j))],
        out_specs=pl.BlockSpec((tm, tn), lambda j, i: (i, j)),
        out_shape=jax.ShapeDtypeStruct((m, nn), out_dtype),
        compiler_params=_cp("arbitrary", "arbitrary"),
        name=name,
    )(x, w)


def _gmlp_kernel(p_ref, g_ref, b_ref, ws_ref, bs_ref, o_ref):
    u = _gelu_tanh(p_ref[:, :D].astype(F32))
    v = _gelu_tanh(p_ref[:, D:].astype(F32))
    v = (_ln(v) * g_ref[...] + b_ref[...]).astype(BF16)
    for g in range(GMLP_GROUPS):
        sl = slice(g * GMLP_GW, (g + 1) * GMLP_GW)
        mixed = jnp.dot(ws_ref[g], v[:, sl], preferred_element_type=F32) + bs_ref[:, sl]
        o_ref[:, sl] = (u[:, sl] * mixed).astype(BF16)


def gmlp(p_gm, ln_g, ln_b, ws, bs_full):
    n = p_gm.shape[0]
    return pl.pallas_call(
        _gmlp_kernel,
        grid=(n // CHUNK,),
        in_specs=[pl.BlockSpec((CHUNK, 2 * D), lambda i: (i, 0)),
                  pl.BlockSpec((1, D), lambda i: (0, 0)),
                  pl.BlockSpec((1, D), lambda i: (0, 0)),
                  pl.BlockSpec((GMLP_GROUPS, CHUNK, CHUNK), lambda i: (0, 0, 0)),
                  pl.BlockSpec((CHUNK, D), lambda i: (0, 0))],
        out_specs=pl.BlockSpec((CHUNK, D), lambda i: (i, 0)),
        out_shape=jax.ShapeDtypeStruct((n, D), BF16),
        compiler_params=_cp("arbitrary"),
        name="gmlp",
    )(p_gm, ln_g, ln_b, ws, bs_full)


def _qproj_kernel(cq_ref, g_ref, w_ref, tab_ref, q_ref):
    cq = cq_ref[...]
    n = (cq * lax.rsqrt(jnp.mean(cq * cq, -1, keepdims=True) + RMS_EPS) * g_ref[...]).astype(BF16)
    tab = tab_ref[...]
    hw = 2 * LANES
    for h in range(HEADS):
        sl = slice(h * hw, (h + 1) * hw)
        q_ref[:, sl] = (jnp.dot(n, w_ref[:, sl], preferred_element_type=F32) * tab).astype(BF16)


def q_proj(p_mla, g, wq, tabq):
    n = p_mla.shape[0]
    hw = 2 * LANES
    return pl.pallas_call(
        _qproj_kernel,
        grid=(n // TB,),
        in_specs=[pl.BlockSpec((TB, Q_LORA), lambda i: (i, 0)),
                  pl.BlockSpec((1, Q_LORA), lambda i: (0, 0)),
                  pl.BlockSpec((Q_LORA, HEADS * hw), lambda i: (0, 0)),
                  pl.BlockSpec((TB, hw), lambda i: (i, 0))],
        out_specs=pl.BlockSpec((TB, HEADS * hw), lambda i: (i, 0)),
        out_shape=jax.ShapeDtypeStruct((n, HEADS * hw), BF16),
        compiler_params=_cp("arbitrary"),
        name="mla_q_proj",
    )(p_mla, g, wq, tabq)


def _kvproj_kernel(ckv_ref, kr_ref, g_ref, wk_ref, wv_ref, tab_ref, kcat_ref, v_ref, ckvn_ref, *, normalize):
    ckv = ckv_ref[...]
    if normalize:
        ckv = ckv * lax.rsqrt(jnp.mean(ckv * ckv, -1, keepdims=True) + RMS_EPS) * g_ref[...]
    ckvn_ref[...] = ckv
    nb = ckv.astype(BF16)
    t = kr_ref[...] * tab_ref[...]
    khi = (t + pltpu.roll(t, ROPE, axis=1)).astype(BF16)
    kn = jnp.dot(nb, wk_ref[...], preferred_element_type=F32).astype(BF16)
    v_ref[...] = jnp.dot(nb, wv_ref[...], preferred_element_type=F32).astype(BF16)
    hw = 2 * LANES
    for h in range(HEADS):
        kcat_ref[:, h * hw:h * hw + LANES] = kn[:, h * LANES:(h + 1) * LANES]
        kcat_ref[:, h * hw + LANES:(h + 1) * hw] = khi


def kv_proj(ckv_src, ckv_col, kr_src, kr_col, g, wk, wv, tabk, normalize):
    n = ckv_src.shape[0]
    hw = 2 * LANES
    return pl.pallas_call(
        functools.partial(_kvproj_kernel, normalize=normalize),
        grid=(n // TB,),
        in_specs=[pl.BlockSpec((TB, KV_LORA), lambda i: (i, ckv_col)),
                  pl.BlockSpec((TB, LANES), lambda i: (i, kr_col)),
                  pl.BlockSpec((1, KV_LORA), lambda i: (0, 0)),
                  pl.BlockSpec((KV_LORA, D), lambda i: (0, 0)),
                  pl.BlockSpec((KV_LORA, D), lambda i: (0, 0)),
                  pl.BlockSpec((TB, LANES), lambda i: (i, 0))],
        out_specs=[pl.BlockSpec((TB, HEADS * hw), lambda i: (i, 0)),
                   pl.BlockSpec((TB, D), lambda i: (i, 0)),
                   pl.BlockSpec((TB, KV_LORA), lambda i: (i, 0))],
        out_shape=[jax.ShapeDtypeStruct((n, HEADS * hw), BF16),
                   jax.ShapeDtypeStruct((n, D), BF16),
                   jax.ShapeDtypeStruct((n, KV_LORA), F32)],
        compiler_params=_cp("arbitrary"),
        name="mla_kv_proj",
    )(ckv_src, kr_src, g, wk, wv, tabk)


def _attn_kernel(q_ref, k_ref, v_ref, o_ref, *, nh):
    hw = 2 * LANES
    for h in range(nh):
        s = lax.dot_general(q_ref[:, h * hw:(h + 1) * hw], k_ref[:, h * hw:(h + 1) * hw],
                            (((1,), (1,)), ((), ())), preferred_element_type=F32)
        s = s * (QK_DIM ** -0.5)
        e = jnp.exp(s - jnp.max(s, -1, keepdims=True))
        p = (e * (1.0 / jnp.sum(e, -1, keepdims=True))).astype(BF16)
        o_ref[:, h * V_DIM:(h + 1) * V_DIM] = jnp.dot(
            p, v_ref[:, h * V_DIM:(h + 1) * V_DIM], preferred_element_type=F32).astype(BF16)


def _attn_lat_kernel(q_ref, kc_ref, vc_ref, kn_ref, vn_ref, o_ref, *, nh, past):
    hw = 2 * LANES
    nt = (((1,), (1,)), ((), ()))
    for h in range(nh):
        q = q_ref[:, h * hw:(h + 1) * hw]
        s = jnp.concatenate(
            [lax.dot_general(q, kc_ref[:, h * hw:(h + 1) * hw], nt, preferred_element_type=F32),
             lax.dot_general(q, kn_ref[:, h * hw:(h + 1) * hw], nt, preferred_element_type=F32)], axis=1)
        s = s * (QK_DIM ** -0.5)
        e = jnp.exp(s - jnp.max(s, -1, keepdims=True))
        p = (e * (1.0 / jnp.sum(e, -1, keepdims=True))).astype(BF16)
        vs = slice(h * V_DIM, (h + 1) * V_DIM)
        o_ref[:, vs] = (jnp.dot(p[:, :past], vc_ref[:, vs], preferred_element_type=F32)
                        + jnp.dot(p[:, past:], vn_ref[:, vs], preferred_element_type=F32)).astype(BF16)


def attention_lat(q, row0, kcat_c, v_c, kcat, v, n_seq, t_q, past, nh):
    hw = 2 * LANES
    qb = t_q // TB
    qoff = row0 // TB
    koff = row0 // t_q
    return pl.pallas_call(
        functools.partial(_attn_lat_kernel, nh=nh, past=past),
        grid=(n_seq, HEADS // nh, qb),
        in_specs=[pl.BlockSpec((TB, nh * hw), lambda s, h, i: (qoff + s * qb + i, h)),
                  pl.BlockSpec((past, nh * hw), lambda s, h, i: (s, h)),
                  pl.BlockSpec((past, nh * V_DIM), lambda s, h, i: (s, h)),
                  pl.BlockSpec((t_q, nh * hw), lambda s, h, i: (koff + s, h)),
                  pl.BlockSpec((t_q, nh * V_DIM), lambda s, h, i: (koff + s, h))],
        out_specs=pl.BlockSpec((TB, nh * V_DIM), lambda s, h, i: (s * qb + i, h)),
        out_shape=jax.ShapeDtypeStruct((n_seq * t_q, D), BF16),
        compiler_params=_cp("arbitrary", "arbitrary", "arbitrary"),
        name="mla_attention_lat",
    )(q, kcat_c, v_c, kcat, v)


def attention(q, q_row0, kcat, v, n_seq, t_q, n_keys, nh, name):
    hw = 2 * LANES
    qb = t_q // TB
    qoff = q_row0 // TB
    return pl.pallas_call(
        functools.partial(_attn_kernel, nh=nh),
        grid=(n_seq, HEADS // nh, qb),
        in_specs=[pl.BlockSpec((TB, nh * hw), lambda s, h, i: (qoff + s * qb + i, h)),
                  pl.BlockSpec((n_keys, nh * hw), lambda s, h, i: (s, h)),
                  pl.BlockSpec((n_keys, nh * V_DIM), lambda s, h, i: (s, h))],
        out_specs=pl.BlockSpec((TB, nh * V_DIM), lambda s, h, i: (s * qb + i, h)),
        out_shape=jax.ShapeDtypeStruct((n_seq * t_q, D), BF16),
        compiler_params=_cp("arbitrary", "arbitrary", "arbitrary"),
        name=name,
    )(q, kcat, v)


def _head_ones():
    r = lax.broadcasted_iota(jnp.int32, (LANES, LANES), 0) >= RH
    c = lax.broadcasted_iota(jnp.int32, (LANES, LANES), 1) >= RH
    return (r == c).astype(BF16)


def _head_sum(x, ones_bd):
    hi, lo = _split_bf16(x)
    parts = [jnp.dot(hi[:, i * LANES:(i + 1) * LANES], ones_bd, preferred_element_type=F32)
             + jnp.dot(lo[:, i * LANES:(i + 1) * LANES], ones_bd, preferred_element_type=F32)
             for i in range(x.shape[1] // LANES)]
    return jnp.concatenate(parts, axis=1)


def _rwkv_prep_kernel(p_ref, prev_ref, next_ref, mu_ref, kkw_ref, w0_ref, w2_ref, a0_ref, a2_ref, ka_ref,
                      rk_ref, g2_ref,
                      r_o, v_o, kk_o, lwf_o, lwb_o, kdf_o, kdb_o, bqf_o, bqb_o, bonus_o, g_o,
                      *, nb_ctx, bps_ctx, bps_lat):
    i = pl.program_id(0)
    is_lat = i >= nb_ctx
    pos = jnp.where(is_lat, lax.rem(i - nb_ctx, bps_lat), lax.rem(i, bps_ctx))
    last = jnp.where(is_lat, bps_lat - 1, bps_ctx - 1)
    has_prev = (pos != 0).astype(F32)
    has_next = (pos != last).astype(F32)
    rows = lax.broadcasted_iota(jnp.int32, (RB, 1), 0)

    def shifted(lo, hi):
        p = p_ref[:, lo:hi]
        prev = jnp.where(rows == 0, prev_ref[7:8, lo:hi] * has_prev, pltpu.roll(p, 1, axis=0))
        nxt = jnp.where(rows == RB - 1, next_ref[0:1, lo:hi] * has_next, pltpu.roll(p, RB - 1, axis=0))
        return p + (0.5 * (prev + nxt) - p) * mu_ref[:, lo:hi]

    ones_bd = _head_ones()
    r = shifted(0, D)
    k = shifted(D, 2 * D)
    v = shifted(2 * D, 3 * D)
    r_o[...] = r
    v_o[...] = v
    kkf = k * kkw_ref[...]
    nrm = jnp.sqrt(_head_sum(kkf * kkf, ones_bd))
    kk = kkf / jnp.maximum(nrm, 1e-12)
    kk_o[...] = kk
    bonus_o[...] = _head_sum(r * k * rk_ref[...], ones_bd) * v
    base = 3 * D
    outs = ((lwf_o, kdf_o, bqf_o), (lwb_o, kdb_o, bqb_o))
    for d in range(2):
        wl = shifted(base + d * LORA_PAD, base + (d + 1) * LORA_PAD)
        z = w0_ref[d:d + 1, :] + jnp.dot(jnp.tanh(wl).astype(BF16), w2_ref[d], preferred_element_type=F32)
        y = -z
        softplus = jnp.maximum(y, 0.0) + jnp.log(1.0 + jnp.exp(-jnp.abs(y)))
        logw = -softplus - 0.5
        outs[d][0][...] = -jnp.exp(logw)
        al = shifted(base + (2 + d) * LORA_PAD, base + (3 + d) * LORA_PAD)
        a = _sigmoid(a0_ref[d:d + 1, :] + jnp.dot(al.astype(BF16), a2_ref[d], preferred_element_type=F32))
        outs[d][1][...] = k * (1.0 + (a - 1.0) * ka_ref[d:d + 1, :])
        outs[d][2][...] = kk * a
    gl = shifted(base + 4 * LORA_PAD, base + 4 * LORA_PAD + GATE_LORA)
    g_o[...] = jnp.dot(_sigmoid(gl).astype(BF16), g2_ref[...], preferred_element_type=F32)


def rwkv_prep(p_rw, mu, kkw, w0, w2, a0, a2, ka, rk, g2, nb_ctx, bps_ctx, bps_lat):
    n = p_rw.shape[0]
    nb = n // RB
    sub = RB // 8
    row = lambda i: (i, 0)
    const2 = lambda i: (0, 0)
    out = jax.ShapeDtypeStruct((n, D), F32)
    return pl.pallas_call(
        functools.partial(_rwkv_prep_kernel, nb_ctx=nb_ctx, bps_ctx=bps_ctx, bps_lat=bps_lat),
        grid=(nb,),
        in_specs=[pl.BlockSpec((RB, RW_COLS), row),
                  pl.BlockSpec((8, RW_COLS), lambda i: (jnp.maximum(i * sub - 1, 0), 0)),
                  pl.BlockSpec((8, RW_COLS), lambda i: (jnp.minimum((i + 1) * sub, nb * sub - 1), 0)),
                  pl.BlockSpec((1, RW_COLS), const2),
                  pl.BlockSpec((1, D), const2),
                  pl.BlockSpec((2, D), const2),
                  pl.BlockSpec((2, LORA_PAD, D), lambda i: (0, 0, 0)),
                  pl.BlockSpec((2, D), const2),
                  pl.BlockSpec((2, LORA_PAD, D), lambda i: (0, 0, 0)),
                  pl.BlockSpec((2, D), const2),
                  pl.BlockSpec((1, D), const2),
                  pl.BlockSpec((GATE_LORA, D), const2)],
        out_specs=[pl.BlockSpec((RB, D), row)] * 11,
        out_shape=[out] * 11,
        compiler_params=_cp("arbitrary"),
        name="rwkv_prep",
    )(p_rw, p_rw, p_rw, mu, kkw, w0, w2, a0, a2, ka, rk, g2)


def _mm(eq, a, b):
    return jnp.einsum(eq, a.astype(BF16), b.astype(BF16), preferred_element_type=F32)


def _split_bf16(x):
    hi = x.astype(BF16)
    return hi, (x - hi.astype(F32)).astype(BF16)


def _mm3(eq, a, b):
    ah, al = _split_bf16(a)
    bh, bl = _split_bf16(b)
    mm = lambda x, y: jnp.einsum(eq, x, y, preferred_element_type=F32)
    return mm(ah, bh) + (mm(ah, bl) + mm(al, bh))


_NN = "hmk,hkn->hmn"
_NT = "hmk,hnk->hmn"
_TN = "hkm,hkn->hmn"


def _scan_sum(x, reverse):
    rows = lax.broadcasted_iota(jnp.int32, (SC, 1), 0)
    shift = 1
    while shift < SC:
        if reverse:
            x = x + jnp.where(rows < SC - shift, pltpu.roll(x, SC - shift, axis=0), 0.0)
        else:
            x = x + jnp.where(rows >= shift, pltpu.roll(x, shift, axis=0), 0.0)
        shift *= 2
    return x


def _wkv_chunk(refs, s, reverse):
    r, v, kk, lw, kd, bq = (x[...] for x in refs)
    last = 0 if reverse else SC - 1
    L = _scan_sum(lw, reverse)
    ltot = L[last:last + 1, :]
    en = jnp.exp(-L)
    et = jnp.exp(ltot - L)

    def heads(x):
        return jnp.stack([x[:, h * RH:(h + 1) * RH] for h in range(HG)], axis=0)

    ax = heads(-kk * jnp.exp(L - lw))
    rt = heads(r * jnp.exp(L))
    bt = heads(bq * en)
    kt = heads(kd * en)
    bc = heads(bq * et)
    kc = heads(kd * et)
    ptot = heads(jnp.exp(ltot))
    v = heads(v)

    ti = lax.broadcasted_iota(jnp.int32, (SC, SC), 0)
    si = lax.broadcasted_iota(jnp.int32, (SC, SC), 1)
    strict = (si > ti) if reverse else (si < ti)
    incl = (si >= ti) if reverse else (si <= ti)
    a_ab = jnp.where(strict, _mm3(_NT, ax, bt), 0.0)
    a_ak = jnp.where(strict, _mm3(_NT, ax, kt), 0.0)
    a_rb = jnp.where(incl, _mm(_NT, rt, bt), 0.0)
    a_rk = jnp.where(incl, _mm(_NT, rt, kt), 0.0)
    npow = a_ab
    e = a_ab
    for _ in range(int(math.log2(SC)) - 2):
        npow = _mm(_NN, npow, npow)
        e = e + npow + _mm(_NN, e, npow)
    ws = _mm(_NT, jnp.concatenate([ax, rt], axis=1), s)
    w = ws[:, :SC] + _mm(_NN, a_ak, v)
    u = w + _mm(_NN, e, w)
    rho = (w - u) + _mm3(_NN, a_ab, u)
    u = u + rho + _mm(_NN, e, rho)
    y = ws[:, SC:] + _mm(_NN, a_rb, u) + _mm(_NN, a_rk, v)
    s_new = s * ptot + _mm3(_TN, jnp.concatenate([u, v], axis=1), jnp.concatenate([bc, kc], axis=1))
    return jnp.concatenate([y[h] for h in range(HG)], axis=1), s_new


def _scan_kernel(rf, vf, kkf, lwf, kdf, bqf, rb, vb, kkb, lwb, kdb, bqb, s0f, s0b,
                 yf_ref, yb_ref, sfo, sbo, stf, stb, *, nc):
    c = pl.program_id(2)

    @pl.when(c == 0)
    def _():
        stf[...] = s0f[...]
        stb[...] = s0b[...]

    for reverse, refs, st, yref in ((False, (rf, vf, kkf, lwf, kdf, bqf), stf, yf_ref),
                                    (True, (rb, vb, kkb, lwb, kdb, bqb), stb, yb_ref)):
        y, s_new = _wkv_chunk(refs, st[...], reverse)
        st[...] = s_new
        yref[...] = y

    @pl.when(c == nc - 1)
    def _():
        sfo[...] = stf[...]
        sbo[...] = stb[...]


def rwkv_scan(r, v, kk, lwf, lwb, kdf, kdb, bqf, bqb, s0f, s0b, row0, n_seq, t_len, name):
    nc = t_len // SC
    off = row0 // SC
    cw = HG * RH
    fwd = lambda s, h, c: (off + s * nc + c, h)
    bwd = lambda s, h, c: (off + s * nc + (nc - 1 - c), h)
    blk = lambda im: pl.BlockSpec((SC, cw), im)
    st_spec = pl.BlockSpec((None, HG, RH, RH), lambda s, h, c: (s, h, 0, 0))
    n_rows = n_seq * t_len
    return pl.pallas_call(
        functools.partial(_scan_kernel, nc=nc),
        grid=(n_seq, RHEADS // HG, nc),
        in_specs=[blk(fwd)] * 6 + [blk(bwd)] * 6 + [st_spec, st_spec],
        out_specs=[pl.BlockSpec((SC, cw), lambda s, h, c: (s * nc + c, h)),
                   pl.BlockSpec((SC, cw), lambda s, h, c: (s * nc + (nc - 1 - c), h)),
                   st_spec, st_spec],
        out_shape=[jax.ShapeDtypeStruct((n_rows, D), F32), jax.ShapeDtypeStruct((n_rows, D), F32),
                   jax.ShapeDtypeStruct(s0f.shape, F32), jax.ShapeDtypeStruct(s0b.shape, F32)],
        scratch_shapes=[pltpu.VMEM((HG, RH, RH), F32), pltpu.VMEM((HG, RH, RH), F32)],
        compiler_params=_cp("arbitrary", "arbitrary", "arbitrary"),
        name=name,
    )(r, v, kk, lwf, kdf, bqf, r, v, kk, lwb, kdb, bqb, s0f, s0b)


def _rwkv_fin_kernel(yfc_ref, ybc_ref, yfl_ref, ybl_ref, bonus_ref, g_ref, lg_ref, lb_ref, o_ref, *, nb_ctx):
    ones_bd = _head_ones()
    is_lat = pl.program_id(0) >= nb_ctx
    y = jnp.where(is_lat, yfl_ref[...] + ybl_ref[...], yfc_ref[...] + ybc_ref[...])
    mu = _head_sum(y, ones_bd) * (1.0 / RH)
    yc = y - mu
    var = _head_sum(yc * yc, ones_bd) * (1.0 / RH)
    yn = yc * lax.rsqrt(var + GN_EPS) * lg_ref[...] + lb_ref[...]
    o_ref[...] = ((yn + bonus_ref[...]) * g_ref[...]).astype(BF16)


def rwkv_finish(yf_c, yb_c, yf_l, yb_l, bonus, g, lnx_g, lnx_b):
    n = bonus.shape[0]
    nb_ctx = yf_c.shape[0] // TB
    row = lambda i: (i, 0)
    const2 = lambda i: (0, 0)
    ctx = pl.BlockSpec((TB, D), lambda i: (jnp.minimum(i, nb_ctx - 1), 0))
    lat = pl.BlockSpec((TB, D), lambda i: (jnp.maximum(i - nb_ctx, 0), 0))
    return pl.pallas_call(
        functools.partial(_rwkv_fin_kernel, nb_ctx=nb_ctx),
        grid=(n // TB,),
        in_specs=[ctx, ctx, lat, lat] + [pl.BlockSpec((TB, D), row)] * 2 + [pl.BlockSpec((1, D), const2)] * 2,
        out_specs=pl.BlockSpec((TB, D), row),
        out_shape=jax.ShapeDtypeStruct((n, D), BF16),
        compiler_params=_cp("arbitrary"),
        name="rwkv_finish",
    )(yf_c, yb_c, yf_l, yb_l, bonus, g, lnx_g, lnx_b)


def _merge_kernel(oa_ref, ob_ref, oc_ref, w_ref, ga_ref, gb_ref, gc_ref, z_ref):
    acc = None
    for n, (o_ref, g_ref) in enumerate(((oa_ref, ga_ref), (ob_ref, gb_ref), (oc_ref, gc_ref))):
        y = _sigmoid(g_ref[...].astype(F32)) * jnp.dot(o_ref[...], w_ref[n], preferred_element_type=F32)
        acc = y if acc is None else acc + y
    z_ref[...] = acc.astype(BF16)


def branch_merge(o_a, o_b, o_c, w_branch, layer, p_gate):
    n = o_a.shape[0]
    tn = 512
    tm = 512 if n % 512 == 0 else TB
    nj = D // tn
    o_spec = pl.BlockSpec((tm, D), lambda j, i: (i, 0))
    gate = lambda b: pl.BlockSpec((tm, tn), lambda j, i: (i, b * nj + j))
    return pl.pallas_call(
        _merge_kernel,
        grid=(nj, n // tm),
        in_specs=[o_spec, o_spec, o_spec,
                  pl.BlockSpec((None, 3, D, tn), lambda j, i: (layer, 0, 0, j)),
                  gate(0), gate(1), gate(2)],
        out_specs=pl.BlockSpec((tm, tn), lambda j, i: (i, j)),
        out_shape=jax.ShapeDtypeStruct((n, D), BF16),
        compiler_params=_cp("arbitrary", "arbitrary"),
        name="branch_merge",
    )(o_a, o_b, o_c, w_branch, p_gate, p_gate, p_gate)


def _bf16_bits(x):
    return lax.bitcast_convert_type(x.astype(BF16).astype(F32), jnp.uint32)


def _pack_rows(x):
    return (_bf16_bits(x[:, :D // 2]) >> 16) | _bf16_bits(x[:, D // 2:])


def _unpack_rows(p):
    lo = lax.bitcast_convert_type(p << 16, F32)
    hi = lax.bitcast_convert_type(p & jnp.uint32(0xFFFF0000), F32)
    return jnp.concatenate([lo, hi], axis=1).astype(BF16)


def _oproj_kernel(z_ref, wo_ref, x_ref, mod_ref, g_ref, b_ref, x1_ref, h2_ref, h2p_ref):
    mix = jnp.dot(z_ref[...], wo_ref[...], preferred_element_type=F32)
    m = mod_ref[...]
    x1 = _ln(ALPHA * x_ref[...] + m[2:3] * mix) * g_ref[...] + b_ref[...]
    x1_ref[...] = x1
    h2 = _ln(x1) * (1.0 + m[4:5]) + m[3:4]
    h2_ref[...] = h2
    h2p_ref[...] = _pack_rows(h2)


def out_proj(z, w_o, layer, x, mod, modrow, ln_g, ln_b):
    n = z.shape[0]
    row = lambda i: (i, 0)
    const2 = lambda i: (0, 0)
    return pl.pallas_call(
        _oproj_kernel,
        grid=(n // TB,),
        in_specs=[pl.BlockSpec((TB, D), row),
                  pl.BlockSpec((None, D, D), lambda i: (layer, 0, 0)),
                  pl.BlockSpec((TB, D), row),
                  pl.BlockSpec((None, 6, D), lambda i: (modrow(i), 0, 0)),
                  pl.BlockSpec((1, D), const2),
                  pl.BlockSpec((1, D), const2)],
        out_specs=[pl.BlockSpec((TB, D), row), pl.BlockSpec((TB, D), row), pl.BlockSpec((TB, D // 2), row)],
        out_shape=[jax.ShapeDtypeStruct((n, D), F32), jax.ShapeDtypeStruct((n, D), F32),
                   jax.ShapeDtypeStruct((n, D // 2), jnp.uint32)],
        compiler_params=_cp("arbitrary"),
        name="out_proj_ln",
    )(z, w_o, x, mod, ln_g, ln_b)


def _router_kernel(h_ref, w_ref, b_ref, idx_ref, wts_ref, hist_ref):
    logits = jnp.dot(h_ref[...], w_ref[...], precision=HI, preferred_element_type=F32) + b_ref[...]
    lane = lax.broadcasted_iota(jnp.int32, (TB, LANES), 1).astype(F32)
    cur = logits
    idx_out = jnp.zeros((TB, LANES), F32)
    val_out = jnp.zeros((TB, LANES), F32)
    hist = jnp.zeros((TB, LANES), F32)
    top = None
    for k in range(TOP_K):
        m = jnp.max(cur, -1, keepdims=True)
        am = jnp.min(jnp.where(cur == m, lane, float(LANES)), -1, keepdims=True)
        sel = lane == am
        top = m if top is None else top
        idx_out = jnp.where(lane == k, am, idx_out)
        val_out = jnp.where(lane == k, jnp.exp(m - top), val_out)
        hist = hist + sel.astype(F32)
        cur = jnp.where(sel, -jnp.inf, cur)
    idx_ref[...] = idx_out.astype(jnp.int32)
    wts_ref[...] = val_out * (1.0 / jnp.sum(val_out, -1, keepdims=True))
    hist_ref[...] = jnp.sum(hist, 0, keepdims=True)


def router(h2, rw_pad, rb_pad):
    n = h2.shape[0]
    nb = n // TB
    return pl.pallas_call(
        _router_kernel,
        grid=(nb,),
        in_specs=[pl.BlockSpec((TB, D), lambda i: (i, 0)),
                  pl.BlockSpec((D, LANES), lambda i: (0, 0)),
                  pl.BlockSpec((1, LANES), lambda i: (0, 0))],
        out_specs=[pl.BlockSpec((TB, LANES), lambda i: (i, 0)),
                   pl.BlockSpec((TB, LANES), lambda i: (i, 0)),
                   pl.BlockSpec((None, 1, LANES), lambda i: (i, 0, 0))],
        out_shape=[jax.ShapeDtypeStruct((n, LANES), jnp.int32),
                   jax.ShapeDtypeStruct((n, LANES), F32),
                   jax.ShapeDtypeStruct((nb, 1, LANES), F32)],
        compiler_params=_cp("arbitrary"),
        name="moe_router",
    )(h2, rw_pad, rb_pad)


def _dest_kernel(idx_ref, base_ref, dest_ref):
    lane = lax.broadcasted_iota(jnp.int32, (TB, LANES), 1).astype(F32)
    ri = lax.broadcasted_iota(jnp.int32, (TB, TB), 0)
    ci = lax.broadcasted_iota(jnp.int32, (TB, TB), 1)
    lower = (ci < ri).astype(BF16)
    idx = idx_ref[...].astype(F32)
    run = base_ref[...]
    dest = jnp.zeros((TB, LANES), F32)
    for k in range(TOP_K):
        e_k = jnp.sum(jnp.where(lane == k, idx, 0), -1, keepdims=True)
        onehot = lane == e_k
        oh = onehot.astype(F32)
        before = jnp.dot(lower, oh.astype(BF16), preferred_element_type=F32)
        d_k = jnp.sum(jnp.where(onehot, run + before, 0.0), -1, keepdims=True)
        dest = jnp.where(lane == k, d_k, dest)
        run = run + jnp.sum(oh, 0, keepdims=True)
    dest_ref[...] = dest.astype(jnp.int32)


def moe_dest(idx, base):
    n = idx.shape[0]
    nb = n // TB
    return pl.pallas_call(
        _dest_kernel,
        grid=(nb,),
        in_specs=[pl.BlockSpec((TB, LANES), lambda i: (i, 0)),
                  pl.BlockSpec((None, 1, LANES), lambda i: (i, 0, 0))],
        out_specs=pl.BlockSpec((TB, LANES), lambda i: (i, 0)),
        out_shape=jax.ShapeDtypeStruct((n, LANES), jnp.int32),
        compiler_params=_cp("arbitrary"),
        name="moe_dest",
    )(idx, base)


def _row_copy(src, src_row, dst, dst_row, sem):
    return pltpu.make_async_copy(src.at[pl.ds(src_row, 1)], dst.at[pl.ds(dst_row, 1)], sem)


def _dispatch_kernel(dest_ref, h_ref, xb_in, xb_out, sem):
    del xb_in

    def issue(t, carry):
        for k in range(TOP_K):
            _row_copy(h_ref, t, xb_out, dest_ref[0, t * TOP_K + k], sem).start()
        return carry

    lax.fori_loop(0, TB, issue, 0, unroll=DMA_UNROLL)

    def drain(t, carry):
        for k in range(TOP_K):
            _row_copy(h_ref, t, xb_out, dest_ref[0, t * TOP_K + k], sem).wait()
        return carry

    lax.fori_loop(0, TB, drain, 0, unroll=DMA_UNROLL)


def moe_dispatch(dest_flat, h2, xb_init):
    n = h2.shape[0]
    nb = n // TB
    return pl.pallas_call(
        _dispatch_kernel,
        grid=(nb,),
        in_specs=[pl.BlockSpec((None, 1, TB * TOP_K), lambda i: (i, 0, 0), memory_space=pltpu.SMEM),
                  pl.BlockSpec((TB, D // 2), lambda i: (i, 0)),
                  pl.BlockSpec(memory_space=pl.ANY)],
        out_specs=pl.BlockSpec(memory_space=pl.ANY),
        out_shape=jax.ShapeDtypeStruct(xb_init.shape, xb_init.dtype),
        scratch_shapes=[pltpu.SemaphoreType.DMA(())],
        input_output_aliases={2: 0},
        compiler_params=_cp("arbitrary"),
        name="moe_dispatch",
    )(dest_flat, h2, xb_init)


def _expert_changed(be_ref):
    i = pl.program_id(1)
    return jnp.logical_or(i == 0, be_ref[i] != be_ref[jnp.maximum(i - 1, 0)])


def _expert_up_kernel(be_ref, nv_ref, x_ref, wg_ref, bg_ref, wu_ref, bu_ref, act_ref, wg_bf, wu_bf):
    used = pl.program_id(1) < nv_ref[0]

    @pl.when(jnp.logical_not(used))
    def _():
        act_ref[...] = jnp.zeros_like(act_ref)

    @pl.when(jnp.logical_and(used, _expert_changed(be_ref)))
    def _():
        wg_bf[...] = wg_ref[...].astype(BF16)
        wu_bf[...] = wu_ref[...].astype(BF16)

    @pl.when(used)
    def _():
        half = BM // 2
        for r in range(2):
            rows = slice(r * half, (r + 1) * half)
            xb = _unpack_rows(x_ref[rows, :])
            gl = jnp.minimum(jnp.dot(xb, wg_bf[...], preferred_element_type=F32) + bg_ref[...], SWIGLU_LIMIT)
            lin = jnp.clip(jnp.dot(xb, wu_bf[...], preferred_element_type=F32) + bu_ref[...],
                           -SWIGLU_LIMIT, SWIGLU_LIMIT)
            act_ref[rows, :] = (gl * _sigmoid(SWIGLU_ALPHA * gl) * (lin + 1.0)).astype(BF16)


def expert_up(block_e, n_valid, xb, w_gate, b_gate, w_up, b_up, layer):
    cap = xb.shape[0]
    tn = EXPERT_UP_TN
    rowblk = lambda j, i, be, nv: (jnp.minimum(i, nv[0] - 1), 0)
    wspec = pl.BlockSpec((None, None, D, tn), lambda j, i, be, nv: (layer, be[i], 0, j))
    bspec = pl.BlockSpec((None, None, 1, tn), lambda j, i, be, nv: (layer, be[i], 0, j))
    return pl.pallas_call(
        _expert_up_kernel,
        grid_spec=pltpu.PrefetchScalarGridSpec(
            num_scalar_prefetch=2,
            grid=(D // tn, cap // BM),
            in_specs=[pl.BlockSpec((BM, D // 2), rowblk), wspec, bspec, wspec, bspec],
            out_specs=pl.BlockSpec((BM, tn), lambda j, i, be, nv: (i, j)),
            scratch_shapes=[pltpu.VMEM((D, tn), BF16), pltpu.VMEM((D, tn), BF16)]),
        out_shape=jax.ShapeDtypeStruct((cap, D), BF16),
        compiler_params=pltpu.CompilerParams(dimension_semantics=("arbitrary", "arbitrary"),
                                             vmem_limit_bytes=EXPERT_VMEM_LIMIT),
        name="moe_expert_up",
    )(block_e, n_valid, xb, w_gate, b_gate, w_up, b_up)


def _expert_down_kernel(be_ref, nv_ref, a_ref, wd_ref, bd_ref, y_ref, wd_bf):
    used = pl.program_id(1) < nv_ref[0]

    @pl.when(jnp.logical_not(used))
    def _():
        y_ref[...] = jnp.zeros_like(y_ref)

    @pl.when(jnp.logical_and(used, _expert_changed(be_ref)))
    def _():
        wd_bf[...] = wd_ref[...].astype(BF16)

    @pl.when(used)
    def _():
        y_ref[...] = jnp.dot(a_ref[...], wd_bf[...], preferred_element_type=F32) + bd_ref[...]


def expert_down(block_e, n_valid, act, w_down, b_down, layer):
    cap = act.shape[0]
    tn = EXPERT_TN
    return pl.pallas_call(
        _expert_down_kernel,
        grid_spec=pltpu.PrefetchScalarGridSpec(
            num_scalar_prefetch=2,
            grid=(D // tn, cap // BM),
            in_specs=[pl.BlockSpec((BM, D), lambda j, i, be, nv: (jnp.minimum(i, nv[0] - 1), 0)),
                      pl.BlockSpec((None, None, D, tn), lambda j, i, be, nv: (layer, be[i], 0, j)),
                      pl.BlockSpec((None, None, 1, tn), lambda j, i, be, nv: (layer, be[i], 0, j))],
            out_specs=pl.BlockSpec((BM, tn), lambda j, i, be, nv: (i, j)),
            scratch_shapes=[pltpu.VMEM((D, tn), BF16)]),
        out_shape=jax.ShapeDtypeStruct((cap, D), F32),
        compiler_params=pltpu.CompilerParams(dimension_semantics=("arbitrary", "arbitrary"),
                                             vmem_limit_bytes=EXPERT_VMEM_LIMIT),
        name="moe_expert_down",
    )(block_e, n_valid, act, w_down, b_down)


def _combine_kernel(dest_ref, wts_ref, x1_ref, mod_ref, g_ref, b_ref, modn_ref, yb_ref,
                    x2_ref, hn_ref, rows, sem):
    def issue(t, carry):
        for k in range(TOP_K):
            _row_copy(yb_ref, dest_ref[0, t * TOP_K + k], rows.at[k], t, sem).start()
        return carry

    lax.fori_loop(0, TB, issue, 0, unroll=DMA_UNROLL)

    def drain(t, carry):
        for k in range(TOP_K):
            _row_copy(yb_ref, dest_ref[0, t * TOP_K + k], rows.at[k], t, sem).wait()
        return carry

    lax.fori_loop(0, TB, drain, 0, unroll=DMA_UNROLL)
    w = wts_ref[...]
    ffn = rows[0] * w[:, 0:1]
    for k in range(1, TOP_K):
        ffn = ffn + rows[k] * w[:, k:k + 1]
    m = mod_ref[...]
    x2 = _ln(ALPHA * x1_ref[...] + m[5:6] * ffn) * g_ref[...] + b_ref[...]
    x2_ref[...] = x2
    mn = modn_ref[...]
    hn_ref[...] = (_ln(x2) * (1.0 + mn[1:2]) + mn[0:1]).astype(BF16)


def moe_combine(dest_flat, wts, x1, mod, mod_next, modrow, ln_g, ln_b, yb):
    n = x1.shape[0]
    row = lambda i: (i, 0)
    const2 = lambda i: (0, 0)
    modspec = pl.BlockSpec((None, 6, D), lambda i: (modrow(i), 0, 0))
    return pl.pallas_call(
        _combine_kernel,
        grid=(n // TB,),
        in_specs=[pl.BlockSpec((None, 1, TB * TOP_K), lambda i: (i, 0, 0), memory_space=pltpu.SMEM),
                  pl.BlockSpec((TB, LANES), row),
                  pl.BlockSpec((TB, D), row),
                  modspec,
                  pl.BlockSpec((1, D), const2),
                  pl.BlockSpec((1, D), const2),
                  modspec,
                  pl.BlockSpec(memory_space=pl.ANY)],
        out_specs=[pl.BlockSpec((TB, D), row), pl.BlockSpec((TB, D), row)],
        out_shape=[jax.ShapeDtypeStruct((n, D), F32), jax.ShapeDtypeStruct((n, D), BF16)],
        scratch_shapes=[pltpu.VMEM((TOP_K, TB, D), F32), pltpu.SemaphoreType.DMA(())],
        compiler_params=_cp("arbitrary"),
        name="moe_combine",
    )(dest_flat, wts, x1, mod, ln_g, ln_b, mod_next, yb)


def moe_layout(hist):
    h = hist[:, 0, :].astype(jnp.int32)
    counts = jnp.sum(h, 0)
    padded = (counts + BM - 1) // BM * BM
    pad_end = jnp.cumsum(padded)
    pad_start = pad_end - padded
    before = jnp.cumsum(h, 0) - h
    base = (pad_start[None, :] + before).astype(F32)[:, None, :]
    return base, pad_end


def _rot_cols(w):
    q = ROPE // 4
    return jnp.concatenate([-w[..., q:2 * q], w[..., 0:q], -w[..., 3 * q:4 * q], w[..., 2 * q:3 * q]], axis=-1)


def _rope_tables(n):
    rows = n // GRID_W
    row_id = jnp.repeat(jnp.arange(rows, dtype=F32), GRID_W)
    col_id = jnp.tile(jnp.arange(GRID_W, dtype=F32), rows)
    half = ROPE // 2
    inv_freq = ROPE_THETA ** (-jnp.arange(0, half, 2, dtype=F32) / half)
    ang_r = row_id[:, None] * inv_freq
    ang_c = col_id[:, None] * inv_freq
    cos = jnp.concatenate([jnp.cos(ang_r)] * 2 + [jnp.cos(ang_c)] * 2, axis=-1)
    sin = jnp.concatenate([jnp.sin(ang_r)] * 2 + [jnp.sin(ang_c)] * 2, axis=-1)
    return cos, sin


def _pad_rows(w, rows):
    return jnp.pad(w, [(0, 0)] * (w.ndim - 2) + [(0, rows - w.shape[-2]), (0, 0)])


def kernel(x_prompt, x_sample, cache_ckv, cache_krope, state_rwkv_fwd, state_rwkv_bwd, c, c_ctx, w_mod, b_mod, w_in, gmlp_ln_g, gmlp_ln_b, gmlp_ws, gmlp_bs, q_norm_g, w_uq, kv_norm_g, w_ukv, rwkv_mu, rwkv_w0, rwkv_w2, rwkv_a0, rwkv_a2, rwkv_g2, rwkv_kk, rwkv_ka, rwkv_rk, rwkv_lnx_g, rwkv_lnx_b, w_branch, w_o, ln1_g, ln1_b, router_w, router_b, w_gate, b_gate, w_up, b_up, w_down, b_down, ln2_g, ln2_b):
    L = w_mod.shape[0]
    b_ctx, seq, _ = x_prompt.shape
    b_lat, t_lat, _ = x_sample.shape
    past = cache_ckv.shape[2]
    n_ctx = b_ctx * seq
    n_lat = b_lat * t_lat
    n = n_ctx + n_lat
    assert seq % TB == 0 and t_lat % TB == 0 and past % TB == 0 and b_lat + 1 <= 8 and n_ctx % t_lat == 0
    nb_ctx = n_ctx // TB
    bps_lat = t_lat // TB

    def modrow(i):
        return jnp.where(i < nb_ctx, 0, 1 + (i - nb_ctx) // bps_lat)

    cond8 = jnp.zeros((8, D), F32).at[0].set(c_ctx).at[1:1 + b_lat].set(c)
    mod_all = modulation(cond8, w_mod, b_mod).reshape(L, 8, 6, D)

    gm_end = 2 * D
    mla_end = gm_end + Q_LORA + KV_LORA + ROPE
    rw_end = mla_end + 3 * D + 2 * DECAY_LORA + 2 * AAA_LORA + GATE_LORA
    w_gm = w_in[:, :, :gm_end].astype(BF16)
    w_kr = w_in[:, :, mla_end - ROPE:mla_end]
    w_mla = jnp.concatenate([w_in[:, :, gm_end:mla_end], _rot_cols(w_kr)], axis=-1).astype(BF16)
    rw = w_in[:, :, mla_end:rw_end]
    segs = [rw[:, :, :3 * D]]
    mu_segs = [rwkv_mu[:, :3 * D]]
    o = 3 * D
    for width in (DECAY_LORA, DECAY_LORA, AAA_LORA, AAA_LORA):
        segs.append(jnp.pad(rw[:, :, o:o + width], ((0, 0), (0, 0), (0, LORA_PAD - width))))
        mu_segs.append(jnp.pad(rwkv_mu[:, o:o + width], ((0, 0), (0, LORA_PAD - width))))
        o += width
    segs.append(rw[:, :, o:])
    mu_segs.append(rwkv_mu[:, o:])
    w_rw = jnp.concatenate(segs, axis=-1).astype(BF16)
    mu_rw = jnp.concatenate(mu_segs, axis=-1)[:, None, :]
    w_gt = w_in[:, :, rw_end:].astype(BF16)
    w2p = _pad_rows(rwkv_w2, LORA_PAD).astype(BF16)
    a2p = _pad_rows(rwkv_a2, LORA_PAD).astype(BF16)
    g2b = rwkv_g2.astype(BF16)
    wq4 = w_uq.reshape(L, Q_LORA, HEADS, QK_DIM)
    wq_r = wq4[..., NOPE:]
    wq = jnp.concatenate([wq4[..., :NOPE], wq_r, _rot_cols(wq_r)], axis=-1)
    wq = wq.reshape(L, Q_LORA, HEADS * 2 * LANES).astype(BF16)
    wkv4 = w_ukv.reshape(L, KV_LORA, HEADS, NOPE + V_DIM)
    wk = wkv4[..., :NOPE].reshape(L, KV_LORA, D).astype(BF16)
    wv = wkv4[..., NOPE:].reshape(L, KV_LORA, D).astype(BF16)
    ws_b = gmlp_ws.astype(BF16)
    bs_full = jnp.repeat(jnp.swapaxes(gmlp_bs, 1, 2), GMLP_GW, axis=2)
    w_br = w_branch.astype(BF16)
    w_ob = w_o.astype(BF16)
    rw_pad = jnp.pad(router_w, ((0, 0), (0, 0), (0, LANES - N_EXPERTS)))
    rb_pad = jnp.pad(router_b, ((0, 0), (0, LANES - N_EXPERTS)), constant_values=-1e30)[:, None, :]
    bg4 = b_gate[:, :, None, :]
    bu4 = b_up[:, :, None, :]
    bd4 = b_down[:, :, None, :]

    cos, sin = _rope_tables(t_lat)
    cos_all = jnp.concatenate([jnp.ones((n_ctx, ROPE), F32), jnp.tile(cos, (b_lat, 1))], axis=0)
    sin_all = jnp.concatenate([jnp.zeros((n_ctx, ROPE), F32), jnp.tile(sin, (b_lat, 1))], axis=0)
    tabk = jnp.concatenate([cos_all, sin_all], axis=-1)
    tabq = jnp.concatenate([jnp.ones((n, NOPE), F32), tabk], axis=-1)
    tab_cache = jnp.concatenate([jnp.ones((b_lat * past, ROPE), F32), jnp.zeros((b_lat * past, ROPE), F32)], -1)

    zeros_state = jnp.zeros((b_ctx, RHEADS, RH, RH), F32)
    sf_lat = state_rwkv_fwd
    sb_lat = state_rwkv_bwd

    cap = (n * TOP_K // BM + N_EXPERTS) * BM
    x = jnp.concatenate([x_prompt.reshape(n_ctx, D), x_sample.reshape(n_lat, D)], axis=0)
    h = ln_modulate(x, mod_all[0], modrow)
    ckv_out, kr_out, sf_out, sb_out = [], [], [], []
    for l in range(L):
        mod = mod_all[l]
        p_gm = matmul(h, w_gm, l, 1024, "in_proj_gmlp", BF16)
        p_mla = matmul(h, w_mla, l, MLA_COLS, "in_proj_mla")
        p_rw = matmul(h, w_rw, l, 768, "in_proj_rwkv")
        p_gate = matmul(h, w_gt, l, 1024, "in_proj_gate", BF16)

        o_a = gmlp(p_gm, gmlp_ln_g[l][None], gmlp_ln_b[l][None], ws_b[l], bs_full[l])

        q = q_proj(p_mla, q_norm_g[l][None], wq[l], tabq)
        kcat, vv, ckvn = kv_proj(p_mla, 1, p_mla, (Q_LORA + KV_LORA) // LANES, kv_norm_g[l][None],
                                 wk[l], wv[l], tabk, True)
        cache_c = cache_ckv[:, l].reshape(b_lat * past, KV_LORA)
        cache_r = jnp.pad(cache_krope[:, l].reshape(b_lat * past, ROPE), ((0, 0), (0, ROPE)))
        kcat_c, vv_c, _ = kv_proj(cache_c, 0, cache_r, 0, kv_norm_g[l][None], wk[l], wv[l], tab_cache, False)
        o_b_ctx = attention(q, 0, kcat, vv, b_ctx, seq, seq, HEADS, "mla_attention_ctx")
        o_b_lat = attention_lat(q, n_ctx, kcat_c, vv_c, kcat, vv, b_lat, t_lat, past, 2)
        o_b = jnp.concatenate([o_b_ctx, o_b_lat], axis=0)
        ckv_out.append(ckvn[:n_ctx].reshape(b_ctx, seq, KV_LORA))
        kr_out.append(p_mla[:n_ctx, Q_LORA + KV_LORA:Q_LORA + KV_LORA + ROPE].reshape(b_ctx, seq, ROPE))

        (r, v, kk, lwf, lwb, kdf, kdb, bqf, bqb, bonus, gg) = rwkv_prep(
            p_rw, mu_rw[l], rwkv_kk[l][None], rwkv_w0[l], w2p[l], rwkv_a0[l], a2p[l], rwkv_ka[l],
            rwkv_rk[l].reshape(1, D), g2b[l], n_ctx // RB, seq // RB, t_lat // RB)
        yf_c, yb_c, sf_c, sb_c = rwkv_scan(r, v, kk, lwf, lwb, kdf, kdb, bqf, bqb, zeros_state, zeros_state,
                                           0, b_ctx, seq, "rwkv_scan_ctx")
        yf_l, yb_l, _, _ = rwkv_scan(r, v, kk, lwf, lwb, kdf, kdb, bqf, bqb, sf_lat[:, l], sb_lat[:, l],
                                     n_ctx, b_lat, t_lat, "rwkv_scan_lat")
        sf_out.append(sf_c)
        sb_out.append(sb_c)
        o_c = rwkv_finish(yf_c, yb_c, yf_l, yb_l, bonus, gg, rwkv_lnx_g[l][None], rwkv_lnx_b[l][None])

        z = branch_merge(o_a, o_b, o_c, w_br, l, p_gate)
        x1, h2, h2p = out_proj(z, w_ob, l, x, mod, modrow, ln1_g[l][None], ln1_b[l][None])

        idx, wts, hist = router(h2, rw_pad[l], rb_pad[l])
        base, pad_end = moe_layout(hist)
        dest = moe_dest(idx, base)
        dest_flat = dest[:, :TOP_K].reshape(n // TB, 1, TB * TOP_K)
        xb = moe_dispatch(dest_flat, h2p, jnp.zeros((cap, D // 2), jnp.uint32))
        n_valid = (pad_end[N_EXPERTS - 1] // BM).astype(jnp.int32)
        blk_start = jnp.minimum(jnp.arange(cap // BM, dtype=jnp.int32), n_valid - 1) * BM
        block_e = jnp.sum((pad_end[None, :N_EXPERTS] <= blk_start[:, None]).astype(jnp.int32), axis=1)
        block_e = jnp.minimum(block_e, N_EXPERTS - 1)
        nv = n_valid.reshape(1)
        act = expert_up(block_e, nv, xb, w_gate, bg4, w_up, bu4, l)
        yb = expert_down(block_e, nv, act, w_down, bd4, l)
        mod_next = mod_all[min(l + 1, L - 1)]
        x, h = moe_combine(dest_flat, wts, x1, mod, mod_next, modrow, ln2_g[l][None], ln2_b[l][None], yb)

    y_prompt = x[:n_ctx].reshape(b_ctx, seq, D)
    y_sample = x[n_ctx:].reshape(b_lat, t_lat, D)
    return (y_prompt, y_sample, jnp.stack(ckv_out, axis=1), jnp.stack(kr_out, axis=1),
            jnp.stack(sf_out, axis=1), jnp.stack(sb_out, axis=1))
```

```python
import functools
import math

import jax
import jax.numpy as jnp
from jax import lax
from jax.experimental import pallas as pl
from jax.experimental.pallas import tpu as pltpu

F32 = jnp.float32
BF16 = jnp.bfloat16
HI = lax.Precision.HIGHEST

DEPTH_NORM = 4
GRID_W = 64
CHUNK = 128
D = 2048
GMLP_GW = 128
GMLP_GROUPS = D // GMLP_GW
V_DIM = 128
NOPE = 128
ROPE = 64
QK_DIM = NOPE + ROPE
HEADS = D // V_DIM
Q_LORA = 512
KV_LORA = 512
ROPE_THETA = 10000.0
RH = 64
RHEADS = D // RH
DECAY_LORA = 96
AAA_LORA = 96
GATE_LORA = 256
GN_EPS = 64e-5
N_EXPERTS = 32
TOP_K = 4
SWIGLU_LIMIT = 7.0
SWIGLU_ALPHA = 1.702
LN_EPS = 1e-5
RMS_EPS = 1e-6
ALPHA = (2 * DEPTH_NORM) ** 0.25

LANES = 128
VMEM_LIMIT = 48 * 1024 * 1024

TB = 256
RB = 128
SC = 64
HG = 32
BM = 512
EXPERT_TN = 1024
EXPERT_UP_TN = 1024
EXPERT_VMEM_LIMIT = 56 * 1024 * 1024
DMA_UNROLL = 8
LORA_PAD = 128
RW_COLS = 3 * D + 4 * LORA_PAD + GATE_LORA
MLA_COLS = Q_LORA + KV_LORA + 2 * ROPE


def _cp(*sem):
    return pltpu.CompilerParams(dimension_semantics=sem, vmem_limit_bytes=VMEM_LIMIT)


def _ln(x):
    mu = jnp.mean(x, -1, keepdims=True)
    xc = x - mu
    var = jnp.mean(xc * xc, -1, keepdims=True)
    return xc * lax.rsqrt(var + LN_EPS)


def _sigmoid(x):
    return 1.0 / (1.0 + jnp.exp(-x))


def _gelu_tanh(x):
    return 0.5 * x * (1.0 + jnp.tanh(math.sqrt(2.0 / math.pi) * (x + 0.044715 * (x * x * x))))


def _mod_kernel(c_ref, w_ref, b_ref, o_ref):
    c = c_ref[...]
    s = (c * _sigmoid(c)).astype(BF16)
    o_ref[...] = jnp.dot(s, w_ref[...].astype(BF16), preferred_element_type=F32) + b_ref[...]


def modulation(cond8, w_mod, b_mod):
    L = w_mod.shape[0]
    tn = 1024
    return pl.pallas_call(
        _mod_kernel,
        grid=(L, 6 * D // tn),
        in_specs=[pl.BlockSpec((8, D), lambda l, j: (0, 0)),
                  pl.BlockSpec((None, D, tn), lambda l, j: (l, 0, j)),
                  pl.BlockSpec((None, 1, tn), lambda l, j: (l, 0, j))],
        out_specs=pl.BlockSpec((None, 8, tn), lambda l, j: (l, 0, j)),
        out_shape=jax.ShapeDtypeStruct((L, 8, 6 * D), F32),
        compiler_params=_cp("arbitrary", "arbitrary"),
        name="modulation",
    )(cond8, w_mod, b_mod.reshape(L, 1, 6 * D))


def _lnmod_kernel(x_ref, mod_ref, h_ref):
    m = mod_ref[...]
    h_ref[...] = (_ln(x_ref[...]) * (1.0 + m[1:2]) + m[0:1]).astype(BF16)


def ln_modulate(x, mod, modrow):
    n = x.shape[0]
    return pl.pallas_call(
        _lnmod_kernel,
        grid=(n // TB,),
        in_specs=[pl.BlockSpec((TB, D), lambda i: (i, 0)),
                  pl.BlockSpec((None, 6, D), lambda i: (modrow(i), 0, 0))],
        out_specs=pl.BlockSpec((TB, D), lambda i: (i, 0)),
        out_shape=jax.ShapeDtypeStruct((n, D), BF16),
        compiler_params=_cp("arbitrary"),
        name="ln_modulate",
    )(x, mod)


def _mm_kernel(x_ref, w_ref, o_ref):
    o_ref[...] = jnp.dot(x_ref[...], w_ref[...], preferred_element_type=F32).astype(o_ref.dtype)


def matmul(x, w, layer, tn, name, out_dtype=F32):
    m, k = x.shape
    nn = w.shape[2]
    tm = 1024 if m % 1024 == 0 else (512 if m % 512 == 0 else TB)
    return pl.pallas_call(
        _mm_kernel,
        grid=(nn // tn, m // tm),
        in_specs=[pl.BlockSpec((tm, k), lambda j, i: (i, 0)),
                  pl.BlockSpec((None, k, tn), lambda j, i: (layer, 0, j))],
        out_specs=pl.BlockSpec((tm, tn), lambda j, i: (i, j)),
        out_shape=jax.ShapeDtypeStruct((m, nn), out_dtype),
        compiler_params=_cp("arbitrary", "arbitrary"),
        name=name,
    )(x, w)


def _gmlp_kernel(p_ref, g_ref, b_ref, ws_ref, bs_ref, o_ref):
    u = _gelu_tanh(p_ref[:, :D].astype(F32))
    v = _gelu_tanh(p_ref[:, D:].astype(F32))
    v = (_ln(v) * g_ref[...] + b_ref[...]).astype(BF16)
    for g in range(GMLP_GROUPS):
        sl = slice(g * GMLP_GW, (g + 1) * GMLP_GW)
        mixed = jnp.dot(ws_ref[g], v[:, sl], preferred_element_type=F32) + bs_ref[:, sl]
        o_ref[:, sl] = (u[:, sl] * mixed).astype(BF16)


def gmlp(p_gm, ln_g, ln_b, ws, bs_full):
    n = p_gm.shape[0]
    return pl.pallas_call(
        _gmlp_kernel,
        grid=(n // CHUNK,),
        in_specs=[pl.BlockSpec((CHUNK, 2 * D), lambda i: (i, 0)),
                  pl.BlockSpec((1, D), lambda i: (0, 0)),
                  pl.BlockSpec((1, D), lambda i: (0, 0)),
                  pl.BlockSpec((GMLP_GROUPS, CHUNK, CHUNK), lambda i: (0, 0, 0)),
                  pl.BlockSpec((CHUNK, D), lambda i: (0, 0))],
        out_specs=pl.BlockSpec((CHUNK, D), lambda i: (i, 0)),
        out_shape=jax.ShapeDtypeStruct((n, D), BF16),
        compiler_params=_cp("arbitrary"),
        name="gmlp",
    )(p_gm, ln_g, ln_b, ws, bs_full)


def _qproj_kernel(cq_ref, g_ref, w_ref, tab_ref, q_ref):
    cq = cq_ref[...]
    n = (cq * lax.rsqrt(jnp.mean(cq * cq, -1, keepdims=True) + RMS_EPS) * g_ref[...]).astype(BF16)
    tab = tab_ref[...]
    hw = 2 * LANES
    for h in range(HEADS):
        sl = slice(h * hw, (h + 1) * hw)
        q_ref[:, sl] = (jnp.dot(n, w_ref[:, sl], preferred_element_type=F32) * tab).astype(BF16)


def q_proj(p_mla, g, wq, tabq):
    n = p_mla.shape[0]
    hw = 2 * LANES
    return pl.pallas_call(
        _qproj_kernel,
        grid=(n // TB,),
        in_specs=[pl.BlockSpec((TB, Q_LORA), lambda i: (i, 0)),
                  pl.BlockSpec((1, Q_LORA), lambda i: (0, 0)),
                  pl.BlockSpec((Q_LORA, HEADS * hw), lambda i: (0, 0)),
                  pl.BlockSpec((TB, hw), lambda i: (i, 0))],
        out_specs=pl.BlockSpec((TB, HEADS * hw), lambda i: (i, 0)),
        out_shape=jax.ShapeDtypeStruct((n, HEADS * hw), BF16),
        compiler_params=_cp("arbitrary"),
        name="mla_q_proj",
    )(p_mla, g, wq, tabq)


def _kvproj_kernel(ckv_ref, kr_ref, g_ref, wk_ref, wv_ref, tab_ref, kcat_ref, v_ref, ckvn_ref, *, normalize):
    ckv = ckv_ref[...]
    if normalize:
        ckv = ckv * lax.rsqrt(jnp.mean(ckv * ckv, -1, keepdims=True) + RMS_EPS) * g_ref[...]
    ckvn_ref[...] = ckv
    nb = ckv.astype(BF16)
    t = kr_ref[...] * tab_ref[...]
    khi = (t + pltpu.roll(t, ROPE, axis=1)).astype(BF16)
    kn = jnp.dot(nb, wk_ref[...], preferred_element_type=F32).astype(BF16)
    v_ref[...] = jnp.dot(nb, wv_ref[...], preferred_element_type=F32).astype(BF16)
    hw = 2 * LANES
    for h in range(HEADS):
        kcat_ref[:, h * hw:h * hw + LANES] = kn[:, h * LANES:(h + 1) * LANES]
        kcat_ref[:, h * hw + LANES:(h + 1) * hw] = khi


def kv_proj(ckv_src, ckv_col, kr_src, kr_col, g, wk, wv, tabk, normalize):
    n = ckv_src.shape[0]
    hw = 2 * LANES
    return pl.pallas_call(
        functools.partial(_kvproj_kernel, normalize=normalize),
        grid=(n // TB,),
        in_specs=[pl.BlockSpec((TB, KV_LORA), lambda i: (i, ckv_col)),
                  pl.BlockSpec((TB, LANES), lambda i: (i, kr_col)),
                  pl.BlockSpec((1, KV_LORA), lambda i: (0, 0)),
                  pl.BlockSpec((KV_LORA, D), lambda i: (0, 0)),
                  pl.BlockSpec((KV_LORA, D), lambda i: (0, 0)),
                  pl.BlockSpec((TB, LANES), lambda i: (i, 0))],
        out_specs=[pl.BlockSpec((TB, HEADS * hw), lambda i: (i, 0)),
                   pl.BlockSpec((TB, D), lambda i: (i, 0)),
                   pl.BlockSpec((TB, KV_LORA), lambda i: (i, 0))],
        out_shape=[jax.ShapeDtypeStruct((n, HEADS * hw), BF16),
                   jax.ShapeDtypeStruct((n, D), BF16),
                   jax.ShapeDtypeStruct((n, KV_LORA), F32)],
        compiler_params=_cp("arbitrary"),
        name="mla_kv_proj",
    )(ckv_src, kr_src, g, wk, wv, tabk)


def _attn_kernel(q_ref, k_ref, v_ref, o_ref, *, nh):
    hw = 2 * LANES
    for h in range(nh):
        s = lax.dot_general(q_ref[:, h * hw:(h + 1) * hw], k_ref[:, h * hw:(h + 1) * hw],
                            (((1,), (1,)), ((), ())), preferred_element_type=F32)
        s = s * (QK_DIM ** -0.5)
        e = jnp.exp(s - jnp.max(s, -1, keepdims=True))
        p = (e * (1.0 / jnp.sum(e, -1, keepdims=True))).astype(BF16)
        o_ref[:, h * V_DIM:(h + 1) * V_DIM] = jnp.dot(
            p, v_ref[:, h * V_DIM:(h + 1) * V_DIM], preferred_element_type=F32).astype(BF16)


def _attn_lat_kernel(q_ref, kc_ref, vc_ref, kn_ref, vn_ref, o_ref, *, nh, past):
    hw = 2 * LANES
    nt = (((1,), (1,)), ((), ()))
    for h in range(nh):
        q = q_ref[:, h * hw:(h + 1) * hw]
        s = jnp.concatenate(
            [lax.dot_general(q, kc_ref[:, h * hw:(h + 1) * hw], nt, preferred_element_type=F32),
             lax.dot_general(q, kn_ref[:, h * hw:(h + 1) * hw], nt, preferred_element_type=F32)], axis=1)
        s = s * (QK_DIM ** -0.5)
        e = jnp.exp(s - jnp.max(s, -1, keepdims=True))
        p = (e * (1.0 / jnp.sum(e, -1, keepdims=True))).astype(BF16)
        vs = slice(h * V_DIM, (h + 1) * V_DIM)
        o_ref[:, vs] = (jnp.dot(p[:, :past], vc_ref[:, vs], preferred_element_type=F32)
                        + jnp.dot(p[:, past:], vn_ref[:, vs], preferred_element_type=F32)).astype(BF16)


def attention_lat(q, row0, kcat_c, v_c, kcat, v, n_seq, t_q, past, nh):
    hw = 2 * LANES
    qb = t_q // TB
    qoff = row0 // TB
    koff = row0 // t_q
    return pl.pallas_call(
        functools.partial(_attn_lat_kernel, nh=nh, past=past),
        grid=(n_seq, HEADS // nh, qb),
        in_specs=[pl.BlockSpec((TB, nh * hw), lambda s, h, i: (qoff + s * qb + i, h)),
                  pl.BlockSpec((past, nh * hw), lambda s, h, i: (s, h)),
                  pl.BlockSpec((past, nh * V_DIM), lambda s, h, i: (s, h)),
                  pl.BlockSpec((t_q, nh * hw), lambda s, h, i: (koff + s, h)),
                  pl.BlockSpec((t_q, nh * V_DIM), lambda s, h, i: (koff + s, h))],
        out_specs=pl.BlockSpec((TB, nh * V_DIM), lambda s, h, i: (s * qb + i, h)),
        out_shape=jax.ShapeDtypeStruct((n_seq * t_q, D), BF16),
        compiler_params=_cp("arbitrary", "arbitrary", "arbitrary"),
        name="mla_attention_lat",
    )(q, kcat_c, v_c, kcat, v)


def attention(q, q_row0, kcat, v, n_seq, t_q, n_keys, nh, name):
    hw = 2 * LANES
    qb = t_q // TB
    qoff = q_row0 // TB
    return pl.pallas_call(
        functools.partial(_attn_kernel, nh=nh),
        grid=(n_seq, HEADS // nh, qb),
        in_specs=[pl.BlockSpec((TB, nh * hw), lambda s, h, i: (qoff + s * qb + i, h)),
                  pl.BlockSpec((n_keys, nh * hw), lambda s, h, i: (s, h)),
                  pl.BlockSpec((n_keys, nh * V_DIM), lambda s, h, i: (s, h))],
        out_specs=pl.BlockSpec((TB, nh * V_DIM), lambda s, h, i: (s * qb + i, h)),
        out_shape=jax.ShapeDtypeStruct((n_seq * t_q, D), BF16),
        compiler_params=_cp("arbitrary", "arbitrary", "arbitrary"),
        name=name,
    )(q, kcat, v)


def _head_ones():
    r = lax.broadcasted_iota(jnp.int32, (LANES, LANES), 0) >= RH
    c = lax.broadcasted_iota(jnp.int32, (LANES, LANES), 1) >= RH
    return (r == c).astype(BF16)


def _head_sum(x, ones_bd):
    hi, lo = _split_bf16(x)
    parts = [jnp.dot(hi[:, i * LANES:(i + 1) * LANES], ones_bd, preferred_element_type=F32)
             + jnp.dot(lo[:, i * LANES:(i + 1) * LANES], ones_bd, preferred_element_type=F32)
             for i in range(x.shape[1] // LANES)]
    return jnp.concatenate(parts, axis=1)


def _rwkv_prep_kernel(p_ref, prev_ref, next_ref, mu_ref, kkw_ref, w0_ref, w2_ref, a0_ref, a2_ref, ka_ref,
                      rk_ref, g2_ref,
                      r_o, v_o, kk_o, lwf_o, lwb_o, kdf_o, kdb_o, bqf_o, bqb_o, bonus_o, g_o,
                      *, nb_ctx, bps_ctx, bps_lat):
    i = pl.program_id(0)
    is_lat = i >= nb_ctx
    pos = jnp.where(is_lat, lax.rem(i - nb_ctx, bps_lat), lax.rem(i, bps_ctx))
    last = jnp.where(is_lat, bps_lat - 1, bps_ctx - 1)
    has_prev = (pos != 0).astype(F32)
    has_next = (pos != last).astype(F32)
    rows = lax.broadcasted_iota(jnp.int32, (RB, 1), 0)

    def shifted(lo, hi):
        p = p_ref[:, lo:hi]
        prev = jnp.where(rows == 0, prev_ref[7:8, lo:hi] * has_prev, pltpu.roll(p, 1, axis=0))
        nxt = jnp.where(rows == RB - 1, next_ref[0:1, lo:hi] * has_next, pltpu.roll(p, RB - 1, axis=0))
        return p + (0.5 * (prev + nxt) - p) * mu_ref[:, lo:hi]

    ones_bd = _head_ones()
    r = shifted(0, D)
    k = shifted(D, 2 * D)
    v = shifted(2 * D, 3 * D)
    r_o[...] = r
    v_o[...] = v
    kkf = k * kkw_ref[...]
    nrm = jnp.sqrt(_head_sum(kkf * kkf, ones_bd))
    kk = kkf / jnp.maximum(nrm, 1e-12)
    kk_o[...] = kk
    bonus_o[...] = _head_sum(r * k * rk_ref[...], ones_bd) * v
    base = 3 * D
    outs = ((lwf_o, kdf_o, bqf_o), (lwb_o, kdb_o, bqb_o))
    for d in range(2):
        wl = shifted(base + d * LORA_PAD, base + (d + 1) * LORA_PAD)
        z = w0_ref[d:d + 1, :] + jnp.dot(jnp.tanh(wl).astype(BF16), w2_ref[d], preferred_element_type=F32)
        y = -z
        softplus = jnp.maximum(y, 0.0) + jnp.log(1.0 + jnp.exp(-jnp.abs(y)))
        logw = -softplus - 0.5
        outs[d][0][...] = -jnp.exp(logw)
        al = shifted(base + (2 + d) * LORA_PAD, base + (3 + d) * LORA_PAD)
        a = _sigmoid(a0_ref[d:d + 1, :] + jnp.dot(al.astype(BF16), a2_ref[d], preferred_element_type=F32))
        outs[d][1][...] = k * (1.0 + (a - 1.0) * ka_ref[d:d + 1, :])
        outs[d][2][...] = kk * a
    gl = shifted(base + 4 * LORA_PAD, base + 4 * LORA_PAD + GATE_LORA)
    g_o[...] = jnp.dot(_sigmoid(gl).astype(BF16), g2_ref[...], preferred_element_type=F32)


def rwkv_prep(p_rw, mu, kkw, w0, w2, a0, a2, ka, rk, g2, nb_ctx, bps_ctx, bps_lat):
    n = p_rw.shape[0]
    nb = n // RB
    sub = RB // 8
    row = lambda i: (i, 0)
    const2 = lambda i: (0, 0)
    out = jax.ShapeDtypeStruct((n, D), F32)
    return pl.pallas_call(
        functools.partial(_rwkv_prep_kernel, nb_ctx=nb_ctx, bps_ctx=bps_ctx, bps_lat=bps_lat),
        grid=(nb,),
        in_specs=[pl.BlockSpec((RB, RW_COLS), row),
                  pl.BlockSpec((8, RW_COLS), lambda i: (jnp.maximum(i * sub - 1, 0), 0)),
                  pl.BlockSpec((8, RW_COLS), lambda i: (jnp.minimum((i + 1) * sub, nb * sub - 1), 0)),
                  pl.BlockSpec((1, RW_COLS), const2),
                  pl.BlockSpec((1, D), const2),
                  pl.BlockSpec((2, D), const2),
                  pl.BlockSpec((2, LORA_PAD, D), lambda i: (0, 0, 0)),
                  pl.BlockSpec((2, D), const2),
                  pl.BlockSpec((2, LORA_PAD, D), lambda i: (0, 0, 0)),
                  pl.BlockSpec((2, D), const2),
                  pl.BlockSpec((1, D), const2),
                  pl.BlockSpec((GATE_LORA, D), const2)],
        out_specs=[pl.BlockSpec((RB, D), row)] * 11,
        out_shape=[out] * 11,
        compiler_params=_cp("arbitrary"),
        name="rwkv_prep",
    )(p_rw, p_rw, p_rw, mu, kkw, w0, w2, a0, a2, ka, rk, g2)


def _mm(eq, a, b):
    return jnp.einsum(eq, a.astype(BF16), b.astype(BF16), preferred_element_type=F32)


def _split_bf16(x):
    hi = x.astype(BF16)
    return hi, (x - hi.astype(F32)).astype(BF16)


def _mm3(eq, a, b):
    ah, al = _split_bf16(a)
    bh, bl = _split_bf16(b)
    mm = lambda x, y: jnp.einsum(eq, x, y, preferred_element_type=F32)
    return mm(ah, bh) + (mm(ah, bl) + mm(al, bh))


_NN = "hmk,hkn->hmn"
_NT = "hmk,hnk->hmn"
_TN = "hkm,hkn->hmn"


def _scan_sum(x, reverse):
    rows = lax.broadcasted_iota(jnp.int32, (SC, 1), 0)
    shift = 1
    while shift < SC:
        if reverse:
            x = x + jnp.where(rows < SC - shift, pltpu.roll(x, SC - shift, axis=0), 0.0)
        else:
            x = x + jnp.where(rows >= shift, pltpu.roll(x, shift, axis=0), 0.0)
        shift *= 2
    return x


def _wkv_chunk(refs, s, reverse):
    r, v, kk, lw, kd, bq = (x[...] for x in refs)
    last = 0 if reverse else SC - 1
    L = _scan_sum(lw, reverse)
    ltot = L[last:last + 1, :]
    en = jnp.exp(-L)
    et = jnp.exp(ltot - L)

    def heads(x):
        return jnp.stack([x[:, h * RH:(h + 1) * RH] for h in range(HG)], axis=0)

    ax = heads(-kk * jnp.exp(L - lw))
    rt = heads(r * jnp.exp(L))
    bt = heads(bq * en)
    kt = heads(kd * en)
    bc = heads(bq * et)
    kc = heads(kd * et)
    ptot = heads(jnp.exp(ltot))
    v = heads(v)

    ti = lax.broadcasted_iota(jnp.int32, (SC, SC), 0)
    si = lax.broadcasted_iota(jnp.int32, (SC, SC), 1)
    strict = (si > ti) if reverse else (si < ti)
    incl = (si >= ti) if reverse else (si <= ti)
    a_ab = jnp.where(strict, _mm3(_NT, ax, bt), 0.0)
    a_ak = jnp.where(strict, _mm3(_NT, ax, kt), 0.0)
    a_rb = jnp.where(incl, _mm(_NT, rt, bt), 0.0)
    a_rk = jnp.where(incl, _mm(_NT, rt, kt), 0.0)
    npow = a_ab
    e = a_ab
    for _ in range(int(math.log2(SC)) - 2):
        npow = _mm(_NN, npow, npow)
        e = e + npow + _mm(_NN, e, npow)
    ws = _mm(_NT, jnp.concatenate([ax, rt], axis=1), s)
    w = ws[:, :SC] + _mm(_NN, a_ak, v)
    u = w + _mm(_NN, e, w)
    rho = (w - u) + _mm3(_NN, a_ab, u)
    u = u + rho + _mm(_NN, e, rho)
    y = ws[:, SC:] + _mm(_NN, a_rb, u) + _mm(_NN, a_rk, v)
    s_new = s * ptot + _mm3(_TN, jnp.concatenate([u, v], axis=1), jnp.concatenate([bc, kc], axis=1))
    return jnp.concatenate([y[h] for h in range(HG)], axis=1), s_new


def _scan_kernel(rf, vf, kkf, lwf, kdf, bqf, rb, vb, kkb, lwb, kdb, bqb, s0f, s0b,
                 yf_ref, yb_ref, sfo, sbo, stf, stb, *, nc):
    c = pl.program_id(2)

    @pl.when(c == 0)
    def _():
        stf[...] = s0f[...]
        stb[...] = s0b[...]

    for reverse, refs, st, yref in ((False, (rf, vf, kkf, lwf, kdf, bqf), stf, yf_ref),
                                    (True, (rb, vb, kkb, lwb, kdb, bqb), stb, yb_ref)):
        y, s_new = _wkv_chunk(refs, st[...], reverse)
        st[...] = s_new
        yref[...] = y

    @pl.when(c == nc - 1)
    def _():
        sfo[...] = stf[...]
        sbo[...] = stb[...]


def rwkv_scan(r, v, kk, lwf, lwb, kdf, kdb, bqf, bqb, s0f, s0b, row0, n_seq, t_len, name):
    nc = t_len // SC
    off = row0 // SC
    cw = HG * RH
    fwd = lambda s, h, c: (off + s * nc + c, h)
    bwd = lambda s, h, c: (off + s * nc + (nc - 1 - c), h)
    blk = lambda im: pl.BlockSpec((SC, cw), im)
    st_spec = pl.BlockSpec((None, HG, RH, RH), lambda s, h, c: (s, h, 0, 0))
    n_rows = n_seq * t_len
    return pl.pallas_call(
        functools.partial(_scan_kernel, nc=nc),
        grid=(n_seq, RHEADS // HG, nc),
        in_specs=[blk(fwd)] * 6 + [blk(bwd)] * 6 + [st_spec, st_spec],
        out_specs=[pl.BlockSpec((SC, cw), lambda s, h, c: (s * nc + c, h)),
                   pl.BlockSpec((SC, cw), lambda s, h, c: (s * nc + (nc - 1 - c), h)),
                   st_spec, st_spec],
        out_shape=[jax.ShapeDtypeStruct((n_rows, D), F32), jax.ShapeDtypeStruct((n_rows, D), F32),
                   jax.ShapeDtypeStruct(s0f.shape, F32), jax.ShapeDtypeStruct(s0b.shape, F32)],
        scratch_shapes=[pltpu.VMEM((HG, RH, RH), F32), pltpu.VMEM((HG, RH, RH), F32)],
        compiler_params=_cp("arbitrary", "arbitrary", "arbitrary"),
        name=name,
    )(r, v, kk, lwf, kdf, bqf, r, v, kk, lwb, kdb, bqb, s0f, s0b)


def _rwkv_fin_kernel(yfc_ref, ybc_ref, yfl_ref, ybl_ref, bonus_ref, g_ref, lg_ref, lb_ref, o_ref, *, nb_ctx):
    ones_bd = _head_ones()
    is_lat = pl.program_id(0) >= nb_ctx
    y = jnp.where(is_lat, yfl_ref[...] + ybl_ref[...], yfc_ref[...] + ybc_ref[...])
    mu = _head_sum(y, ones_bd) * (1.0 / RH)
    yc = y - mu
    var = _head_sum(yc * yc, ones_bd) * (1.0 / RH)
    yn = yc * lax.rsqrt(var + GN_EPS) * lg_ref[...] + lb_ref[...]
    o_ref[...] = ((yn + bonus_ref[...]) * g_ref[...]).astype(BF16)


def rwkv_finish(yf_c, yb_c, yf_l, yb_l, bonus, g, lnx_g, lnx_b):
    n = bonus.shape[0]
    nb_ctx = yf_c.shape[0] // TB
    row = lambda i: (i, 0)
    const2 = lambda i: (0, 0)
    ctx = pl.BlockSpec((TB, D), lambda i: (jnp.minimum(i, nb_ctx - 1), 0))
    lat = pl.BlockSpec((TB, D), lambda i: (jnp.maximum(i - nb_ctx, 0), 0))
    return pl.pallas_call(
        functools.partial(_rwkv_fin_kernel, nb_ctx=nb_ctx),
        grid=(n // TB,),
        in_specs=[ctx, ctx, lat, lat] + [pl.BlockSpec((TB, D), row)] * 2 + [pl.BlockSpec((1, D), const2)] * 2,
        out_specs=pl.BlockSpec((TB, D), row),
        out_shape=jax.ShapeDtypeStruct((n, D), BF16),
        compiler_params=_cp("arbitrary"),
        name="rwkv_finish",
    )(yf_c, yb_c, yf_l, yb_l, bonus, g, lnx_g, lnx_b)


def _merge_kernel(oa_ref, ob_ref, oc_ref, w_ref, ga_ref, gb_ref, gc_ref, z_ref):
    acc = None
    for n, (o_ref, g_ref) in enumerate(((oa_ref, ga_ref), (ob_ref, gb_ref), (oc_ref, gc_ref))):
        y = _sigmoid(g_ref[...].astype(F32)) * jnp.dot(o_ref[...], w_ref[n], preferred_element_type=F32)
        acc = y if acc is None else acc + y
    z_ref[...] = acc.astype(BF16)


def branch_merge(o_a, o_b, o_c, w_branch, layer, p_gate):
    n = o_a.shape[0]
    tn = 512
    tm = 512 if n % 512 == 0 else TB
    nj = D // tn
    o_spec = pl.BlockSpec((tm, D), lambda j, i: (i, 0))
    gate = lambda b: pl.BlockSpec((tm, tn), lambda j, i: (i, b * nj + j))
    return pl.pallas_call(
        _merge_kernel,
        grid=(nj, n // tm),
        in_specs=[o_spec, o_spec, o_spec,
                  pl.BlockSpec((None, 3, D, tn), lambda j, i: (layer, 0, 0, j)),
                  gate(0), gate(1), gate(2)],
        out_specs=pl.BlockSpec((tm, tn), lambda j, i: (i, j)),
        out_shape=jax.ShapeDtypeStruct((n, D), BF16),
        compiler_params=_cp("arbitrary", "arbitrary"),
        name="branch_merge",
    )(o_a, o_b, o_c, w_branch, p_gate, p_gate, p_gate)


def _bf16_bits(x):
    return lax.bitcast_convert_type(x.astype(BF16).astype(F32), jnp.uint32)


def _pack_rows(x):
    return (_bf16_bits(x[:, :D // 2]) >> 16) | _bf16_bits(x[:, D // 2:])


def _unpack_rows(p):
    lo = lax.bitcast_convert_type(p << 16, F32)
    hi = lax.bitcast_convert_type(p & jnp.uint32(0xFFFF0000), F32)
    return jnp.concatenate([lo, hi], axis=1).astype(BF16)


def _oproj_kernel(z_ref, wo_ref, x_ref, mod_ref, g_ref, b_ref, x1_ref, h2_ref, h2p_ref):
    mix = jnp.dot(z_ref[...], wo_ref[...], preferred_element_type=F32)
    m = mod_ref[...]
    x1 = _ln(ALPHA * x_ref[...] + m[2:3] * mix) * g_ref[...] + b_ref[...]
    x1_ref[...] = x1
    h2 = _ln(x1) * (1.0 + m[4:5]) + m[3:4]
    h2_ref[...] = h2
    h2p_ref[...] = _pack_rows(h2)


def out_proj(z, w_o, layer, x, mod, modrow, ln_g, ln_b):
    n = z.shape[0]
    row = lambda i: (i, 0)
    const2 = lambda i: (0, 0)
    return pl.pallas_call(
        _oproj_kernel,
        grid=(n // TB,),
        in_specs=[pl.BlockSpec((TB, D), row),
                  pl.BlockSpec((None, D, D), lambda i: (layer, 0, 0)),
                  pl.BlockSpec((TB, D), row),
                  pl.BlockSpec((None, 6, D), lambda i: (modrow(i), 0, 0)),
                  pl.BlockSpec((1, D), const2),
                  pl.BlockSpec((1, D), const2)],
        out_specs=[pl.BlockSpec((TB, D), row), pl.BlockSpec((TB, D), row), pl.BlockSpec((TB, D // 2), row)],
        out_shape=[jax.ShapeDtypeStruct((n, D), F32), jax.ShapeDtypeStruct((n, D), F32),
                   jax.ShapeDtypeStruct((n, D // 2), jnp.uint32)],
        compiler_params=_cp("arbitrary"),
        name="out_proj_ln",
    )(z, w_o, x, mod, ln_g, ln_b)


def _router_kernel(h_ref, w_ref, b_ref, idx_ref, wts_ref, hist_ref):
    logits = jnp.dot(h_ref[...], w_ref[...], precision=HI, preferred_element_type=F32) + b_ref[...]
    lane = lax.broadcasted_iota(jnp.int32, (TB, LANES), 1).astype(F32)
    cur = logits
    idx_out = jnp.zeros((TB, LANES), F32)
    val_out = jnp.zeros((TB, LANES), F32)
    hist = jnp.zeros((TB, LANES), F32)
    top = None
    for k in range(TOP_K):
        m = jnp.max(cur, -1, keepdims=True)
        am = jnp.min(jnp.where(cur == m, lane, float(LANES)), -1, keepdims=True)
        sel = lane == am
        top = m if top is None else top
        idx_out = jnp.where(lane == k, am, idx_out)
        val_out = jnp.where(lane == k, jnp.exp(m - top), val_out)
        hist = hist + sel.astype(F32)
        cur = jnp.where(sel, -jnp.inf, cur)
    idx_ref[...] = idx_out.astype(jnp.int32)
    wts_ref[...] = val_out * (1.0 / jnp.sum(val_out, -1, keepdims=True))
    hist_ref[...] = jnp.sum(hist, 0, keepdims=True)


def router(h2, rw_pad, rb_pad):
    n = h2.shape[0]
    nb = n // TB
    return pl.pallas_call(
        _router_kernel,
        grid=(nb,),
        in_specs=[pl.BlockSpec((TB, D), lambda i: (i, 0)),
                  pl.BlockSpec((D, LANES), lambda i: (0, 0)),
                  pl.BlockSpec((1, LANES), lambda i: (0, 0))],
        out_specs=[pl.BlockSpec((TB, LANES), lambda i: (i, 0)),
                   pl.BlockSpec((TB, LANES), lambda i: (i, 0)),
                   pl.BlockSpec((None, 1, LANES), lambda i: (i, 0, 0))],
        out_shape=[jax.ShapeDtypeStruct((n, LANES), jnp.int32),
                   jax.ShapeDtypeStruct((n, LANES), F32),
                   jax.ShapeDtypeStruct((nb, 1, LANES), F32)],
        compiler_params=_cp("arbitrary"),
        name="moe_router",
    )(h2, rw_pad, rb_pad)


def _dest_kernel(idx_ref, base_ref, dest_ref):
    lane = lax.broadcasted_iota(jnp.int32, (TB, LANES), 1).astype(F32)
    ri = lax.broadcasted_iota(jnp.int32, (TB, TB), 0)
    ci = lax.broadcasted_iota(jnp.int32, (TB, TB), 1)
    lower = (ci < ri).astype(BF16)
    idx = idx_ref[...].astype(F32)
    run = base_ref[...]
    dest = jnp.zeros((TB, LANES), F32)
    for k in range(TOP_K):
        e_k = jnp.sum(jnp.where(lane == k, idx, 0), -1, keepdims=True)
        onehot = lane == e_k
        oh = onehot.astype(F32)
        before = jnp.dot(lower, oh.astype(BF16), preferred_element_type=F32)
        d_k = jnp.sum(jnp.where(onehot, run + before, 0.0), -1, keepdims=True)
        dest = jnp.where(lane == k, d_k, dest)
        run = run + jnp.sum(oh, 0, keepdims=True)
    dest_ref[...] = dest.astype(jnp.int32)


def moe_dest(idx, base):
    n = idx.shape[0]
    nb = n // TB
    return pl.pallas_call(
        _dest_kernel,
        grid=(nb,),
        in_specs=[pl.BlockSpec((TB, LANES), lambda i: (i, 0)),
                  pl.BlockSpec((None, 1, LANES), lambda i: (i, 0, 0))],
        out_specs=pl.BlockSpec((TB, LANES), lambda i: (i, 0)),
        out_shape=jax.ShapeDtypeStruct((n, LANES), jnp.int32),
        compiler_params=_cp("arbitrary"),
        name="moe_dest",
    )(idx, base)


def _row_copy(src, src_row, dst, dst_row, sem):
    return pltpu.make_async_copy(src.at[pl.ds(src_row, 1)], dst.at[pl.ds(dst_row, 1)], sem)


def _dispatch_kernel(dest_ref, h_ref, xb_in, xb_out, sem):
    del xb_in

    def issue(t, carry):
        for k in range(TOP_K):
            _row_copy(h_ref, t, xb_out, dest_ref[0, t * TOP_K + k], sem).start()
        return carry

    lax.fori_loop(0, TB, issue, 0, unroll=DMA_UNROLL)

    def drain(t, carry):
        for k in range(TOP_K):
            _row_copy(h_ref, t, xb_out, dest_ref[0, t * TOP_K + k], sem).wait()
        return carry

    lax.fori_loop(0, TB, drain, 0, unroll=DMA_UNROLL)


def moe_dispatch(dest_flat, h2, xb_init):
    n = h2.shape[0]
    nb = n // TB
    return pl.pallas_call(
        _dispatch_kernel,
        grid=(nb,),
        in_specs=[pl.BlockSpec((None, 1, TB * TOP_K), lambda i: (i, 0, 0), memory_space=pltpu.SMEM),
                  pl.BlockSpec((TB, D // 2), lambda i: (i, 0)),
                  pl.BlockSpec(memory_space=pl.ANY)],
        out_specs=pl.BlockSpec(memory_space=pl.ANY),
        out_shape=jax.ShapeDtypeStruct(xb_init.shape, xb_init.dtype),
        scratch_shapes=[pltpu.SemaphoreType.DMA(())],
        input_output_aliases={2: 0},
        compiler_params=_cp("arbitrary"),
        name="moe_dispatch",
    )(dest_flat, h2, xb_init)


def _expert_changed(be_ref):
    i = pl.program_id(1)
    return jnp.logical_or(i == 0, be_ref[i] != be_ref[jnp.maximum(i - 1, 0)])


def _expert_up_kernel(be_ref, nv_ref, x_ref, wg_ref, bg_ref, wu_ref, bu_ref, act_ref, wg_bf, wu_bf):
    used = pl.program_id(1) < nv_ref[0]

    @pl.when(jnp.logical_not(used))
    def _():
        act_ref[...] = jnp.zeros_like(act_ref)

    @pl.when(jnp.logical_and(used, _expert_changed(be_ref)))
    def _():
        wg_bf[...] = wg_ref[...].astype(BF16)
        wu_bf[...] = wu_ref[...].astype(BF16)

    @pl.when(used)
    def _():
        half = BM // 2
        for r in range(2):
            rows = slice(r * half, (r + 1) * half)
            xb = _unpack_rows(x_ref[rows, :])
            gl = jnp.minimum(jnp.dot(xb, wg_bf[...], preferred_element_type=F32) + bg_ref[...], SWIGLU_LIMIT)
            lin = jnp.clip(jnp.dot(xb, wu_bf[...], preferred_element_type=F32) + bu_ref[...],
                           -SWIGLU_LIMIT, SWIGLU_LIMIT)
            act_ref[rows, :] = (gl * _sigmoid(SWIGLU_ALPHA * gl) * (lin + 1.0)).astype(BF16)


def expert_up(block_e, n_valid, xb, w_gate, b_gate, w_up, b_up, layer):
    cap = xb.shape[0]
    tn = EXPERT_UP_TN
    rowblk = lambda j, i, be, nv: (jnp.minimum(i, nv[0] - 1), 0)
    wspec = pl.BlockSpec((None, None, D, tn), lambda j, i, be, nv: (layer, be[i], 0, j))
    bspec = pl.BlockSpec((None, None, 1, tn), lambda j, i, be, nv: (layer, be[i], 0, j))
    return pl.pallas_call(
        _expert_up_kernel,
        grid_spec=pltpu.PrefetchScalarGridSpec(
            num_scalar_prefetch=2,
            grid=(D // tn, cap // BM),
            in_specs=[pl.BlockSpec((BM, D // 2), rowblk), wspec, bspec, wspec, bspec],
            out_specs=pl.BlockSpec((BM, tn), lambda j, i, be, nv: (i, j)),
            scratch_shapes=[pltpu.VMEM((D, tn), BF16), pltpu.VMEM((D, tn), BF16)]),
        out_shape=jax.ShapeDtypeStruct((cap, D), BF16),
        compiler_params=pltpu.CompilerParams(dimension_semantics=("arbitrary", "arbitrary"),
                                             vmem_limit_bytes=EXPERT_VMEM_LIMIT),
        name="moe_expert_up",
    )(block_e, n_valid, xb, w_gate, b_gate, w_up, b_up)


def _expert_down_kernel(be_ref, nv_ref, a_ref, wd_ref, bd_ref, y_ref, wd_bf):
    used = pl.program_id(1) < nv_ref[0]

    @pl.when(jnp.logical_not(used))
    def _():
        y_ref[...] = jnp.zeros_like(y_ref)

    @pl.when(jnp.logical_and(used, _expert_changed(be_ref)))
    def _():
        wd_bf[...] = wd_ref[...].astype(BF16)

    @pl.when(used)
    def _():
        y_ref[...] = jnp.dot(a_ref[...], wd_bf[...], preferred_element_type=F32) + bd_ref[...]


def expert_down(block_e, n_valid, act, w_down, b_down, layer):
    cap = act.shape[0]
    tn = EXPERT_TN
    return pl.pallas_call(
        _expert_down_kernel,
        grid_spec=pltpu.PrefetchScalarGridSpec(
            num_scalar_prefetch=2,
            grid=(D // tn, cap // BM),
            in_specs=[pl.BlockSpec((BM, D), lambda j, i, be, nv: (jnp.minimum(i, nv[0] - 1), 0)),
                      pl.BlockSpec((None, None, D, tn), lambda j, i, be, nv: (layer, be[i], 0, j)),
                      pl.BlockSpec((None, None, 1, tn), lambda j, i, be, nv: (layer, be[i], 0, j))],
            out_specs=pl.BlockSpec((BM, tn), lambda j, i, be, nv: (i, j)),
            scratch_shapes=[pltpu.VMEM((D, tn), BF16)]),
        out_shape=jax.ShapeDtypeStruct((cap, D), F32),
        compiler_params=pltpu.CompilerParams(dimension_semantics=("arbitrary", "arbitrary"),
                                             vmem_limit_bytes=EXPERT_VMEM_LIMIT),
        name="moe_expert_down",
    )(block_e, n_valid, act, w_down, b_down)


def _combine_kernel(dest_ref, wts_ref, x1_ref, mod_ref, g_ref, b_ref, modn_ref, yb_ref,
                    x2_ref, hn_ref, rows, sem):
    def issue(t, carry):
        for k in range(TOP_K):
            _row_copy(yb_ref, dest_ref[0, t * TOP_K + k], rows.at[k], t, sem).start()
        return carry

    lax.fori_loop(0, TB, issue, 0, unroll=DMA_UNROLL)

    def drain(t, carry):
        for k in range(TOP_K):
            _row_copy(yb_ref, dest_ref[0, t * TOP_K + k], rows.at[k], t, sem).wait()
        return carry

    lax.fori_loop(0, TB, drain, 0, unroll=DMA_UNROLL)
    w = wts_ref[...]
    ffn = rows[0] * w[:, 0:1]
    for k in range(1, TOP_K):
        ffn = ffn + rows[k] * w[:, k:k + 1]
    m = mod_ref[...]
    x2 = _ln(ALPHA * x1_ref[...] + m[5:6] * ffn) * g_ref[...] + b_ref[...]
    x2_ref[...] = x2
    mn = modn_ref[...]
    hn_ref[...] = (_ln(x2) * (1.0 + mn[1:2]) + mn[0:1]).astype(BF16)


def moe_combine(dest_flat, wts, x1, mod, mod_next, modrow, ln_g, ln_b, yb):
    n = x1.shape[0]
    row = lambda i: (i, 0)
    const2 = lambda i: (0, 0)
    modspec = pl.BlockSpec((None, 6, D), lambda i: (modrow(i), 0, 0))
    return pl.pallas_call(
        _combine_kernel,
        grid=(n // TB,),
        in_specs=[pl.BlockSpec((None, 1, TB * TOP_K), lambda i: (i, 0, 0), memory_space=pltpu.SMEM),
                  pl.BlockSpec((TB, LANES), row),
                  pl.BlockSpec((TB, D), row),
                  modspec,
                  pl.BlockSpec((1, D), const2),
                  pl.BlockSpec((1, D), const2),
                  modspec,
                  pl.BlockSpec(memory_space=pl.ANY)],
        out_specs=[pl.BlockSpec((TB, D), row), pl.BlockSpec((TB, D), row)],
        out_shape=[jax.ShapeDtypeStruct((n, D), F32), jax.ShapeDtypeStruct((n, D), BF16)],
        scratch_shapes=[pltpu.VMEM((TOP_K, TB, D), F32), pltpu.SemaphoreType.DMA(())],
        compiler_params=_cp("arbitrary"),
        name="moe_combine",
    )(dest_flat, wts, x1, mod, ln_g, ln_b, mod_next, yb)


def moe_layout(hist):
    h = hist[:, 0, :].astype(jnp.int32)
    counts = jnp.sum(h, 0)
    padded = (counts + BM - 1) // BM * BM
    pad_end = jnp.cumsum(padded)
    pad_start = pad_end - padded
    before = jnp.cumsum(h, 0) - h
    base = (pad_start[None, :] + before).astype(F32)[:, None, :]
    return base, pad_end


def _rot_cols(w):
    q = ROPE // 4
    return jnp.concatenate([-w[..., q:2 * q], w[..., 0:q], -w[..., 3 * q:4 * q], w[..., 2 * q:3 * q]], axis=-1)


def _rope_tables(n):
    rows = n // GRID_W
    row_id = jnp.repeat(jnp.arange(rows, dtype=F32), GRID_W)
    col_id = jnp.tile(jnp.arange(GRID_W, dtype=F32), rows)
    half = ROPE // 2
    inv_freq = ROPE_THETA ** (-jnp.arange(0, half, 2, dtype=F32) / half)
    ang_r = row_id[:, None] * inv_freq
    ang_c = col_id[:, None] * inv_freq
    cos = jnp.concatenate([jnp.cos(ang_r)] * 2 + [jnp.cos(ang_c)] * 2, axis=-1)
    sin = jnp.concatenate([jnp.sin(ang_r)] * 2 + [jnp.sin(ang_c)] * 2, axis=-1)
    return cos, sin


def _pad_rows(w, rows):
    return jnp.pad(w, [(0, 0)] * (w.ndim - 2) + [(0, rows - w.shape[-2]), (0, 0)])


def kernel(x_prompt, x_sample, cache_ckv, cache_krope, state_rwkv_fwd, state_rwkv_bwd, c, c_ctx, w_mod, b_mod, w_in, gmlp_ln_g, gmlp_ln_b, gmlp_ws, gmlp_bs, q_norm_g, w_uq, kv_norm_g, w_ukv, rwkv_mu, rwkv_w0, rwkv_w2, rwkv_a0, rwkv_a2, rwkv_g2, rwkv_kk, rwkv_ka, rwkv_rk, rwkv_lnx_g, rwkv_lnx_b, w_branch, w_o, ln1_g, ln1_b, router_w, router_b, w_gate, b_gate, w_up, b_up, w_down, b_down, ln2_g, ln2_b):
    L = w_mod.shape[0]
    b_ctx, seq, _ = x_prompt.shape
    b_lat, t_lat, _ = x_sample.shape
    past = cache_ckv.shape[2]
    n_ctx = b_ctx * seq
    n_lat = b_lat * t_lat
    n = n_ctx + n_lat
    assert seq % TB == 0 and t_lat % TB == 0 and past % TB == 0 and b_lat + 1 <= 8 and n_ctx % t_lat == 0
    nb_ctx = n_ctx // TB
    bps_lat = t_lat // TB

    def modrow(i):
        return jnp.where(i < nb_ctx, 0, 1 + (i - nb_ctx) // bps_lat)

    cond8 = jnp.zeros((8, D), F32).at[0].set(c_ctx).at[1:1 + b_lat].set(c)
    mod_all = modulation(cond8, w_mod, b_mod).reshape(L, 8, 6, D)

    gm_end = 2 * D
    mla_end = gm_end + Q_LORA + KV_LORA + ROPE
    rw_end = mla_end + 3 * D + 2 * DECAY_LORA + 2 * AAA_LORA + GATE_LORA
    w_gm = w_in[:, :, :gm_end].astype(BF16)
    w_kr = w_in[:, :, mla_end - ROPE:mla_end]
    w_mla = jnp.concatenate([w_in[:, :, gm_end:mla_end], _rot_cols(w_kr)], axis=-1).astype(BF16)
    rw = w_in[:, :, mla_end:rw_end]
    segs = [rw[:, :, :3 * D]]
    mu_segs = [rwkv_mu[:, :3 * D]]
    o = 3 * D
    for width in (DECAY_LORA, DECAY_LORA, AAA_LORA, AAA_LORA):
        segs.append(jnp.pad(rw[:, :, o:o + width], ((0, 0), (0, 0), (0, LORA_PAD - width))))
        mu_segs.append(jnp.pad(rwkv_mu[:, o:o + width], ((0, 0), (0, LORA_PAD - width))))
        o += width
    segs.append(rw[:, :, o:])
    mu_segs.append(rwkv_mu[:, o:])
    w_rw = jnp.concatenate(segs, axis=-1).astype(BF16)
    mu_rw = jnp.concatenate(mu_segs, axis=-1)[:, None, :]
    w_gt = w_in[:, :, rw_end:].astype(BF16)
    w2p = _pad_rows(rwkv_w2, LORA_PAD).astype(BF16)
    a2p = _pad_rows(rwkv_a2, LORA_PAD).astype(BF16)
    g2b = rwkv_g2.astype(BF16)
    wq4 = w_uq.reshape(L, Q_LORA, HEADS, QK_DIM)
    wq_r = wq4[..., NOPE:]
    wq = jnp.concatenate([wq4[..., :NOPE], wq_r, _rot_cols(wq_r)], axis=-1)
    wq = wq.reshape(L, Q_LORA, HEADS * 2 * LANES).astype(BF16)
    wkv4 = w_ukv.reshape(L, KV_LORA, HEADS, NOPE + V_DIM)
    wk = wkv4[..., :NOPE].reshape(L, KV_LORA, D).astype(BF16)
    wv = wkv4[..., NOPE:].reshape(L, KV_LORA, D).astype(BF16)
    ws_b = gmlp_ws.astype(BF16)
    bs_full = jnp.repeat(jnp.swapaxes(gmlp_bs, 1, 2), GMLP_GW, axis=2)
    w_br = w_branch.astype(BF16)
    w_ob = w_o.astype(BF16)
    rw_pad = jnp.pad(router_w, ((0, 0), (0, 0), (0, LANES - N_EXPERTS)))
    rb_pad = jnp.pad(router_b, ((0, 0), (0, LANES - N_EXPERTS)), constant_values=-1e30)[:, None, :]
    bg4 = b_gate[:, :, None, :]
    bu4 = b_up[:, :, None, :]
    bd4 = b_down[:, :, None, :]

    cos, sin = _rope_tables(t_lat)
    cos_all = jnp.concatenate([jnp.ones((n_ctx, ROPE), F32), jnp.tile(cos, (b_lat, 1))], axis=0)
    sin_all = jnp.concatenate([jnp.zeros((n_ctx, ROPE), F32), jnp.tile(sin, (b_lat, 1))], axis=0)
    tabk = jnp.concatenate([cos_all, sin_all], axis=-1)
    tabq = jnp.concatenate([jnp.ones((n, NOPE), F32), tabk], axis=-1)
    tab_cache = jnp.concatenate([jnp.ones((b_lat * past, ROPE), F32), jnp.zeros((b_lat * past, ROPE), F32)], -1)

    zeros_state = jnp.zeros((b_ctx, RHEADS, RH, RH), F32)
    sf_lat = state_rwkv_fwd
    sb_lat = state_rwkv_bwd

    cap = (n * TOP_K // BM + N_EXPERTS) * BM
    x = jnp.concatenate([x_prompt.reshape(n_ctx, D), x_sample.reshape(n_lat, D)], axis=0)
    h = ln_modulate(x, mod_all[0], modrow)
    ckv_out, kr_out, sf_out, sb_out = [], [], [], []
    for l in range(L):
        mod = mod_all[l]
        p_gm = matmul(h, w_gm, l, 1024, "in_proj_gmlp", BF16)
        p_mla = matmul(h, w_mla, l, MLA_COLS, "in_proj_mla")
        p_rw = matmul(h, w_rw, l, 768, "in_proj_rwkv")
        p_gate = matmul(h, w_gt, l, 1024, "in_proj_gate", BF16)

        o_a = gmlp(p_gm, gmlp_ln_g[l][None], gmlp_ln_b[l][None], ws_b[l], bs_full[l])

        q = q_proj(p_mla, q_norm_g[l][None], wq[l], tabq)
        kcat, vv, ckvn = kv_proj(p_mla, 1, p_mla, (Q_LORA + KV_LORA) // LANES, kv_norm_g[l][None],
                                 wk[l], wv[l], tabk, True)
        cache_c = cache_ckv[:, l].reshape(b_lat * past, KV_LORA)
        cache_r = jnp.pad(cache_krope[:, l].reshape(b_lat * past, ROPE), ((0, 0), (0, ROPE)))
        kcat_c, vv_c, _ = kv_proj(cache_c, 0, cache_r, 0, kv_norm_g[l][None], wk[l], wv[l], tab_cache, False)
        o_b_ctx = attention(q, 0, kcat, vv, b_ctx, seq, seq, HEADS, "mla_attention_ctx")
        o_b_lat = attention_lat(q, n_ctx, kcat_c, vv_c, kcat, vv, b_lat, t_lat, past, 4)
        o_b = jnp.concatenate([o_b_ctx, o_b_lat], axis=0)
        ckv_out.append(ckvn[:n_ctx].reshape(b_ctx, seq, KV_LORA))
        kr_out.append(p_mla[:n_ctx, Q_LORA + KV_LORA:Q_LORA + KV_LORA + ROPE].reshape(b_ctx, seq, ROPE))

        (r, v, kk, lwf, lwb, kdf, kdb, bqf, bqb, bonus, gg) = rwkv_prep(
            p_rw, mu_rw[l], rwkv_kk[l][None], rwkv_w0[l], w2p[l], rwkv_a0[l], a2p[l], rwkv_ka[l],
            rwkv_rk[l].reshape(1, D), g2b[l], n_ctx // RB, seq // RB, t_lat // RB)
        yf_c, yb_c, sf_c, sb_c = rwkv_scan(r, v, kk, lwf, lwb, kdf, kdb, bqf, bqb, zeros_state, zeros_state,
                                           0, b_ctx, seq, "rwkv_scan_ctx")
        yf_l, yb_l, _, _ = rwkv_scan(r, v, kk, lwf, lwb, kdf, kdb, bqf, bqb, sf_lat[:, l], sb_lat[:, l],
                                     n_ctx, b_lat, t_lat, "rwkv_scan_lat")
        sf_out.append(sf_c)
        sb_out.append(sb_c)
        o_c = rwkv_finish(yf_c, yb_c, yf_l, yb_l, bonus, gg, rwkv_lnx_g[l][None], rwkv_lnx_b[l][None])

        z = branch_merge(o_a, o_b, o_c, w_br, l, p_gate)
        x1, h2, h2p = out_proj(z, w_ob, l, x, mod, modrow, ln1_g[l][None], ln1_b[l][None])

        idx, wts, hist = router(h2, rw_pad[l], rb_pad[l])
        base, pad_end = moe_layout(hist)
        dest = moe_dest(idx, base)
        dest_flat = dest[:, :TOP_K].reshape(n // TB, 1, TB * TOP_K)
        xb = moe_dispatch(dest_flat, h2p, jnp.zeros((cap, D // 2), jnp.uint32))
        n_valid = (pad_end[N_EXPERTS - 1] // BM).astype(jnp.int32)
        blk_start = jnp.minimum(jnp.arange(cap // BM, dtype=jnp.int32), n_valid - 1) * BM
        block_e = jnp.sum((pad_end[None, :N_EXPERTS] <= blk_start[:, None]).astype(jnp.int32), axis=1)
        block_e = jnp.minimum(block_e, N_EXPERTS - 1)
        nv = n_valid.reshape(1)
        act = expert_up(block_e, nv, xb, w_gate, bg4, w_up, bu4, l)
        yb = expert_down(block_e, nv, act, w_down, bd4, l)
        mod_next = mod_all[min(l + 1, L - 1)]
        x, h = moe_combine(dest_flat, wts, x1, mod, mod_next, modrow, ln2_g[l][None], ln2_b[l][None], yb)

    y_prompt = x[:n_ctx].reshape(b_ctx, seq, D)
    y_sample = x[n_ctx:].reshape(b_lat, t_lat, D)
    return (y_prompt, y_sample, jnp.stack(ckv_out, axis=1), jnp.stack(kr_out, axis=1),
            jnp.stack(sf_out, axis=1), jnp.stack(sb_out, axis=1))
```
